```python
import math
import jax
import jax.numpy as jnp
from jax import lax
import numpy as np

D_MODEL = 4096
BATCH = 32
SEQ = 256
DEPTH = 4
DEC_BATCH = 8
DEC_SEQ = 1024
PAST_LEN = 512

GRID_W = 64
HEAD_DIM = 128
MIX_WIDTH = D_MODEL
GROUP_WIDTH = MIX_WIDTH // 4
NAT_HEADS = GROUP_WIDTH // HEAD_DIM
NAT_KR = 8
NAT_KC = 16
NAT_QC = 16
NAT_KSPAN = NAT_QC + NAT_KC
SWA_HEADS = GROUP_WIDTH // HEAD_DIM
SWA_KV_HEADS = 2
SWA_GROUP = SWA_HEADS // SWA_KV_HEADS
SWA_WINDOW = 128
MLA_NOPE = 128
MLA_ROPE = 64
MLA_V = 128
MLA_HEADS = GROUP_WIDTH // MLA_V
MLA_Q_LORA = 896
MLA_KV_LORA = 256
DIFF_HEADS = GROUP_WIDTH // (2 * HEAD_DIM)

ROPE_BASE = 10000.0
EPS = 1e-6
NEG_INF = -1e30
Q_BLOCK = 128

IN_SPLITS = (
    NAT_HEADS * HEAD_DIM, NAT_HEADS * HEAD_DIM, NAT_HEADS * HEAD_DIM,
    SWA_HEADS * HEAD_DIM, SWA_KV_HEADS * HEAD_DIM, SWA_KV_HEADS * HEAD_DIM,
    MLA_Q_LORA, MLA_KV_LORA, MLA_ROPE,
    DIFF_HEADS * 2 * HEAD_DIM, DIFF_HEADS * 2 * HEAD_DIM, DIFF_HEADS * 2 * HEAD_DIM,
    MIX_WIDTH,
)
IN_COLS = sum(IN_SPLITS)

kernel_name = 'hybrid_diffusion_parallel_heads_step'


def rmsnorm(x, g):
    xf = x.astype(jnp.float32)
    y = xf * lax.rsqrt(jnp.mean(xf * xf, axis=-1, keepdims=True) + EPS)
    return (y * g.astype(jnp.float32)).astype(x.dtype)


def split_columns(proj):
    offsets = []
    acc = 0
    for w in IN_SPLITS[:-1]:
        acc += w
        offsets.append(acc)
    return jnp.split(proj, offsets, axis=-1)


def grid_angles(n_tokens, rot_dim):
    axis_dim = rot_dim // 2
    inv = 1.0 / (ROPE_BASE ** (jnp.arange(0, axis_dim, 2, dtype=jnp.float32) / axis_dim))
    t = jnp.arange(n_tokens)
    row = (t // GRID_W).astype(jnp.float32)
    col = (t % GRID_W).astype(jnp.float32)
    return row[:, None] * inv, col[:, None] * inv


def _rotate_half(x, ang):
    x1, x2 = jnp.split(x, 2, axis=-1)
    cos = jnp.cos(ang).astype(x.dtype)
    sin = jnp.sin(ang).astype(x.dtype)
    return jnp.concatenate([x1 * cos - x2 * sin, x2 * cos + x1 * sin], axis=-1)


def rope_2d(x, ang_row, ang_col):
    shape = (ang_row.shape[0],) + (1,) * (x.ndim - 3) + (ang_row.shape[1],)
    xr, xc = jnp.split(x, 2, axis=-1)
    return jnp.concatenate([_rotate_half(xr, ang_row.reshape(shape)),
                            _rotate_half(xc, ang_col.reshape(shape))], axis=-1)


def dense_attention(q, k, v, scale, sink=None):
    b, sq, kvh, g, dk = q.shape
    nqb = sq // Q_BLOCK
    qb = jnp.moveaxis(q.reshape(b, nqb, Q_BLOCK, kvh, g, dk), 1, 0)

    def block(qi):
        s = jnp.einsum('bqhgd,bkhd->bhgqk', qi, k).astype(jnp.float32) * scale
        if sink is not None:
            sk = jnp.broadcast_to(sink.astype(jnp.float32)[None, :, :, None, None],
                                  s.shape[:-1] + (1,))
            p = jax.nn.softmax(jnp.concatenate([s, sk], axis=-1), axis=-1)[..., :-1]
        else:
            p = jax.nn.softmax(s, axis=-1)
        return jnp.einsum('bhgqk,bkhd->bqhgd', p.astype(v.dtype), v)

    out = lax.map(block, qb)
    return jnp.moveaxis(out, 0, 1).reshape(b, sq, kvh, g, v.shape[-1])


def nat_latent(q, k, v, ctx_k, ctx_v, rpb):
    b, s, h, d = q.shape
    rows = s // GRID_W
    kr = min(NAT_KR, rows)
    ncb = GRID_W // NAT_QC
    r = jnp.arange(rows)
    row_idx = jnp.clip(r - kr // 2, 0, rows - kr)[:, None] + jnp.arange(kr)
    qcol = jnp.arange(GRID_W).reshape(ncb, NAT_QC)
    win_start = jnp.clip(qcol - NAT_KC // 2, 0, GRID_W - NAT_KC)
    blk_start = jnp.minimum(win_start[:, 0], GRID_W - NAT_KSPAN)
    col_idx = blk_start[:, None] + jnp.arange(NAT_KSPAN)
    gi_r = row_idx[:, None, :, None]
    gi_c = col_idx[None, :, None, :]
    nk = kr * NAT_KSPAN
    k_nb = k.reshape(b, rows, GRID_W, h, d)[:, gi_r, gi_c].reshape(b, rows, ncb, nk, h, d)
    v_nb = v.reshape(b, rows, GRID_W, h, d)[:, gi_r, gi_c].reshape(b, rows, ncb, nk, h, d)
    qg = q.reshape(b, rows, ncb, NAT_QC, h, d)
    scale = d ** -0.5
    s_nb = jnp.einsum('brnqhd,brnkhd->bhrnqk', qg, k_nb).astype(jnp.float32) * scale
    dr = row_idx - r[:, None] + (NAT_KR - 1)
    dcol = col_idx[:, None, :] - qcol[:, :, None]
    dc = jnp.clip(dcol, -(NAT_KC - 1), NAT_KC - 1) + (NAT_KC - 1)
    bias = rpb.astype(jnp.float32)[:, dr[:, None, None, :, None], dc[None, :, :, None, :]]
    inwin = (col_idx[:, None, :] >= win_start[:, :, None]) & (col_idx[:, None, :] < win_start[:, :, None] + NAT_KC)
    bias = jnp.where(inwin[None, :, :, None, :], bias, NEG_INF).reshape(h, rows, ncb, NAT_QC, nk)
    s_nb = s_nb + bias[None]
    s_ctx = jnp.einsum('brnqhd,bchd->bhrnqc', qg, ctx_k).astype(jnp.float32) * scale
    p = jax.nn.softmax(jnp.concatenate([s_nb, s_ctx], axis=-1), axis=-1).astype(v.dtype)
    out = (jnp.einsum('bhrnqk,brnkhd->brnqhd', p[..., :nk], v_nb)
           + jnp.einsum('bhrnqc,bchd->brnqhd', p[..., nk:], ctx_v))
    return out.reshape(b, s, h * d)


def swa_latent(q, k, v, ctx_k, ctx_v, sink):
    b, s, kvh, g, d = q.shape
    w = SWA_WINDOW
    nb = s // w
    pad = ((0, 0), (w, w), (0, 0), (0, 0))

    def band(x):
        xb = jnp.pad(x, pad).reshape(b, nb + 2, w, kvh, d)
        return jnp.concatenate([xb[:, :-2], xb[:, 1:-1], xb[:, 2:]], axis=2)

    k_band, v_band = band(k), band(v)
    qb = q.reshape(b, nb, w, kvh, g, d)
    scale = d ** -0.5
    s_band = jnp.einsum('bnqhgd,bnkhd->bhgnqk', qb, k_band).astype(jnp.float32) * scale
    qpos = jnp.arange(s).reshape(nb, w)
    kpos = (jnp.arange(nb)[:, None] - 1) * w + jnp.arange(3 * w)
    ok = ((jnp.abs(qpos[:, :, None] - kpos[:, None, :]) <= w)
          & (kpos >= 0)[:, None, :] & (kpos < s)[:, None, :])
    s_band = jnp.where(ok, s_band, NEG_INF)
    s_ctx = jnp.einsum('bnqhgd,bchd->bhgnqc', qb, ctx_k).astype(jnp.float32) * scale
    s_sink = jnp.broadcast_to(sink.astype(jnp.float32)[None, :, :, None, None, None],
                              s_band.shape[:-1] + (1,))
    p = jax.nn.softmax(jnp.concatenate([s_band, s_ctx, s_sink], axis=-1), axis=-1).astype(v.dtype)
    nk = 3 * w
    lc = ctx_k.shape[1]
    out = (jnp.einsum('bhgnqk,bnkhd->bnqhgd', p[..., :nk], v_band)
           + jnp.einsum('bhgnqc,bchd->bnqhgd', p[..., nk:nk + lc], ctx_v))
    return out.reshape(b, s, kvh * g * d)


def mla_queries(cq, g_q, w_uq, angles):
    b, s, _ = cq.shape
    q = (rmsnorm(cq, g_q) @ w_uq).reshape(b, s, MLA_HEADS, MLA_NOPE + MLA_ROPE)
    q_nope, q_pe = q[..., :MLA_NOPE], q[..., MLA_NOPE:]
    if angles is not None:
        q_pe = rope_2d(q_pe, *angles)
    return jnp.concatenate([q_nope, q_pe], axis=-1)


def mla_attend(q, ckv_all, kpe_all, w_ukv):
    b, s = q.shape[0], q.shape[1]
    lk = ckv_all.shape[1]
    kv = (ckv_all @ w_ukv).reshape(b, lk, MLA_HEADS, MLA_NOPE + MLA_V)
    k_nope, v = kv[..., :MLA_NOPE], kv[..., MLA_NOPE:]
    k_pe = jnp.broadcast_to(kpe_all[:, :, None, :], (b, lk, MLA_HEADS, MLA_ROPE))
    k = jnp.concatenate([k_nope, k_pe], axis=-1)
    out = dense_attention(q[:, :, :, None, :], k, v, (MLA_NOPE + MLA_ROPE) ** -0.5)
    return out.reshape(b, s, MLA_HEADS * MLA_V)


def diff_attend(q, k_all, v_all, lam_params, g_subln, lam_init):
    b, s = q.shape[0], q.shape[1]
    lp = lam_params.astype(jnp.float32)
    lam = jnp.exp(jnp.sum(lp[0] * lp[1])) - jnp.exp(jnp.sum(lp[2] * lp[3])) + lam_init
    scale = HEAD_DIM ** -0.5
    o1 = dense_attention(q[:, :, :, 0, :][:, :, :, None, :], k_all[:, :, :, 0, :], v_all, scale)
    o2 = dense_attention(q[:, :, :, 1, :][:, :, :, None, :], k_all[:, :, :, 1, :], v_all, scale)
    o = (o1 - lam.astype(o1.dtype) * o2)[:, :, :, 0, :]
    o = rmsnorm(o, g_subln) * (1.0 - lam_init)
    return o.reshape(b, s, DIFF_HEADS * 2 * HEAD_DIM)


def _open_sublayer(x, cond, p):
    shift, scale, gate = jnp.split(jax.nn.silu(cond) @ p['w_mod'] + p['b_mod'], 3, axis=-1)
    h = rmsnorm(x, p['g_pre']) * (1 + scale) + shift
    return split_columns(h @ p['w_in']), gate


def _close_sublayer(x, mixed, gpath, gate, p):
    o = (mixed * jax.nn.silu(gpath)) @ p['w_out']
    return x + gate * rmsnorm(o, p['g_post'])


def context_layer(x, c_ctx, p, lam_init):
    b, s, _ = x.shape
    parts, gate = _open_sublayer(x, c_ctx[None, None, :], p)
    nat_q, nat_k, nat_v, swa_q, swa_k, swa_v, cq, ckv, kpe, dq, dk, dv, gpath = parts
    scale = HEAD_DIM ** -0.5
    nat_k = nat_k.reshape(b, s, NAT_HEADS, HEAD_DIM)
    nat_v = nat_v.reshape(b, s, NAT_HEADS, HEAD_DIM)
    o_a = dense_attention(nat_q.reshape(b, s, NAT_HEADS, 1, HEAD_DIM), nat_k, nat_v, scale)
    o_a = o_a.reshape(b, s, GROUP_WIDTH)
    swa_k = swa_k.reshape(b, s, SWA_KV_HEADS, HEAD_DIM)
    swa_v = swa_v.reshape(b, s, SWA_KV_HEADS, HEAD_DIM)
    o_b = dense_attention(swa_q.reshape(b, s, SWA_KV_HEADS, SWA_GROUP, HEAD_DIM), swa_k, swa_v, scale,
                          sink=p['swa_sink'].reshape(SWA_KV_HEADS, SWA_GROUP))
    o_b = o_b.reshape(b, s, GROUP_WIDTH)
    ckv = rmsnorm(ckv, p['mla_g_kv'])
    o_c = mla_attend(mla_queries(cq, p['mla_g_q'], p['mla_w_uq'], None), ckv, kpe, p['mla_w_ukv'])
    dk = dk.reshape(b, s, DIFF_HEADS, 2 * HEAD_DIM)
    dv = dv.reshape(b, s, DIFF_HEADS, 2 * HEAD_DIM)
    o_d = diff_attend(dq.reshape(b, s, DIFF_HEADS, 2, HEAD_DIM), dk.reshape(b, s, DIFF_HEADS, 2, HEAD_DIM),
                      dv, p['diff_lambda'], p['diff_g_subln'], lam_init)
    x = _close_sublayer(x, jnp.concatenate([o_a, o_b, o_c, o_d], axis=-1), gpath, gate, p)
    return x, (nat_k, nat_v, swa_k, swa_v, ckv, kpe, dk, dv)


def latent_layer(x, c, cache, p, lam_init):
    b, s, _ = x.shape
    parts, gate = _open_sublayer(x, c[:, None, :], p)
    nat_q, nat_k, nat_v, swa_q, swa_k, swa_v, cq, ckv, kpe, dq, dk, dv, gpath = parts
    ctx_nat_k, ctx_nat_v, ctx_swa_k, ctx_swa_v, ctx_ckv, ctx_kpe, ctx_dk, ctx_dv = cache
    lc = ctx_ckv.shape[1]
    ang = grid_angles(s, HEAD_DIM)
    ang_pe = grid_angles(s, MLA_ROPE)
    o_a = nat_latent(nat_q.reshape(b, s, NAT_HEADS, HEAD_DIM), nat_k.reshape(b, s, NAT_HEADS, HEAD_DIM),
                     nat_v.reshape(b, s, NAT_HEADS, HEAD_DIM), ctx_nat_k, ctx_nat_v, p['nat_rpb'])
    swa_q = rope_2d(swa_q.reshape(b, s, SWA_KV_HEADS, SWA_GROUP, HEAD_DIM), *ang)
    swa_k = rope_2d(swa_k.reshape(b, s, SWA_KV_HEADS, HEAD_DIM), *ang)
    o_b = swa_latent(swa_q, swa_k, swa_v.reshape(b, s, SWA_KV_HEADS, HEAD_DIM), ctx_swa_k, ctx_swa_v,
                     p['swa_sink'].reshape(SWA_KV_HEADS, SWA_GROUP))
    q_c = mla_queries(cq, p['mla_g_q'], p['mla_w_uq'], ang_pe)
    ckv = rmsnorm(ckv, p['mla_g_kv'])
    kpe = rope_2d(kpe, *ang_pe)
    o_c = mla_attend(q_c, jnp.concatenate([ctx_ckv, ckv], axis=1), jnp.concatenate([ctx_kpe, kpe], axis=1),
                     p['mla_w_ukv'])
    dq = rope_2d(dq.reshape(b, s, DIFF_HEADS, 2, HEAD_DIM), *ang)
    dk = rope_2d(dk.reshape(b, s, DIFF_HEADS, 2, HEAD_DIM), *ang)
    k_all = jnp.concatenate([ctx_dk.reshape(b, lc, DIFF_HEADS, 2, HEAD_DIM), dk], axis=1)
    v_all = jnp.concatenate([ctx_dv, dv.reshape(b, s, DIFF_HEADS, 2 * HEAD_DIM)], axis=1)
    o_d = diff_attend(dq, k_all, v_all, p['diff_lambda'], p['diff_g_subln'], lam_init)
    return _close_sublayer(x, jnp.concatenate([o_a, o_b, o_c, o_d], axis=-1), gpath, gate, p)


def setup_inputs(seed: int = 0) -> dict:
    key = jax.random.key(seed)
    ks = jax.random.split(key, 26)
    f32 = jnp.float32

    def nrm(k, shape, s):
        return jax.random.normal(k, shape, f32) * s

    return {
        'x_prompt': nrm(ks[0], (BATCH, SEQ, D_MODEL), 1.0),
        'x_sample': nrm(ks[1], (DEC_BATCH, DEC_SEQ, D_MODEL), 1.0),
        'cache_nat_k': nrm(ks[2], (DEC_BATCH, DEPTH, PAST_LEN, NAT_HEADS, HEAD_DIM), 1.0),
        'cache_nat_v': nrm(ks[3], (DEC_BATCH, DEPTH, PAST_LEN, NAT_HEADS, HEAD_DIM), 1.0),
        'cache_swa_k': nrm(ks[4], (DEC_BATCH, DEPTH, PAST_LEN, SWA_KV_HEADS, HEAD_DIM), 1.0),
        'cache_swa_v': nrm(ks[5], (DEC_BATCH, DEPTH, PAST_LEN, SWA_KV_HEADS, HEAD_DIM), 1.0),
        'cache_mla_ckv': nrm(ks[6], (DEC_BATCH, DEPTH, PAST_LEN, MLA_KV_LORA), 1.0),
        'cache_mla_kpe': nrm(ks[7], (DEC_BATCH, DEPTH, PAST_LEN, MLA_ROPE), 1.0),
        'cache_diff_k': nrm(ks[8], (DEC_BATCH, DEPTH, PAST_LEN, DIFF_HEADS, 2 * HEAD_DIM), 1.0),
        'cache_diff_v': nrm(ks[9], (DEC_BATCH, DEPTH, PAST_LEN, DIFF_HEADS, 2 * HEAD_DIM), 1.0),
        'c': nrm(ks[10], (DEC_BATCH, D_MODEL), 1.0),
        'c_ctx': nrm(ks[11], (D_MODEL,), 1.0),
        'w_mod': nrm(ks[12], (DEPTH, D_MODEL, 3 * D_MODEL), 0.2 * D_MODEL ** -0.5),
        'b_mod': nrm(ks[13], (DEPTH, 3 * D_MODEL), 0.02),
        'g_pre': 1.0 + nrm(ks[14], (DEPTH, D_MODEL), 0.01),
        'g_post': 1.0 + nrm(ks[15], (DEPTH, D_MODEL), 0.01),
        'w_in': nrm(ks[16], (DEPTH, D_MODEL, IN_COLS), D_MODEL ** -0.5),
        'w_out': nrm(ks[17], (DEPTH, MIX_WIDTH, D_MODEL), MIX_WIDTH ** -0.5),
        'nat_rpb': nrm(ks[18], (DEPTH, NAT_HEADS, 2 * NAT_KR - 1, 2 * NAT_KC - 1), 0.1),
        'swa_sink': nrm(ks[19], (DEPTH, SWA_HEADS), 0.5),
        'mla_g_q': 1.0 + nrm(ks[20], (DEPTH, MLA_Q_LORA), 0.01),
        'mla_g_kv': 1.0 + nrm(ks[21], (DEPTH, MLA_KV_LORA), 0.01),
        'mla_w_uq': nrm(ks[22], (DEPTH, MLA_Q_LORA, MLA_HEADS * (MLA_NOPE + MLA_ROPE)), MLA_Q_LORA ** -0.5),
        'mla_w_ukv': nrm(ks[23], (DEPTH, MLA_KV_LORA, MLA_HEADS * (MLA_NOPE + MLA_V)), MLA_KV_LORA ** -0.5),
        'diff_lambda': nrm(ks[24], (DEPTH, 4, HEAD_DIM), 0.1),
        'diff_g_subln': 1.0 + nrm(ks[25], (DEPTH, 2 * HEAD_DIM), 0.01),
    }


def reference(x_prompt, x_sample, cache_nat_k, cache_nat_v, cache_swa_k, cache_swa_v,
              cache_mla_ckv, cache_mla_kpe, cache_diff_k, cache_diff_v, c, c_ctx,
              w_mod, b_mod, g_pre, g_post, w_in, w_out, nat_rpb, swa_sink,
              mla_g_q, mla_g_kv, mla_w_uq, mla_w_ukv, diff_lambda, diff_g_subln):
    y_prompt = x_prompt
    y_sample = x_sample
    states = [[] for _ in range(8)]
    for l in range(DEPTH):
        p = {
            'w_mod': w_mod[l], 'b_mod': b_mod[l], 'g_pre': g_pre[l], 'g_post': g_post[l],
            'w_in': w_in[l], 'w_out': w_out[l], 'nat_rpb': nat_rpb[l], 'swa_sink': swa_sink[l],
            'mla_g_q': mla_g_q[l], 'mla_g_kv': mla_g_kv[l], 'mla_w_uq': mla_w_uq[l],
            'mla_w_ukv': mla_w_ukv[l], 'diff_lambda': diff_lambda[l], 'diff_g_subln': diff_g_subln[l],
        }
        lam_init = 0.8 - 0.6 * math.exp(-0.3 * l)
        y_prompt, ctx = context_layer(y_prompt, c_ctx, p, lam_init)
        for lst, t in zip(states, ctx):
            lst.append(t)
        cache_l = (cache_nat_k[:, l], cache_nat_v[:, l], cache_swa_k[:, l], cache_swa_v[:, l],
                   cache_mla_ckv[:, l], cache_mla_kpe[:, l], cache_diff_k[:, l], cache_diff_v[:, l])
        y_sample = latent_layer(y_sample, c, cache_l, p, lam_init)
    new_nat_k = jnp.stack(states[0], axis=1)
    new_nat_v = jnp.stack(states[1], axis=1)
    new_swa_k = jnp.stack(states[2], axis=1)
    new_swa_v = jnp.stack(states[3], axis=1)
    new_mla_ckv = jnp.stack(states[4], axis=1)
    new_mla_kpe = jnp.stack(states[5], axis=1)
    new_diff_k = jnp.stack(states[6], axis=1)
    new_diff_v = jnp.stack(states[7], axis=1)
    return (y_prompt, y_sample, new_nat_k, new_nat_v, new_swa_k, new_swa_v,
            new_mla_ckv, new_mla_kpe, new_diff_k, new_diff_v)
```

```python
import functools
import math

import jax
import jax.numpy as jnp
from jax import lax
from jax.experimental import pallas as pl
from jax.experimental.pallas import tpu as pltpu

F32 = jnp.float32
BF16 = jnp.bfloat16

D_MODEL = 4096
HEAD_DIM = 128
GRID_W = 64
SEQ = 256
DEC_SEQ = 1024
PAST_LEN = 512
NAT_KR, NAT_KC = 8, 16
SWA_WINDOW = 128
MLA_NOPE, MLA_ROPE, MLA_V = 128, 64, 128
MLA_Q_LORA, MLA_KV_LORA = 896, 256
ROPE_BASE = 10000.0
EPS = 1e-6
NEG_INF = -1e30
SCALE = HEAD_DIM ** -0.5
SCALE_MLA = (MLA_NOPE + MLA_ROPE) ** -0.5

C_GP = 0
C_CQ = 4096
C_KPE = 4992
C_SWQ = 5120
C_DQ = 6144
C_DK = 7168
C_DV = 8192
C_NQ = 9216
C_NK = 10240
C_NV = 11264
C_CKV = 12288
C_SWK = 12544
C_SWV = 12800
N_PROJ = 13056

ROW_BLOCK = 256
TQ = 256
VMEM_LIMIT = 56 * 1024 * 1024


def _cparams(n_axes):
    return pltpu.CompilerParams(dimension_semantics=("arbitrary",) * n_axes,
                                vmem_limit_bytes=VMEM_LIMIT)


def _silu(x):
    return x * (1.0 / (1.0 + jnp.exp(-x)))


def _nt(a, b):
    return lax.dot_general(a, b, (((1,), (1,)), ((), ())), preferred_element_type=F32)


def _softmax_pv(parts, extra=None):
    m = parts[0][0].max(axis=-1, keepdims=True)
    for s, _ in parts[1:]:
        m = jnp.maximum(m, s.max(axis=-1, keepdims=True))
    if extra is not None:
        m = jnp.maximum(m, extra)
    den = None
    acc = None
    for s, v in parts:
        p = jnp.exp(s - m)
        ps = p.sum(axis=-1, keepdims=True)
        pv = jnp.dot(p.astype(BF16), v, preferred_element_type=F32)
        den = ps if den is None else den + ps
        acc = pv if acc is None else acc + pv
    if extra is not None:
        den = den + jnp.exp(extra - m)
    return acc / den


def _rope(x, cos, sin_lo, sin_hi, half):
    lanes = x.shape[-1]
    return (x * cos + pltpu.roll(x, lanes - half, 1) * sin_lo + pltpu.roll(x, half, 1) * sin_hi)


def _mod_kernel(c_ref, w_ref, b_ref, o_ref):
    s = _silu(c_ref[...]).astype(BF16)
    o_ref[...] = jnp.dot(s, w_ref[...].astype(BF16), preferred_element_type=F32) + b_ref[...]


def _modulation(cond, w_mod, b_mod):
    depth = w_mod.shape[0]
    rows = cond.shape[0]
    tn = 512
    n = 3 * D_MODEL
    return pl.pallas_call(
        _mod_kernel,
        grid=(depth, n // tn),
        in_specs=[
            pl.BlockSpec((rows, D_MODEL), lambda l, j: (0, 0)),
            pl.BlockSpec((None, D_MODEL, tn), lambda l, j: (l, 0, j)),
            pl.BlockSpec((None, 1, tn), lambda l, j: (l, 0, j)),
        ],
        out_specs=pl.BlockSpec((None, rows, tn), lambda l, j: (l, 0, j)),
        out_shape=jax.ShapeDtypeStruct((depth, rows, n), F32),
        compiler_params=_cparams(2),
        name="modulation",
    )(cond, w_mod, b_mod.reshape(depth, 1, n))


def _norm_mod_kernel(x_ref, g_ref, m_ref, h_ref):
    x = x_ref[...]
    y = x * lax.rsqrt(jnp.mean(x * x, axis=-1, keepdims=True) + EPS) * g_ref[...]
    h_ref[...] = (y * (1.0 + m_ref[1:2, :]) + m_ref[0:1, :]).astype(BF16)


def _mod_spec(n_ctx):
    ctx_blocks = n_ctx // ROW_BLOCK
    per_req = DEC_SEQ // ROW_BLOCK

    def index(i):
        return (jnp.where(i < ctx_blocks, 0, 1 + (i - ctx_blocks) // per_req), 0, 0)

    return pl.BlockSpec((None, 3, D_MODEL), index)


def _norm_mod(x, g, mod, n_ctx):
    m = x.shape[0]
    return pl.pallas_call(
        _norm_mod_kernel,
        grid=(m // ROW_BLOCK,),
        in_specs=[
            pl.BlockSpec((ROW_BLOCK, D_MODEL), lambda i: (i, 0)),
            pl.BlockSpec((1, D_MODEL), lambda i: (0, 0)),
            _mod_spec(n_ctx),
        ],
        out_specs=pl.BlockSpec((ROW_BLOCK, D_MODEL), lambda i: (i, 0)),
        out_shape=jax.ShapeDtypeStruct((m, D_MODEL), BF16),
        compiler_params=_cparams(1),
        name="norm_mod",
    )(x, g.reshape(1, D_MODEL), mod)


def _post_kernel(x_ref, o_ref, g_ref, m_ref, y_ref):
    o = o_ref[...]
    n = o * lax.rsqrt(jnp.mean(o * o, axis=-1, keepdims=True) + EPS) * g_ref[...]
    y_ref[...] = x_ref[...] + m_ref[2:3, :] * n


def _post(x, o, g, mod, n_ctx):
    m = x.shape[0]
    return pl.pallas_call(
        _post_kernel,
        grid=(m // ROW_BLOCK,),
        in_specs=[
            pl.BlockSpec((ROW_BLOCK, D_MODEL), lambda i: (i, 0)),
            pl.BlockSpec((ROW_BLOCK, D_MODEL), lambda i: (i, 0)),
            pl.BlockSpec((1, D_MODEL), lambda i: (0, 0)),
            _mod_spec(n_ctx),
        ],
        out_specs=pl.BlockSpec((ROW_BLOCK, D_MODEL), lambda i: (i, 0)),
        out_shape=jax.ShapeDtypeStruct((m, D_MODEL), F32),
        compiler_params=_cparams(1),
        name="post_residual",
    )(x, o, g.reshape(1, D_MODEL), mod)


def _rms_cols_kernel(x_ref, g_ref, y_ref, *, width):
    x = x_ref[:, :width]
    y = x * lax.rsqrt(jnp.mean(x * x, axis=-1, keepdims=True) + EPS) * g_ref[...]
    y_ref[...] = y.astype(y_ref.dtype)


def _rms_cols(p, col, block_w, width, g, out_dtype):
    m = p.shape[0]
    tm = min(1024, m)
    return pl.pallas_call(
        functools.partial(_rms_cols_kernel, width=width),
        grid=(m // tm,),
        in_specs=[
            pl.BlockSpec((tm, block_w), lambda i: (i, col // block_w)),
            pl.BlockSpec((1, width), lambda i: (0, 0)),
        ],
        out_specs=pl.BlockSpec((tm, width), lambda i: (i, 0)),
        out_shape=jax.ShapeDtypeStruct((m, width), out_dtype),
        compiler_params=_cparams(1),
        name="rms_cols",
    )(p, g.reshape(1, width))


def _mm_kernel(a_ref, b_ref, o_ref):
    o_ref[...] = jnp.dot(a_ref[...], b_ref[...], preferred_element_type=F32).astype(o_ref.dtype)


def _matmul(a, b, tn, out_dtype=F32, name="matmul"):
    m, k = a.shape
    n = b.shape[1]
    tm = 1024 if m % 1024 == 0 else 512
    return pl.pallas_call(
        _mm_kernel,
        grid=(m // tm, n // tn),
        in_specs=[
            pl.BlockSpec((tm, k), lambda i, j: (i, 0)),
            pl.BlockSpec((k, tn), lambda i, j: (0, j)),
        ],
        out_specs=pl.BlockSpec((tm, tn), lambda i, j: (i, j)),
        out_shape=jax.ShapeDtypeStruct((m, n), out_dtype),
        compiler_params=_cparams(2),
        name=name,
    )(a, b)


def _gate_store(o_ref, o, gp, sl):
    o_ref[:, sl] = (o * _silu(gp)).astype(o_ref.dtype)


def _ctx_a_kernel(q_ref, k_ref, v_ref, gp_ref, o_ref):
    for h in range(8):
        sl = slice(h * HEAD_DIM, (h + 1) * HEAD_DIM)
        s = _nt(q_ref[:, sl].astype(BF16), k_ref[:, sl].astype(BF16)) * SCALE
        o = _softmax_pv([(s, v_ref[:, sl].astype(BF16))])
        _gate_store(o_ref, o, gp_ref[:, sl], sl)


def _ctx_b_kernel(q_ref, k_ref, v_ref, sink_ref, gp_ref, o_ref):
    for h in range(8):
        sl = slice(h * HEAD_DIM, (h + 1) * HEAD_DIM)
        kv = slice((h // 4) * HEAD_DIM, (h // 4 + 1) * HEAD_DIM)
        s = _nt(q_ref[:, sl].astype(BF16), k_ref[:, kv].astype(BF16)) * SCALE
        sink = sink_ref[:, h * HEAD_DIM:h * HEAD_DIM + 1]
        o = _softmax_pv([(s, v_ref[:, kv].astype(BF16))], extra=sink)
        _gate_store(o_ref, o, gp_ref[:, sl], sl)


def _ctx_c_kernel(qn_ref, qp_ref, kv_ref, kpe_ref, gp_ref, o_ref):
    kpe = kpe_ref[...].astype(BF16)
    for h in range(8):
        sl = slice(h * HEAD_DIM, (h + 1) * HEAD_DIM)
        kn = kv_ref[:, 2 * h * HEAD_DIM:(2 * h + 1) * HEAD_DIM].astype(BF16)
        v = kv_ref[:, (2 * h + 1) * HEAD_DIM:(2 * h + 2) * HEAD_DIM].astype(BF16)
        s = (_nt(qn_ref[:, sl].astype(BF16), kn) + _nt(qp_ref[:, sl].astype(BF16), kpe)) * SCALE_MLA
        o = _softmax_pv([(s, v)])
        _gate_store(o_ref, o, gp_ref[:, sl], sl)


def _diff_lambda(lam_ref, lam_init):
    lp = lam_ref[...]
    a = jnp.sum(lp[0:1, :] * lp[1:2, :], axis=-1, keepdims=True)
    b = jnp.sum(lp[2:3, :] * lp[3:4, :], axis=-1, keepdims=True)
    return jnp.exp(a) - jnp.exp(b) + lam_init


def _diff_finish(o1, o2, lam, gsub, lam_init):
    o = o1 - lam * o2
    o = o * lax.rsqrt(jnp.mean(o * o, axis=-1, keepdims=True) + EPS) * gsub
    return o * (1.0 - lam_init)


def _ctx_d_kernel(q_ref, k_ref, v_ref, lam_ref, gsub_ref, gp_ref, o_ref, *, lam_init):
    lam = _diff_lambda(lam_ref, lam_init)
    gsub = gsub_ref[...]
    for h in range(4):
        c0 = h * 2 * HEAD_DIM
        s1, s2 = slice(c0, c0 + HEAD_DIM), slice(c0 + HEAD_DIM, c0 + 2 * HEAD_DIM)
        sl = slice(c0, c0 + 2 * HEAD_DIM)
        v = v_ref[:, sl].astype(BF16)
        o1 = _softmax_pv([(_nt(q_ref[:, s1].astype(BF16), k_ref[:, s1].astype(BF16)) * SCALE, v)])
        o2 = _softmax_pv([(_nt(q_ref[:, s2].astype(BF16), k_ref[:, s2].astype(BF16)) * SCALE, v)])
        o = _diff_finish(o1, o2, lam, gsub, lam_init)
        _gate_store(o_ref, o, gp_ref[:, sl], sl)


def _ctx_attention(p, q_up, kv_up, sink_lanes, lam, gsub, lam_init, n_ctx):
    nb = n_ctx // SEQ
    wide = 8 * HEAD_DIM

    def pblk(col, w=wide):
        return pl.BlockSpec((SEQ, w), lambda b: (b, col // w))

    out_shape = jax.ShapeDtypeStruct((n_ctx, wide), BF16)
    out_spec = pl.BlockSpec((SEQ, wide), lambda b: (b, 0))
    o_a = pl.pallas_call(
        _ctx_a_kernel, grid=(nb,),
        in_specs=[pblk(C_NQ), pblk(C_NK), pblk(C_NV), pblk(C_GP)],
        out_specs=out_spec, out_shape=out_shape, compiler_params=_cparams(1), name="ctx_nat",
    )(p, p, p, p)
    o_b = pl.pallas_call(
        _ctx_b_kernel, grid=(nb,),
        in_specs=[pblk(C_SWQ), pblk(C_SWK, 256), pblk(C_SWV, 256),
                  pl.BlockSpec((1, wide), lambda b: (0, 0)), pblk(C_GP + wide)],
        out_specs=out_spec, out_shape=out_shape, compiler_params=_cparams(1), name="ctx_swa",
    )(p, p, p, sink_lanes, p)
    o_c = pl.pallas_call(
        _ctx_c_kernel, grid=(nb,),
        in_specs=[pl.BlockSpec((SEQ, wide), lambda b: (b, 0)),
                  pl.BlockSpec((SEQ, wide), lambda b: (b, 1)),
                  pl.BlockSpec((SEQ, 2 * wide), lambda b: (b, 0)),
                  pblk(C_KPE, HEAD_DIM), pblk(C_GP + 2 * wide)],
        out_specs=out_spec, out_shape=out_shape, compiler_params=_cparams(1), name="ctx_mla",
    )(q_up, q_up, kv_up, p, p)
    o_d = pl.pallas_call(
        functools.partial(_ctx_d_kernel, lam_init=lam_init), grid=(nb,),
        in_specs=[pblk(C_DQ), pblk(C_DK), pblk(C_DV),
                  pl.BlockSpec((4, HEAD_DIM), lambda b: (0, 0)),
                  pl.BlockSpec((1, 2 * HEAD_DIM), lambda b: (0, 0)), pblk(C_GP + 3 * wide)],
        out_specs=out_spec, out_shape=out_shape, compiler_params=_cparams(1), name="ctx_diff",
    )(p, p, p, lam, gsub, p)
    return o_a, o_b, o_c, o_d


def _chunk(i):
    return pl.ds(pl.multiple_of(i * TQ, TQ), TQ)


def _lat_a_kernel(q_ref, k_ref, v_ref, ck_ref, cv_ref, bias_ref, gp_ref, o_ref):
    k = k_ref[...].astype(BF16)
    v = v_ref[...].astype(BF16)
    ck = ck_ref[...].astype(BF16)
    cv = cv_ref[...].astype(BF16)

    def body(i, carry):
        rows = _chunk(i)
        q = q_ref[rows, :].astype(BF16)
        s_lat = _nt(q, k) * SCALE + bias_ref[rows, :]
        s_ctx = _nt(q, ck) * SCALE
        o = _softmax_pv([(s_lat, v), (s_ctx, cv)])
        o_ref[rows, :] = (o * _silu(gp_ref[rows, :])).astype(o_ref.dtype)
        return carry

    lax.fori_loop(0, DEC_SEQ // TQ, body, 0)


def _band_mask(i):
    qi = i * TQ + lax.broadcasted_iota(jnp.int32, (TQ, DEC_SEQ), 0)
    kj = lax.broadcasted_iota(jnp.int32, (TQ, DEC_SEQ), 1)
    return jnp.abs(qi - kj) <= SWA_WINDOW


def _lat_b_kernel(q_ref, k_ref, v_ref, ck_ref, cv_ref, sink_ref, cos_ref, sl_ref, sh_ref,
                  gp_ref, o_ref):
    k = _rope(k_ref[...], cos_ref[...], sl_ref[...], sh_ref[...], 32).astype(BF16)
    v = v_ref[...].astype(BF16)
    ck = ck_ref[...].astype(BF16)
    cv = cv_ref[...].astype(BF16)
    sink = sink_ref[:, 0:1]

    def body(i, carry):
        rows = _chunk(i)
        q = _rope(q_ref[rows, :], cos_ref[rows, :], sl_ref[rows, :], sh_ref[rows, :], 32).astype(BF16)
        s_lat = jnp.where(_band_mask(i), _nt(q, k) * SCALE, NEG_INF)
        s_ctx = _nt(q, ck) * SCALE
        o = _softmax_pv([(s_lat, v), (s_ctx, cv)], extra=sink)
        o_ref[rows, :] = (o * _silu(gp_ref[rows, :])).astype(o_ref.dtype)
        return carry

    lax.fori_loop(0, DEC_SEQ // TQ, body, 0)


def _lat_c_kernel(qn_ref, qp_ref, kn_ref, v_ref, kpe_ref, ckn_ref, cv_ref, ckpe_ref,
                  cos_ref, sl_ref, sh_ref, gp_ref, o_ref):
    kn = kn_ref[...].astype(BF16)
    v = v_ref[...].astype(BF16)
    kpe = _rope(kpe_ref[...], cos_ref[...], sl_ref[...], sh_ref[...], 16).astype(BF16)
    ckn = ckn_ref[...].astype(BF16)
    cv = cv_ref[...].astype(BF16)
    ckpe = ckpe_ref[...].astype(BF16)

    def body(i, carry):
        rows = _chunk(i)
        qn = qn_ref[rows, :].astype(BF16)
        qp = _rope(qp_ref[rows, :], cos_ref[rows, :], sl_ref[rows, :], sh_ref[rows, :], 16).astype(BF16)
        s_lat = (_nt(qn, kn) + _nt(qp, kpe)) * SCALE_MLA
        s_ctx = (_nt(qn, ckn) + _nt(qp[:, :MLA_ROPE], ckpe)) * SCALE_MLA
        o = _softmax_pv([(s_ctx, cv), (s_lat, v)])
        o_ref[rows, :] = (o * _silu(gp_ref[rows, :])).astype(o_ref.dtype)
        return carry

    lax.fori_loop(0, DEC_SEQ // TQ, body, 0)


def _lat_d_kernel(q1_ref, q2_ref, k1_ref, k2_ref, v_ref, ck1_ref, ck2_ref, cv_ref,
                  lam_ref, gsub_ref, cos_ref, sl_ref, sh_ref, gp_ref, o_ref, *, lam_init):
    cos, sl, sh = cos_ref[...], sl_ref[...], sh_ref[...]
    k1 = _rope(k1_ref[...], cos, sl, sh, 32).astype(BF16)
    k2 = _rope(k2_ref[...], cos, sl, sh, 32).astype(BF16)
    v = v_ref[...].astype(BF16)
    ck1 = ck1_ref[...].astype(BF16)
    ck2 = ck2_ref[...].astype(BF16)
    cv = cv_ref[...].astype(BF16)
    lam = _diff_lambda(lam_ref, lam_init)
    gsub = gsub_ref[...]

    def body(i, carry):
        rows = _chunk(i)
        c, a, b = cos_ref[rows, :], sl_ref[rows, :], sh_ref[rows, :]
        q1 = _rope(q1_ref[rows, :], c, a, b, 32).astype(BF16)
        q2 = _rope(q2_ref[rows, :], c, a, b, 32).astype(BF16)
        o1 = _softmax_pv([(_nt(q1, ck1) * SCALE, cv), (_nt(q1, k1) * SCALE, v)])
        o2 = _softmax_pv([(_nt(q2, ck2) * SCALE, cv), (_nt(q2, k2) * SCALE, v)])
        o = _diff_finish(o1, o2, lam, gsub, lam_init)
        o_ref[rows, :] = (o * _silu(gp_ref[rows, :])).astype(o_ref.dtype)
        return carry

    lax.fori_loop(0, DEC_SEQ // TQ, body, 0)


def _lat_attention(p, q_up, kv_up, caches, layer, nat_bias, sink_lanes, lam, gsub, lam_init,
                   rope128, rope64, n_ctx, n_lat):
    nb = n_lat // DEC_SEQ
    rb0 = n_ctx // DEC_SEQ
    cb0 = (n_ctx + n_lat) // PAST_LEN
    c_nat_k, c_nat_v, c_swa_k, c_swa_v, c_kpe, c_dk, c_dv = caches
    hd = HEAD_DIM

    def rows(col_of, w=hd):
        return pl.BlockSpec((DEC_SEQ, w), lambda b, h: (rb0 + b, col_of(h)))

    def cache(col_of, w=hd):
        return pl.BlockSpec((None, None, PAST_LEN, w), lambda b, h: (b, layer, 0, col_of(h)))

    def table():
        return pl.BlockSpec((DEC_SEQ, hd), lambda b, h: (0, 0))

    def out_rows(w=hd):
        return pl.BlockSpec((DEC_SEQ, w), lambda b, h: (b, h))

    def out():
        return jax.ShapeDtypeStruct((n_lat, 8 * hd), BF16)

    o_a = pl.pallas_call(
        _lat_a_kernel, grid=(nb, 8),
        in_specs=[rows(lambda h: C_NQ // hd + h), rows(lambda h: C_NK // hd + h),
                  rows(lambda h: C_NV // hd + h), cache(lambda h: h), cache(lambda h: h),
                  pl.BlockSpec((None, DEC_SEQ, DEC_SEQ), lambda b, h: (h, 0, 0)),
                  rows(lambda h: C_GP // hd + h)],
        out_specs=out_rows(), out_shape=out(), compiler_params=_cparams(2), name="lat_nat",
    )(p, p, p, c_nat_k, c_nat_v, nat_bias, p)
    o_b = pl.pallas_call(
        _lat_b_kernel, grid=(nb, 8),
        in_specs=[rows(lambda h: C_SWQ // hd + h), rows(lambda h: C_SWK // hd + h // 4),
                  rows(lambda h: C_SWV // hd + h // 4), cache(lambda h: h // 4), cache(lambda h: h // 4),
                  pl.BlockSpec((1, hd), lambda b, h: (0, h)), table(), table(), table(),
                  rows(lambda h: C_GP // hd + 8 + h)],
        out_specs=out_rows(), out_shape=out(), compiler_params=_cparams(2), name="lat_swa",
    )(p, p, p, c_swa_k, c_swa_v, sink_lanes, *rope128, p)
    o_c = pl.pallas_call(
        _lat_c_kernel, grid=(nb, 8),
        in_specs=[rows(lambda h: h), rows(lambda h: 8 + h),
                  rows(lambda h: 2 * h), rows(lambda h: 2 * h + 1), rows(lambda h: C_KPE // hd),
                  pl.BlockSpec((PAST_LEN, hd), lambda b, h: (cb0 + b, 2 * h)),
                  pl.BlockSpec((PAST_LEN, hd), lambda b, h: (cb0 + b, 2 * h + 1)),
                  cache(lambda h: 0, MLA_ROPE), table(), table(), table(),
                  rows(lambda h: C_GP // hd + 16 + h)],
        out_specs=out_rows(), out_shape=out(), compiler_params=_cparams(2), name="lat_mla",
    )(q_up, q_up, kv_up, kv_up, p, kv_up, kv_up, c_kpe, *rope64, p)
    o_d = pl.pallas_call(
        functools.partial(_lat_d_kernel, lam_init=lam_init), grid=(nb, 4),
        in_specs=[rows(lambda h: C_DQ // hd + 2 * h), rows(lambda h: C_DQ // hd + 2 * h + 1),
                  rows(lambda h: C_DK // hd + 2 * h), rows(lambda h: C_DK // hd + 2 * h + 1),
                  rows(lambda h: C_DV // (2 * hd) + h, 2 * hd),
                  cache(lambda h: 2 * h), cache(lambda h: 2 * h + 1), cache(lambda h: h, 2 * hd),
                  pl.BlockSpec((4, hd), lambda b, h: (0, 0)),
                  pl.BlockSpec((1, 2 * hd), lambda b, h: (0, 0)), table(), table(), table(),
                  rows(lambda h: C_GP // (2 * hd) + 12 + h, 2 * hd)],
        out_specs=out_rows(2 * hd), out_shape=out(), compiler_params=_cparams(2),
        name="lat_diff",
    )(p, p, p, p, p, c_dk, c_dk, c_dv, lam, gsub, *rope128, p)
    return o_a, o_b, o_c, o_d


def _rope_tables(rot_dim):
    axis_dim = rot_dim // 2
    half = axis_dim // 2
    inv = 1.0 / (ROPE_BASE ** (jnp.arange(0, axis_dim, 2, dtype=F32) / axis_dim))
    t = jnp.arange(DEC_SEQ)
    ang_row = (t // GRID_W).astype(F32)[:, None] * inv
    ang_col = (t % GRID_W).astype(F32)[:, None] * inv
    zeros = jnp.zeros_like(ang_row)
    pad = HEAD_DIM - rot_dim
    cos = jnp.concatenate([jnp.cos(ang_row)] * 2 + [jnp.cos(ang_col)] * 2
                          + [jnp.ones((DEC_SEQ, pad), F32)], axis=-1)
    sin_lo = jnp.concatenate([-jnp.sin(ang_row), zeros, -jnp.sin(ang_col), zeros,
                              jnp.zeros((DEC_SEQ, pad), F32)], axis=-1)
    sin_hi = jnp.concatenate([zeros, jnp.sin(ang_row), zeros, jnp.sin(ang_col),
                              jnp.zeros((DEC_SEQ, pad), F32)], axis=-1)
    del half
    return cos, sin_lo, sin_hi


def _nat_bias(rpb):
    n_rows = DEC_SEQ // GRID_W
    r = jnp.arange(n_rows)
    c = jnp.arange(GRID_W)
    row0 = jnp.clip(r - NAT_KR // 2, 0, n_rows - NAT_KR)
    col0 = jnp.clip(c - NAT_KC // 2, 0, GRID_W - NAT_KC)
    ok_r = (r[None, :] >= row0[:, None]) & (r[None, :] < row0[:, None] + NAT_KR)
    ok_c = (c[None, :] >= col0[:, None]) & (c[None, :] < col0[:, None] + NAT_KC)
    dr = jnp.clip(r[None, :] - r[:, None], -(NAT_KR - 1), NAT_KR - 1) + (NAT_KR - 1)
    dc = jnp.clip(c[None, :] - c[:, None], -(NAT_KC - 1), NAT_KC - 1) + (NAT_KC - 1)
    sel_r = jax.nn.one_hot(dr, 2 * NAT_KR - 1, dtype=F32)
    sel_c = jax.nn.one_hot(dc, 2 * NAT_KC - 1, dtype=F32)
    bias = jnp.einsum("qkr,hrc,xyc->hqxky", sel_r, rpb.astype(F32), sel_c,
                      precision=lax.Precision.HIGHEST)
    ok = ok_r[:, None, :, None] & ok_c[None, :, None, :]
    bias = jnp.where(ok[None], bias, NEG_INF)
    return bias.reshape(rpb.shape[0], DEC_SEQ, DEC_SEQ)


def _pack_w_in(w_in):
    def seg(a, b):
        return w_in[..., a:b]
    zeros = jnp.zeros(w_in.shape[:-1] + (64,), w_in.dtype)
    parts = [seg(8896, 12992), seg(4608, 5504), seg(5760, 5824), zeros, seg(3072, 4096),
             seg(5824, 6848), seg(6848, 7872), seg(7872, 8896),
             seg(0, 1024), seg(1024, 2048), seg(2048, 3072),
             seg(5504, 5760), seg(4096, 4352), seg(4352, 4608)]
    return jnp.concatenate(parts, axis=-1).astype(BF16)


def _pack_w_uq(w_uq):
    depth = w_uq.shape[0]
    w = w_uq.reshape(depth, MLA_Q_LORA, 8, MLA_NOPE + MLA_ROPE)
    nope = w[..., :MLA_NOPE].reshape(depth, MLA_Q_LORA, 8 * MLA_NOPE)
    pe = jnp.pad(w[..., MLA_NOPE:], ((0, 0), (0, 0), (0, 0), (0, HEAD_DIM - MLA_ROPE)))
    return jnp.concatenate([nope, pe.reshape(depth, MLA_Q_LORA, 8 * HEAD_DIM)], axis=-1).astype(BF16)


def kernel(x_prompt, x_sample, cache_nat_k, cache_nat_v, cache_swa_k, cache_swa_v, cache_mla_ckv, cache_mla_kpe, cache_diff_k, cache_diff_v, c, c_ctx, w_mod, b_mod, g_pre, g_post, w_in, w_out, nat_rpb, swa_sink, mla_g_q, mla_g_kv, mla_w_uq, mla_w_ukv, diff_lambda, diff_g_subln):
    depth = w_mod.shape[0]
    bc, bl = x_prompt.shape[0], x_sample.shape[0]
    n_ctx, n_lat = bc * SEQ, bl * DEC_SEQ
    assert n_ctx % DEC_SEQ == 0 and x_prompt.shape[1] == SEQ and x_sample.shape[1] == DEC_SEQ

    x = jnp.concatenate([x_prompt.reshape(n_ctx, D_MODEL), x_sample.reshape(n_lat, D_MODEL)], axis=0)

    n_cond = -(-(1 + bl) // 8) * 8
    cond = jnp.concatenate([c_ctx[None], c, jnp.zeros((n_cond - 1 - bl, D_MODEL), F32)], axis=0)
    mod = _modulation(cond, w_mod, b_mod).reshape(depth, n_cond, 3, D_MODEL)

    w_in_p = _pack_w_in(w_in)
    w_out_b = w_out.astype(BF16)
    w_uq_p = _pack_w_uq(mla_w_uq)
    w_ukv_b = mla_w_ukv.astype(BF16)
    rope128 = _rope_tables(HEAD_DIM)
    rope64 = _rope_tables(MLA_ROPE)
    sink_lanes = jnp.repeat(swa_sink, HEAD_DIM, axis=-1).reshape(depth, 1, 8 * HEAD_DIM)

    caches = (cache_nat_k.reshape(bl, depth, PAST_LEN, -1), cache_nat_v.reshape(bl, depth, PAST_LEN, -1),
              cache_swa_k.reshape(bl, depth, PAST_LEN, -1), cache_swa_v.reshape(bl, depth, PAST_LEN, -1),
              cache_mla_kpe, cache_diff_k.reshape(bl, depth, PAST_LEN, -1),
              cache_diff_v.reshape(bl, depth, PAST_LEN, -1))

    states = [[] for _ in range(8)]
    for l in range(depth):
        lam_init = 0.8 - 0.6 * math.exp(-0.3 * l)
        h = _norm_mod(x, g_pre[l], mod[l], n_ctx)
        p = _matmul(h, w_in_p[l], 768, name="in_proj")
        cqn = _rms_cols(p, C_CQ, 1024, MLA_Q_LORA, mla_g_q[l], BF16)
        ckvn = _rms_cols(p, C_CKV, MLA_KV_LORA, MLA_KV_LORA, mla_g_kv[l], F32)
        q_up = _matmul(cqn, w_uq_p[l], 1024, name="mla_q_up")
        ckv_all = jnp.concatenate([ckvn, cache_mla_ckv[:, l].reshape(bl * PAST_LEN, MLA_KV_LORA)], axis=0)
        kv_up = _matmul(ckv_all.astype(BF16), w_ukv_b[l], 1024, name="mla_kv_up")

        lam = diff_lambda[l]
        gsub = diff_g_subln[l].reshape(1, 2 * HEAD_DIM)
        ctx_o = _ctx_attention(p, q_up, kv_up, sink_lanes[l], lam, gsub, lam_init, n_ctx)
        lat_o = _lat_attention(p, q_up, kv_up, caches, l, _nat_bias(nat_rpb[l]), sink_lanes[l], lam, gsub,
                               lam_init, rope128, rope64, n_ctx, n_lat)
        mixed = jnp.concatenate([jnp.concatenate(ctx_o, axis=-1), jnp.concatenate(lat_o, axis=-1)], axis=0)
        o = _matmul(mixed, w_out_b[l], 1024, name="out_proj")
        x = _post(x, o, g_post[l], mod[l], n_ctx)

        pc = p[:n_ctx]
        states[0].append(pc[:, C_NK:C_NK + 1024].reshape(bc, SEQ, 8, HEAD_DIM))
        states[1].append(pc[:, C_NV:C_NV + 1024].reshape(bc, SEQ, 8, HEAD_DIM))
        states[2].append(pc[:, C_SWK:C_SWK + 256].reshape(bc, SEQ, 2, HEAD_DIM))
        states[3].append(pc[:, C_SWV:C_SWV + 256].reshape(bc, SEQ, 2, HEAD_DIM))
        states[4].append(ckvn[:n_ctx].reshape(bc, SEQ, MLA_KV_LORA))
        states[5].append(pc[:, C_KPE:C_KPE + MLA_ROPE].reshape(bc, SEQ, MLA_ROPE))
        states[6].append(pc[:, C_DK:C_DK + 1024].reshape(bc, SEQ, 4, 2 * HEAD_DIM))
        states[7].append(pc[:, C_DV:C_DV + 1024].reshape(bc, SEQ, 4, 2 * HEAD_DIM))

    y_prompt = x[:n_ctx].reshape(bc, SEQ, D_MODEL)
    y_sample = x[n_ctx:].reshape(bl, DEC_SEQ, D_MODEL)
    return (y_prompt, y_sample) + tuple(jnp.stack(s, axis=1) for s in states)
```

```python
import functools
import math
from typing import Callable, NamedTuple

import jax
import jax.numpy as jnp
from jax import lax
from jax.experimental import pallas as pl
from jax.experimental.pallas import tpu as pltpu

F32 = jnp.float32
BF16 = jnp.bfloat16

D_MODEL = 4096
HEAD_DIM = 128
GRID_W = 64
SEQ = 256
DEC_SEQ = 1024
PAST_LEN = 512
NAT_KR, NAT_KC = 8, 16
SWA_WINDOW = 128
MLA_NOPE, MLA_ROPE, MLA_V = 128, 64, 128
MLA_Q_LORA, MLA_KV_LORA = 896, 256
ROPE_BASE = 10000.0
EPS = 1e-6
NEG_INF = -1e30
LOG2E = 1.4426950408889634
QSCALE = HEAD_DIM ** -0.5 * LOG2E
QSCALE_MLA = (MLA_NOPE + MLA_ROPE) ** -0.5 * LOG2E

N_A = 6144
B_START = 5824
N_B = 7168
A_NQ, A_NK, A_NV = 0, 1024, 2048
A_SWQ, A_SWK, A_SWV = 3072, 4096, 4352
A_CQ, A_CKV, A_KPE = 4608, 5504, 5760
B_DQ, B_DK, B_DV, B_GP = 0, 1024, 2048, 3072

ROW_BLOCK = 256
TQ_WIN = 128
TQ_DENSE = 256
NAT_SPAN = 10
VMEM_LIMIT = 56 * 1024 * 1024


class Group(NamedTuple):
    row0: int
    rows: int
    mod_row: Callable


def _cparams(n_axes):
    return pltpu.CompilerParams(dimension_semantics=("arbitrary",) * n_axes,
                                vmem_limit_bytes=VMEM_LIMIT)


def _silu(x):
    return x * (1.0 / (1.0 + jnp.exp(-x)))


def _nt(a, b):
    return lax.dot_general(a, b, (((1,), (1,)), ((), ())), preferred_element_type=F32)


def _softmax_pv(parts, extra=None):
    m = parts[0][0].max(axis=-1, keepdims=True)
    for s, _ in parts[1:]:
        m = jnp.maximum(m, s.max(axis=-1, keepdims=True))
    if extra is not None:
        m = jnp.maximum(m, extra)
    den = None
    acc = None
    for s, v in parts:
        p = jnp.exp2(s - m)
        ps = p.sum(axis=-1, keepdims=True)
        pv = jnp.dot(p.astype(BF16), v, preferred_element_type=F32)
        den = ps if den is None else den + ps
        acc = pv if acc is None else acc + pv
    if extra is not None:
        den = den + jnp.exp2(extra - m)
    return acc / den


def _rope(x, cos, sin_lo, sin_hi, half):
    lanes = x.shape[-1]
    return (x * cos + pltpu.roll(x, lanes - half, 1) * sin_lo + pltpu.roll(x, half, 1) * sin_hi)


def _rms(x, g):
    return x * lax.rsqrt(jnp.mean(x * x, axis=-1, keepdims=True) + EPS) * g


def _aliased_call(body, n_in, alias_args, **kw):
    n_alias = len(alias_args)
    n_out = len(kw["out_shape"]) if isinstance(kw["out_shape"], (tuple, list)) else 1
    if n_alias == 0:
        return pl.pallas_call(body, **kw)

    def with_aliases(*refs):
        body(*refs[:n_in], *refs[n_in + n_alias:])

    kw["in_specs"] = list(kw["in_specs"]) + [pl.BlockSpec(memory_space=pl.ANY)] * n_alias
    kw["input_output_aliases"] = {n_in + k: n_out - n_alias + k for k in range(n_alias)}
    return pl.pallas_call(with_aliases, **kw)


def _mod_kernel(c_ref, w_ref, b_ref, o_ref):
    s = _silu(c_ref[...]).astype(BF16)
    o_ref[...] = jnp.dot(s, w_ref[...].astype(BF16), preferred_element_type=F32) + b_ref[...]


def _modulation(cond, w_mod, b_mod):
    depth = w_mod.shape[0]
    rows = cond.shape[0]
    tn = 512
    n = 3 * D_MODEL
    return pl.pallas_call(
        _mod_kernel,
        grid=(depth, n // tn),
        in_specs=[
            pl.BlockSpec((rows, D_MODEL), lambda l, j: (0, 0)),
            pl.BlockSpec((None, D_MODEL, tn), lambda l, j: (l, 0, j)),
            pl.BlockSpec((None, 1, tn), lambda l, j: (l, 0, j)),
        ],
        out_specs=pl.BlockSpec((None, rows, tn), lambda l, j: (l, 0, j)),
        out_shape=jax.ShapeDtypeStruct((depth, rows, n), F32),
        compiler_params=_cparams(2),
        name="modulation",
    )(cond, w_mod, b_mod.reshape(depth, 1, n))


def _mod_spec(grp, layer):
    return pl.BlockSpec((None, None, 3, D_MODEL), lambda i: (layer, grp.mod_row(i), 0, 0))


def _vec_spec(layer):
    return pl.BlockSpec((None, 1, D_MODEL), lambda i: (layer, 0, 0))


def _tok_spec(grp):
    return pl.BlockSpec((ROW_BLOCK, D_MODEL), lambda i: (grp.row0 // ROW_BLOCK + i, 0))


def _own_spec():
    return pl.BlockSpec((ROW_BLOCK, D_MODEL), lambda i: (i, 0))


def _norm_mod_kernel(x_ref, g_ref, m_ref, h_ref):
    y = _rms(x_ref[...], g_ref[...])
    h_ref[...] = (y * (1.0 + m_ref[1:2, :]) + m_ref[0:1, :]).astype(BF16)


def _norm_mod(x, g_pre, mod, layer, grp, n_tok, h_prev):
    return _aliased_call(
        _norm_mod_kernel, 3, [] if h_prev is None else [h_prev],
        grid=(grp.rows // ROW_BLOCK,),
        in_specs=[_own_spec(), _vec_spec(layer), _mod_spec(grp, layer)],
        out_specs=_tok_spec(grp),
        out_shape=jax.ShapeDtypeStruct((n_tok, D_MODEL), BF16),
        compiler_params=_cparams(1),
        name="norm_mod",
    )(x, g_pre, mod, *([] if h_prev is None else [h_prev]))


def _post_kernel(x_ref, o_ref, g_ref, m_ref, y_ref):
    y_ref[...] = x_ref[...] + m_ref[2:3, :] * _rms(o_ref[...], g_ref[...])


def _post(x, o, g_post, mod, layer, grp):
    return pl.pallas_call(
        _post_kernel,
        grid=(grp.rows // ROW_BLOCK,),
        in_specs=[_own_spec(), _tok_spec(grp), _vec_spec(layer), _mod_spec(grp, layer)],
        out_specs=_own_spec(),
        out_shape=jax.ShapeDtypeStruct((grp.rows, D_MODEL), F32),
        compiler_params=_cparams(1),
        name="post_residual",
    )(x, o, g_post, mod)


def _post_norm_kernel(x_ref, o_ref, g_ref, m_ref, gn_ref, mn_ref, y_ref, h_ref):
    y = x_ref[...] + m_ref[2:3, :] * _rms(o_ref[...], g_ref[...])
    y_ref[...] = y
    h_ref[...] = (_rms(y, gn_ref[...]) * (1.0 + mn_ref[1:2, :]) + mn_ref[0:1, :]).astype(BF16)


def _post_norm(x, o, g_post, g_pre, mod, layer, grp, n_tok, h_prev):
    return _aliased_call(
        _post_norm_kernel, 6, [] if h_prev is None else [h_prev],
        grid=(grp.rows // ROW_BLOCK,),
        in_specs=[_own_spec(), _tok_spec(grp), _vec_spec(layer), _mod_spec(grp, layer),
                  _vec_spec(layer + 1), _mod_spec(grp, layer + 1)],
        out_specs=[_own_spec(), _tok_spec(grp)],
        out_shape=[jax.ShapeDtypeStruct((grp.rows, D_MODEL), F32),
                   jax.ShapeDtypeStruct((n_tok, D_MODEL), BF16)],
        compiler_params=_cparams(1),
        name="post_norm",
    )(x, o, g_post, mod, g_pre, mod, *([] if h_prev is None else [h_prev]))


def _rms_cols_kernel(a_ref, b_ref, g_ref, y_ref, *, wa, wb):
    x = jnp.concatenate([a_ref[:, :wa], b_ref[:, :wb]], axis=-1)
    y_ref[...] = _rms(x, g_ref[...]).astype(y_ref.dtype)


def _rms_cols(p, col, block_w, width, g, layer, out_dtype):
    m = p.shape[0]
    tm = 1024
    first = col // block_w
    return pl.pallas_call(
        functools.partial(_rms_cols_kernel, wa=block_w, wb=width - block_w),
        grid=(m // tm,),
        in_specs=[
            pl.BlockSpec((tm, block_w), lambda i: (i, first)),
            pl.BlockSpec((tm, block_w), lambda i: (i, first + 1)),
            pl.BlockSpec((None, 1, width), lambda i: (layer, 0, 0)),
        ],
        out_specs=pl.BlockSpec((tm, width), lambda i: (i, 0)),
        out_shape=jax.ShapeDtypeStruct((m, width), out_dtype),
        compiler_params=_cparams(1),
        name="rms_cols",
    )(p, p, g)


def _mm_kernel(a_ref, b_ref, o_ref):
    o_ref[...] = jnp.dot(a_ref[...], b_ref[...], preferred_element_type=F32).astype(o_ref.dtype)


def _matmul(a, w, layer, tn, name):
    m, k = a.shape
    n = w.shape[-1]
    tm = 1024 if m % 1024 == 0 else 512
    return pl.pallas_call(
        _mm_kernel,
        grid=(m // tm, n // tn),
        in_specs=[
            pl.BlockSpec((tm, k), lambda i, j: (i, 0)),
            pl.BlockSpec((None, k, tn), lambda i, j: (layer, 0, j)),
        ],
        out_specs=pl.BlockSpec((tm, tn), lambda i, j: (i, j)),
        out_shape=jax.ShapeDtypeStruct((m, n), F32),
        compiler_params=_cparams(2),
        name=name,
    )(a, w)


def _gated(o, gp):
    return (o * _silu(gp)).astype(BF16)


def _ctx_a_kernel(q_ref, k_ref, v_ref, gp_ref, o_ref):
    for h in range(8):
        sl = slice(h * HEAD_DIM, (h + 1) * HEAD_DIM)
        q = (q_ref[:, sl] * QSCALE).astype(BF16)
        o = _softmax_pv([(_nt(q, k_ref[:, sl].astype(BF16)), v_ref[:, sl].astype(BF16))])
        o_ref[:, sl] = _gated(o, gp_ref[:, sl])


def _ctx_b_kernel(q_ref, k_ref, v_ref, sink_ref, gp_ref, o_ref, *, layer):
    for h in range(8):
        sl = slice(h * HEAD_DIM, (h + 1) * HEAD_DIM)
        kv = slice((h // 4) * HEAD_DIM, (h // 4 + 1) * HEAD_DIM)
        q = (q_ref[:, sl] * QSCALE).astype(BF16)
        sink = sink_ref[layer, h] * LOG2E
        o = _softmax_pv([(_nt(q, k_ref[:, kv].astype(BF16)), v_ref[:, kv].astype(BF16))], extra=sink)
        o_ref[:, sl] = _gated(o, gp_ref[:, sl])


def _ctx_c_kernel(qn_ref, qp_ref, kv_ref, kpe_ref, gp_ref, o_ref):
    kpe = kpe_ref[...].astype(BF16)
    for h in range(8):
        sl = slice(h * HEAD_DIM, (h + 1) * HEAD_DIM)
        kn = kv_ref[:, 2 * h * HEAD_DIM:(2 * h + 1) * HEAD_DIM].astype(BF16)
        v = kv_ref[:, (2 * h + 1) * HEAD_DIM:(2 * h + 2) * HEAD_DIM].astype(BF16)
        q = (jnp.concatenate([qn_ref[:, sl], qp_ref[:, sl]], axis=-1) * QSCALE_MLA).astype(BF16)
        o = _softmax_pv([(_nt(q, jnp.concatenate([kn, kpe], axis=-1)), v)])
        o_ref[:, sl] = _gated(o, gp_ref[:, sl])


def _diff_lambda(lam_ref, lam_init):
    lp = lam_ref[...]
    a = jnp.sum(lp[0:1, :] * lp[1:2, :], axis=-1, keepdims=True)
    b = jnp.sum(lp[2:3, :] * lp[3:4, :], axis=-1, keepdims=True)
    return jnp.exp(a) - jnp.exp(b) + lam_init


def _diff_finish(o1, o2, lam, gsub, lam_init):
    return _rms(o1 - lam * o2, gsub) * (1.0 - lam_init)


def _ctx_d_kernel(q_ref, k_ref, v_ref, lam_ref, gsub_ref, gp_ref, o_ref, *, lam_init):
    lam = _diff_lambda(lam_ref, lam_init)
    gsub = gsub_ref[...]
    for h in range(4):
        c0 = h * 2 * HEAD_DIM
        s1, s2 = slice(c0, c0 + HEAD_DIM), slice(c0 + HEAD_DIM, c0 + 2 * HEAD_DIM)
        sl = slice(c0, c0 + 2 * HEAD_DIM)
        v = v_ref[:, sl].astype(BF16)
        q1 = (q_ref[:, s1] * QSCALE).astype(BF16)
        q2 = (q_ref[:, s2] * QSCALE).astype(BF16)
        o1 = _softmax_pv([(_nt(q1, k_ref[:, s1].astype(BF16)), v)])
        o2 = _softmax_pv([(_nt(q2, k_ref[:, s2].astype(BF16)), v)])
        o_ref[:, sl] = _gated(_diff_finish(o1, o2, lam, gsub, lam_init), gp_ref[:, sl])


def _ctx_attention(pa, pb, q_up, kv_up, sink, lam, gsub, layer, lam_init, ctx, n_tok):
    nb = ctx.rows // SEQ
    wide = 8 * HEAD_DIM

    def blk(col, w=wide):
        return pl.BlockSpec((SEQ, w), lambda b: (b, col // w))

    def small(shape):
        return pl.BlockSpec((None,) + shape, lambda b: (layer,) + (0,) * len(shape))

    def call(body, n_in, in_specs, args, mixer, mixed, name):
        return _aliased_call(
            body, n_in, [] if mixed is None else [mixed], grid=(nb,), in_specs=in_specs,
            out_specs=pl.BlockSpec((SEQ, wide), lambda b: (b, mixer)),
            out_shape=jax.ShapeDtypeStruct((n_tok, D_MODEL), BF16),
            compiler_params=_cparams(1), name=name,
        )(*args, *([] if mixed is None else [mixed]))

    mixed = call(_ctx_a_kernel, 4, [blk(A_NQ), blk(A_NK), blk(A_NV), blk(B_GP)],
                 (pa, pa, pa, pb), 0, None, "ctx_nat")
    mixed = call(functools.partial(_ctx_b_kernel, layer=layer), 5,
                 [blk(A_SWQ), blk(A_SWK, 256), blk(A_SWV, 256),
                  pl.BlockSpec(memory_space=pltpu.SMEM), blk(B_GP + wide)],
                 (pa, pa, pa, sink, pb), 1, mixed, "ctx_swa")
    mixed = call(_ctx_c_kernel, 5,
                 [blk(0), blk(wide), blk(0, 2 * wide), blk(A_KPE, HEAD_DIM), blk(B_GP + 2 * wide)],
                 (q_up, q_up, kv_up, pa, pb), 2, mixed, "ctx_mla")
    mixed = call(functools.partial(_ctx_d_kernel, lam_init=lam_init), 6,
                 [blk(B_DQ), blk(B_DK), blk(B_DV), small((4, HEAD_DIM)), small((1, 2 * HEAD_DIM)),
                  blk(B_GP + 3 * wide)],
                 (pb, pb, pb, lam, gsub, pb), 3, mixed, "ctx_diff")
    return mixed


def _nat_win_start(n):
    n_rows = DEC_SEQ // GRID_W
    return min(max(2 * n - NAT_KR // 2, 0), n_rows - NAT_SPAN)


def _lat_a_kernel(q_ref, k_ref, v_ref, ck_ref, cv_ref, bias_ref, gp_ref, o_ref):
    k = k_ref[...].astype(BF16)
    v = v_ref[...].astype(BF16)
    ck = ck_ref[...].astype(BF16)
    cv = cv_ref[...].astype(BF16)
    for n in range(DEC_SEQ // TQ_WIN):
        rows = slice(n * TQ_WIN, (n + 1) * TQ_WIN)
        win = slice(_nat_win_start(n) * GRID_W, (_nat_win_start(n) + NAT_SPAN) * GRID_W)
        q = (q_ref[rows, :] * QSCALE).astype(BF16)
        o = _softmax_pv([(_nt(q, ck), cv), (_nt(q, k[win]) + bias_ref[n], v[win])])
        o_ref[rows, :] = _gated(o, gp_ref[rows, :])


def _swa_win_start(n):
    return min(max(n - 1, 0), DEC_SEQ // TQ_WIN - 3) * TQ_WIN


def _lat_b_kernel(q_ref, k_ref, v_ref, ck_ref, cv_ref, sink_ref, mask_ref, cos_ref, sl_ref, sh_ref,
                  gp_ref, o_ref, *, layer):
    k = _rope(k_ref[...], cos_ref[...], sl_ref[...], sh_ref[...], 32).astype(BF16)
    v = v_ref[...].astype(BF16)
    ck = ck_ref[...].astype(BF16)
    cv = cv_ref[...].astype(BF16)
    sink = sink_ref[layer, pl.program_id(1)] * LOG2E
    for n in range(DEC_SEQ // TQ_WIN):
        rows = slice(n * TQ_WIN, (n + 1) * TQ_WIN)
        w0 = _swa_win_start(n)
        win = slice(w0, w0 + 3 * TQ_WIN)
        q = _rope(q_ref[rows, :], cos_ref[rows, :], sl_ref[rows, :], sh_ref[rows, :], 32)
        q = (q * QSCALE).astype(BF16)
        band = mask_ref[n - w0 // TQ_WIN]
        o = _softmax_pv([(_nt(q, ck), cv), (_nt(q, k[win]) + band, v[win])], extra=sink)
        o_ref[rows, :] = _gated(o, gp_ref[rows, :])


def _lat_c_kernel(qn_ref, qp_ref, kn_ref, v_ref, kpe_ref, ckn_ref, cv_ref, ckpe_ref,
                  cos_ref, sl_ref, sh_ref, gp_ref, o_ref):
    kpe = _rope(kpe_ref[...], cos_ref[...], sl_ref[...], sh_ref[...], 16)
    keys = jnp.concatenate([
        jnp.concatenate([ckn_ref[...], ckpe_ref[...]], axis=-1),
        jnp.concatenate([kn_ref[...], kpe], axis=-1)], axis=0).astype(BF16)
    vals = jnp.concatenate([cv_ref[...], v_ref[...]], axis=0).astype(BF16)
    for n in range(DEC_SEQ // TQ_DENSE):
        rows = slice(n * TQ_DENSE, (n + 1) * TQ_DENSE)
        qp = _rope(qp_ref[rows, :], cos_ref[rows, :], sl_ref[rows, :], sh_ref[rows, :], 16)
        q = (jnp.concatenate([qn_ref[rows, :], qp], axis=-1) * QSCALE_MLA).astype(BF16)
        o = _softmax_pv([(_nt(q, keys), vals)])
        o_ref[rows, :] = _gated(o, gp_ref[rows, :])


def _lat_d_kernel(q1_ref, q2_ref, k1_ref, k2_ref, v_ref, ck1_ref, ck2_ref, cv_ref,
                  lam_ref, gsub_ref, cos_ref, sl_ref, sh_ref, gp_ref, o_ref, *, lam_init):
    cos, sl, sh = cos_ref[...], sl_ref[...], sh_ref[...]
    keys1 = jnp.concatenate([ck1_ref[...], _rope(k1_ref[...], cos, sl, sh, 32)], axis=0).astype(BF16)
    keys2 = jnp.concatenate([ck2_ref[...], _rope(k2_ref[...], cos, sl, sh, 32)], axis=0).astype(BF16)
    vals = jnp.concatenate([cv_ref[...], v_ref[...]], axis=0).astype(BF16)
    lam = _diff_lambda(lam_ref, lam_init)
    gsub = gsub_ref[...]
    for n in range(DEC_SEQ // TQ_DENSE):
        rows = slice(n * TQ_DENSE, (n + 1) * TQ_DENSE)
        c, a, b = cos_ref[rows, :], sl_ref[rows, :], sh_ref[rows, :]
        q1 = (_rope(q1_ref[rows, :], c, a, b, 32) * QSCALE).astype(BF16)
        q2 = (_rope(q2_ref[rows, :], c, a, b, 32) * QSCALE).astype(BF16)
        o1 = _softmax_pv([(_nt(q1, keys1), vals)])
        o2 = _softmax_pv([(_nt(q2, keys2), vals)])
        o_ref[rows, :] = _gated(_diff_finish(o1, o2, lam, gsub, lam_init), gp_ref[rows, :])


def _lat_attention(mixed, pa, pb, q_up, kv_up, caches, layer, nat_bias, swa_mask, sink, lam,
                   gsub, lam_init, rope128, rope64, lat, n_tok):
    nb = lat.rows // DEC_SEQ
    rb0 = lat.row0 // DEC_SEQ
    cb0 = n_tok // PAST_LEN
    c_nat_k, c_nat_v, c_swa_k, c_swa_v, c_kpe, c_dk, c_dv = caches
    hd = HEAD_DIM

    def rows(col_of, w=hd):
        return pl.BlockSpec((DEC_SEQ, w), lambda b, h: (rb0 + b, col_of(h)))

    def cache(col_of, w=hd):
        return pl.BlockSpec((None, None, PAST_LEN, w), lambda b, h: (b, layer, 0, col_of(h)))

    def table():
        return pl.BlockSpec((DEC_SEQ, hd), lambda b, h: (0, 0))

    def small(shape):
        return pl.BlockSpec((None,) + shape, lambda b, h: (layer,) + (0,) * len(shape))

    def call(body, n_in, heads, in_specs, args, mixed, w, col0, name):
        return _aliased_call(
            body, n_in, [mixed], grid=(nb, heads), in_specs=in_specs,
            out_specs=rows(lambda h: col0 // w + h, w),
            out_shape=jax.ShapeDtypeStruct((n_tok, D_MODEL), BF16),
            compiler_params=_cparams(2), name=name,
        )(*args, mixed)

    mixed = call(
        _lat_a_kernel, 7, 8,
        [rows(lambda h: A_NQ // hd + h), rows(lambda h: A_NK // hd + h), rows(lambda h: A_NV // hd + h),
         cache(lambda h: h), cache(lambda h: h),
         pl.BlockSpec((None, None, DEC_SEQ // TQ_WIN, TQ_WIN, NAT_SPAN * GRID_W),
                      lambda b, h: (layer, h, 0, 0, 0)),
         rows(lambda h: B_GP // hd + h)],
        (pa, pa, pa, c_nat_k, c_nat_v, nat_bias, pb), mixed, hd, 0, "lat_nat")
    mixed = call(
        functools.partial(_lat_b_kernel, layer=layer), 11, 8,
        [rows(lambda h: A_SWQ // hd + h), rows(lambda h: A_SWK // hd + h // 4),
         rows(lambda h: A_SWV // hd + h // 4), cache(lambda h: h // 4), cache(lambda h: h // 4),
         pl.BlockSpec(memory_space=pltpu.SMEM),
         pl.BlockSpec((3, TQ_WIN, 3 * TQ_WIN), lambda b, h: (0, 0, 0)), table(), table(), table(),
         rows(lambda h: B_GP // hd + 8 + h)],
        (pa, pa, pa, c_swa_k, c_swa_v, sink, swa_mask, *rope128, pb), mixed, hd, 8 * hd, "lat_swa")
    mixed = call(
        _lat_c_kernel, 12, 8,
        [rows(lambda h: h), rows(lambda h: 8 + h),
         rows(lambda h: 2 * h), rows(lambda h: 2 * h + 1), rows(lambda h: A_KPE // hd),
         pl.BlockSpec((PAST_LEN, hd), lambda b, h: (cb0 + b, 2 * h)),
         pl.BlockSpec((PAST_LEN, hd), lambda b, h: (cb0 + b, 2 * h + 1)),
         cache(lambda h: 0), table(), table(), table(),
         rows(lambda h: B_GP // hd + 16 + h)],
        (q_up, q_up, kv_up, kv_up, pa, kv_up, kv_up, c_kpe, *rope64, pb), mixed, hd, 16 * hd, "lat_mla")
    mixed = call(
        functools.partial(_lat_d_kernel, lam_init=lam_init), 14, 4,
        [rows(lambda h: B_DQ // hd + 2 * h), rows(lambda h: B_DQ // hd + 2 * h + 1),
         rows(lambda h: B_DK // hd + 2 * h), rows(lambda h: B_DK // hd + 2 * h + 1),
         rows(lambda h: B_DV // (2 * hd) + h, 2 * hd),
         cache(lambda h: 2 * h), cache(lambda h: 2 * h + 1), cache(lambda h: h, 2 * hd),
         small((4, hd)), small((1, 2 * hd)), table(), table(), table(),
         rows(lambda h: B_GP // (2 * hd) + 12 + h, 2 * hd)],
        (pb, pb, pb, pb, pb, c_dk, c_dk, c_dv, lam, gsub, *rope128, pb), mixed, 2 * hd, 24 * hd,
        "lat_diff")
    return mixed


def _state_kernel(nk_ref, nv_ref, sk_ref, sv_ref, ckv_ref, kpe_ref, dk_ref, dv_ref,
                  o_nk, o_nv, o_sk, o_sv, o_ckv, o_kpe, o_dk, o_dv):
    for src, dst, heads, w in ((nk_ref, o_nk, 8, HEAD_DIM), (nv_ref, o_nv, 8, HEAD_DIM),
                               (sk_ref, o_sk, 2, HEAD_DIM), (sv_ref, o_sv, 2, HEAD_DIM),
                               (dk_ref, o_dk, 4, 2 * HEAD_DIM), (dv_ref, o_dv, 4, 2 * HEAD_DIM)):
        for h in range(heads):
            dst[:, h, :] = src[:, h * w:(h + 1) * w]
    o_ckv[...] = ckv_ref[...]
    o_kpe[...] = kpe_ref[:, :MLA_ROPE]


def _write_states(pa, pb, ckvn, layer, depth, bc, prev):
    def src(col, w):
        return pl.BlockSpec((SEQ, w), lambda b: (b, col // w))

    def dst(*tail):
        return pl.BlockSpec((None, None, SEQ) + tail, lambda b: (b, layer, 0) + (0,) * len(tail))

    tails = [(8, HEAD_DIM), (8, HEAD_DIM), (2, HEAD_DIM), (2, HEAD_DIM), (MLA_KV_LORA,), (MLA_ROPE,),
             (4, 2 * HEAD_DIM), (4, 2 * HEAD_DIM)]
    return _aliased_call(
        _state_kernel, 8, list(prev), grid=(bc,),
        in_specs=[src(A_NK, 1024), src(A_NV, 1024), src(A_SWK, 256), src(A_SWV, 256),
                  pl.BlockSpec((SEQ, MLA_KV_LORA), lambda b: (b, 0)), src(A_KPE, HEAD_DIM),
                  src(B_DK, 1024), src(B_DV, 1024)],
        out_specs=[dst(*t) for t in tails],
        out_shape=[jax.ShapeDtypeStruct((bc, depth, SEQ) + t, F32) for t in tails],
        compiler_params=_cparams(1), name="write_states",
    )(pa, pa, pa, pa, ckvn, pa, pb, pb, *prev)


def _rope_tables(rot_dim):
    axis_dim = rot_dim // 2
    inv = 1.0 / (ROPE_BASE ** (jnp.arange(0, axis_dim, 2, dtype=F32) / axis_dim))
    t = jnp.arange(DEC_SEQ)
    ang_row = (t // GRID_W).astype(F32)[:, None] * inv
    ang_col = (t % GRID_W).astype(F32)[:, None] * inv
    zeros = jnp.zeros_like(ang_row)
    pad = HEAD_DIM - rot_dim
    cos = jnp.concatenate([jnp.cos(ang_row)] * 2 + [jnp.cos(ang_col)] * 2
                          + [jnp.ones((DEC_SEQ, pad), F32)], axis=-1)
    sin_lo = jnp.concatenate([-jnp.sin(ang_row), zeros, -jnp.sin(ang_col), zeros,
                              jnp.zeros((DEC_SEQ, pad), F32)], axis=-1)
    sin_hi = jnp.concatenate([zeros, jnp.sin(ang_row), zeros, jnp.sin(ang_col),
                              jnp.zeros((DEC_SEQ, pad), F32)], axis=-1)
    return cos, sin_lo, sin_hi


def _nat_bias(rpb):
    n_rows = DEC_SEQ // GRID_W
    steps = DEC_SEQ // TQ_WIN
    q_per = TQ_WIN // GRID_W
    qr = jnp.arange(steps)[:, None] * q_per + jnp.arange(q_per)[None, :]
    kr = jnp.asarray([_nat_win_start(n) for n in range(steps)])[:, None] + jnp.arange(NAT_SPAN)[None, :]
    c = jnp.arange(GRID_W)
    row0 = jnp.clip(qr - NAT_KR // 2, 0, n_rows - NAT_KR)
    col0 = jnp.clip(c - NAT_KC // 2, 0, GRID_W - NAT_KC)
    ok_r = (kr[:, None, :] >= row0[:, :, None]) & (kr[:, None, :] < row0[:, :, None] + NAT_KR)
    ok_c = (c[None, :] >= col0[:, None]) & (c[None, :] < col0[:, None] + NAT_KC)
    dr = jnp.clip(kr[:, None, :] - qr[:, :, None], -(NAT_KR - 1), NAT_KR - 1) + (NAT_KR - 1)
    dc = jnp.clip(c[None, :] - c[:, None], -(NAT_KC - 1), NAT_KC - 1) + (NAT_KC - 1)
    sel_r = jax.nn.one_hot(dr, 2 * NAT_KR - 1, dtype=F32)
    sel_c = jax.nn.one_hot(dc, 2 * NAT_KC - 1, dtype=F32)
    bias = jnp.einsum("nqkr,lhrc,xyc->lhnqxky", sel_r, rpb.astype(F32), sel_c,
                      precision=lax.Precision.HIGHEST)
    ok = ok_r[:, :, None, :, None] & ok_c[None, None, :, None, :]
    bias = jnp.where(ok[None, None], bias * LOG2E, NEG_INF)
    return bias.reshape(rpb.shape[0], rpb.shape[1], steps, TQ_WIN, NAT_SPAN * GRID_W)


def _swa_mask():
    qi = jnp.arange(TQ_WIN)[None, :, None] + TQ_WIN * jnp.arange(3)[:, None, None]
    kj = jnp.arange(3 * TQ_WIN)[None, None, :]
    return jnp.where(jnp.abs(qi - kj) <= SWA_WINDOW, 0.0, NEG_INF).astype(F32)


def _pack_w_uq(w_uq):
    depth = w_uq.shape[0]
    w = w_uq.reshape(depth, MLA_Q_LORA, 8, MLA_NOPE + MLA_ROPE)
    nope = w[..., :MLA_NOPE].reshape(depth, MLA_Q_LORA, 8 * MLA_NOPE)
    pe = jnp.pad(w[..., MLA_NOPE:], ((0, 0), (0, 0), (0, 0), (0, HEAD_DIM - MLA_ROPE)))
    return jnp.concatenate([nope, pe.reshape(depth, MLA_Q_LORA, 8 * HEAD_DIM)], axis=-1).astype(BF16)


def kernel(x_prompt, x_sample, cache_nat_k, cache_nat_v, cache_swa_k, cache_swa_v, cache_mla_ckv, cache_mla_kpe, cache_diff_k, cache_diff_v, c, c_ctx, w_mod, b_mod, g_pre, g_post, w_in, w_out, nat_rpb, swa_sink, mla_g_q, mla_g_kv, mla_w_uq, mla_w_ukv, diff_lambda, diff_g_subln):
    depth = w_mod.shape[0]
    bc, bl = x_prompt.shape[0], x_sample.shape[0]
    n_ctx, n_lat = bc * SEQ, bl * DEC_SEQ
    n_tok = n_ctx + n_lat
    assert n_ctx % DEC_SEQ == 0 and x_prompt.shape[1] == SEQ and x_sample.shape[1] == DEC_SEQ
    ctx = Group(0, n_ctx, lambda i: 0)
    lat = Group(n_ctx, n_lat, lambda i: 1 + i // (DEC_SEQ // ROW_BLOCK))

    n_cond = -(-(1 + bl) // 8) * 8
    cond = jnp.concatenate([c_ctx[None], c, jnp.zeros((n_cond - 1 - bl, D_MODEL), F32)], axis=0)
    mod = _modulation(cond, w_mod, b_mod).reshape(depth, n_cond, 3, D_MODEL)

    w_a = w_in[..., :N_A].astype(BF16)
    w_b = w_in[..., B_START:].astype(BF16)
    w_out_b = w_out.astype(BF16)
    w_uq_p = _pack_w_uq(mla_w_uq)
    w_ukv_b = mla_w_ukv.astype(BF16)
    g_pre3 = g_pre.reshape(depth, 1, D_MODEL)
    g_post3 = g_post.reshape(depth, 1, D_MODEL)
    g_q3 = mla_g_q.reshape(depth, 1, MLA_Q_LORA)
    g_kv3 = mla_g_kv.reshape(depth, 1, MLA_KV_LORA)
    gsub3 = diff_g_subln.reshape(depth, 1, 2 * HEAD_DIM)
    rope128 = _rope_tables(HEAD_DIM)
    rope64 = _rope_tables(MLA_ROPE)
    nat_bias = _nat_bias(nat_rpb)
    swa_mask = _swa_mask()

    caches = (cache_nat_k.reshape(bl, depth, PAST_LEN, -1), cache_nat_v.reshape(bl, depth, PAST_LEN, -1),
              cache_swa_k.reshape(bl, depth, PAST_LEN, -1), cache_swa_v.reshape(bl, depth, PAST_LEN, -1),
              jnp.pad(cache_mla_kpe, ((0, 0), (0, 0), (0, 0), (0, HEAD_DIM - MLA_ROPE))),
              cache_diff_k.reshape(bl, depth, PAST_LEN, -1), cache_diff_v.reshape(bl, depth, PAST_LEN, -1))

    x_c = x_prompt.reshape(n_ctx, D_MODEL)
    x_l = x_sample.reshape(n_lat, D_MODEL)
    h = _norm_mod(x_c, g_pre3, mod, 0, ctx, n_tok, None)
    h = _norm_mod(x_l, g_pre3, mod, 0, lat, n_tok, h)
    states = ()
    for l in range(depth):
        lam_init = 0.8 - 0.6 * math.exp(-0.3 * l)
        pa = _matmul(h, w_a, l, 768, "in_proj_a")
        pb = _matmul(h, w_b, l, 1024, "in_proj_b")
        cqn = _rms_cols(pa, A_CQ, 512, MLA_Q_LORA, g_q3, l, BF16)
        ckvn = _rms_cols(pa, A_CKV, 128, MLA_KV_LORA, g_kv3, l, F32)
        q_up = _matmul(cqn, w_uq_p, l, 1024, "mla_q_up")
        ckv_all = jnp.concatenate([ckvn, cache_mla_ckv[:, l].reshape(bl * PAST_LEN, MLA_KV_LORA)], axis=0)
        kv_up = _matmul(ckv_all.astype(BF16), w_ukv_b, l, 1024, "mla_kv_up")
        states = _write_states(pa, pb, ckvn, l, depth, bc, states)

        mixed = _ctx_attention(pa, pb, q_up, kv_up, swa_sink, diff_lambda, gsub3, l, lam_init, ctx, n_tok)
        mixed = _lat_attention(mixed, pa, pb, q_up, kv_up, caches, l, nat_bias, swa_mask, swa_sink,
                               diff_lambda, gsub3, lam_init, rope128, rope64, lat, n_tok)
        o = _matmul(mixed, w_out_b, l, 1024, "out_proj")
        if l + 1 < depth:
            x_c, h = _post_norm(x_c, o, g_post3, g_pre3, mod, l, ctx, n_tok, None)
            x_l, h = _post_norm(x_l, o, g_post3, g_pre3, mod, l, lat, n_tok, h)
        else:
            x_c = _post(x_c, o, g_post3, mod, l, ctx)
            x_l = _post(x_l, o, g_post3, mod, l, lat)

    return (x_c.reshape(bc, SEQ, D_MODEL), x_l.reshape(bl, DEC_SEQ, D_MODEL)) + tuple(states)
```

```python
import functools
import math
from typing import Callable, NamedTuple

import jax
import jax.numpy as jnp
from jax import lax
from jax.experimental import pallas as pl
from jax.experimental.pallas import tpu as pltpu

F32 = jnp.float32
BF16 = jnp.bfloat16

D_MODEL = 4096
HEAD_DIM = 128
GRID_W = 64
SEQ = 256
DEC_SEQ = 1024
PAST_LEN = 512
NAT_KR, NAT_KC = 8, 16
SWA_WINDOW = 128
MLA_NOPE, MLA_ROPE, MLA_V = 128, 64, 128
MLA_Q_LORA, MLA_KV_LORA = 896, 256
ROPE_BASE = 10000.0
EPS = 1e-6
NEG_INF = -1e30
LOG2E = 1.4426950408889634
QSCALE = HEAD_DIM ** -0.5 * LOG2E
QSCALE_MLA = (MLA_NOPE + MLA_ROPE) ** -0.5 * LOG2E

N_A = 6144
B_START = 5824
N_B = 7168
A_NQ, A_NK, A_NV = 0, 1024, 2048
A_SWQ, A_SWK, A_SWV = 3072, 4096, 4352
A_CQ, A_CKV, A_KPE = 4608, 5504, 5760
B_DQ, B_DK, B_DV, B_GP = 0, 1024, 2048, 3072

ROW_BLOCK = 256
TQ_WIN = 128
TQ_DENSE = 256
NAT_SPAN = 10
VMEM_LIMIT = 56 * 1024 * 1024


class Group(NamedTuple):
    row0: int
    rows: int
    mod_row: Callable


def _cparams(n_axes):
    return pltpu.CompilerParams(dimension_semantics=("arbitrary",) * n_axes,
                                vmem_limit_bytes=VMEM_LIMIT)


def _silu(x):
    return x * (1.0 / (1.0 + jnp.exp(-x)))


def _nt(a, b):
    return lax.dot_general(a, b, (((1,), (1,)), ((), ())), preferred_element_type=F32)


def _softmax_pv(parts, extra=None):
    m = parts[0][0].max(axis=-1, keepdims=True)
    for s, _ in parts[1:]:
        m = jnp.maximum(m, s.max(axis=-1, keepdims=True))
    if extra is not None:
        m = jnp.maximum(m, extra)
    den = None
    acc = None
    for s, v in parts:
        p = jnp.exp2(s - m)
        ps = p.sum(axis=-1, keepdims=True)
        pv = jnp.dot(p.astype(BF16), v, preferred_element_type=F32)
        den = ps if den is None else den + ps
        acc = pv if acc is None else acc + pv
    if extra is not None:
        den = den + jnp.exp2(extra - m)
    return acc / den


def _rope(x, cos, sin_lo, sin_hi, half):
    lanes = x.shape[-1]
    return (x * cos + pltpu.roll(x, lanes - half, 1) * sin_lo + pltpu.roll(x, half, 1) * sin_hi)


def _rms(x, g):
    return x * lax.rsqrt(jnp.mean(x * x, axis=-1, keepdims=True) + EPS) * g


def _aliased_call(body, n_in, alias_args, **kw):
    n_alias = len(alias_args)
    n_out = len(kw["out_shape"]) if isinstance(kw["out_shape"], (tuple, list)) else 1
    if n_alias == 0:
        return pl.pallas_call(body, **kw)

    def with_aliases(*refs):
        body(*refs[:n_in], *refs[n_in + n_alias:])

    kw["in_specs"] = list(kw["in_specs"]) + [pl.BlockSpec(memory_space=pl.ANY)] * n_alias
    kw["input_output_aliases"] = {n_in + k: n_out - n_alias + k for k in range(n_alias)}
    return pl.pallas_call(with_aliases, **kw)


def _mod_kernel(c_ref, w_ref, b_ref, o_ref):
    s = _silu(c_ref[...]).astype(BF16)
    o_ref[...] = jnp.dot(s, w_ref[...].astype(BF16), preferred_element_type=F32) + b_ref[...]


def _modulation(cond, w_mod, b_mod):
    depth = w_mod.shape[0]
    rows = cond.shape[0]
    tn = 512
    n = 3 * D_MODEL
    return pl.pallas_call(
        _mod_kernel,
        grid=(depth, n // tn),
        in_specs=[
            pl.BlockSpec((rows, D_MODEL), lambda l, j: (0, 0)),
            pl.BlockSpec((None, D_MODEL, tn), lambda l, j: (l, 0, j)),
            pl.BlockSpec((None, 1, tn), lambda l, j: (l, 0, j)),
        ],
        out_specs=pl.BlockSpec((None, rows, tn), lambda l, j: (l, 0, j)),
        out_shape=jax.ShapeDtypeStruct((depth, rows, n), F32),
        compiler_params=_cparams(2),
        name="modulation",
    )(cond, w_mod, b_mod.reshape(depth, 1, n))


def _mod_spec(grp, layer):
    return pl.BlockSpec((None, None, 3, D_MODEL), lambda i: (layer, grp.mod_row(i), 0, 0))


def _vec_spec(layer):
    return pl.BlockSpec((None, 1, D_MODEL), lambda i: (layer, 0, 0))


def _tok_spec(grp):
    return pl.BlockSpec((ROW_BLOCK, D_MODEL), lambda i: (grp.row0 // ROW_BLOCK + i, 0))


def _own_spec():
    return pl.BlockSpec((ROW_BLOCK, D_MODEL), lambda i: (i, 0))


def _norm_mod_kernel(x_ref, g_ref, m_ref, h_ref):
    y = _rms(x_ref[...], g_ref[...])
    h_ref[...] = (y * (1.0 + m_ref[1:2, :]) + m_ref[0:1, :]).astype(BF16)


def _norm_mod(x, g_pre, mod, layer, grp, n_tok, h_prev):
    return _aliased_call(
        _norm_mod_kernel, 3, [] if h_prev is None else [h_prev],
        grid=(grp.rows // ROW_BLOCK,),
        in_specs=[_own_spec(), _vec_spec(layer), _mod_spec(grp, layer)],
        out_specs=_tok_spec(grp),
        out_shape=jax.ShapeDtypeStruct((n_tok, D_MODEL), BF16),
        compiler_params=_cparams(1),
        name="norm_mod",
    )(x, g_pre, mod, *([] if h_prev is None else [h_prev]))


def _post_kernel(x_ref, o_ref, g_ref, m_ref, y_ref):
    y_ref[...] = x_ref[...] + m_ref[2:3, :] * _rms(o_ref[...], g_ref[...])


def _post(x, o, g_post, mod, layer, grp):
    return pl.pallas_call(
        _post_kernel,
        grid=(grp.rows // ROW_BLOCK,),
        in_specs=[_own_spec(), _tok_spec(grp), _vec_spec(layer), _mod_spec(grp, layer)],
        out_specs=_own_spec(),
        out_shape=jax.ShapeDtypeStruct((grp.rows, D_MODEL), F32),
        compiler_params=_cparams(1),
        name="post_residual",
    )(x, o, g_post, mod)


def _post_norm_kernel(x_ref, o_ref, g_ref, m_ref, gn_ref, mn_ref, y_ref, h_ref):
    y = x_ref[...] + m_ref[2:3, :] * _rms(o_ref[...], g_ref[...])
    y_ref[...] = y
    h_ref[...] = (_rms(y, gn_ref[...]) * (1.0 + mn_ref[1:2, :]) + mn_ref[0:1, :]).astype(BF16)


def _post_norm(x, o, g_post, g_pre, mod, layer, grp, n_tok, h_prev):
    return _aliased_call(
        _post_norm_kernel, 6, [] if h_prev is None else [h_prev],
        grid=(grp.rows // ROW_BLOCK,),
        in_specs=[_own_spec(), _tok_spec(grp), _vec_spec(layer), _mod_spec(grp, layer),
                  _vec_spec(layer + 1), _mod_spec(grp, layer + 1)],
        out_specs=[_own_spec(), _tok_spec(grp)],
        out_shape=[jax.ShapeDtypeStruct((grp.rows, D_MODEL), F32),
                   jax.ShapeDtypeStruct((n_tok, D_MODEL), BF16)],
        compiler_params=_cparams(1),
        name="post_norm",
    )(x, o, g_post, mod, g_pre, mod, *([] if h_prev is None else [h_prev]))


def _rms_cols_kernel(a_ref, b_ref, g_ref, y_ref, *, wa, wb):
    x = jnp.concatenate([a_ref[:, :wa], b_ref[:, :wb]], axis=-1)
    y_ref[...] = _rms(x, g_ref[...]).astype(y_ref.dtype)


def _rms_cols(p, col, block_w, width, g, layer, out_dtype):
    m = p.shape[0]
    tm = 1024
    first = col // block_w
    return pl.pallas_call(
        functools.partial(_rms_cols_kernel, wa=block_w, wb=width - block_w),
        grid=(m // tm,),
        in_specs=[
            pl.BlockSpec((tm, block_w), lambda i: (i, first)),
            pl.BlockSpec((tm, block_w), lambda i: (i, first + 1)),
            pl.BlockSpec((None, 1, width), lambda i: (layer, 0, 0)),
        ],
        out_specs=pl.BlockSpec((tm, width), lambda i: (i, 0)),
        out_shape=jax.ShapeDtypeStruct((m, width), out_dtype),
        compiler_params=_cparams(1),
        name="rms_cols",
    )(p, p, g)


def _mm_kernel(a_ref, b_ref, o_ref):
    o_ref[...] = jnp.dot(a_ref[...], b_ref[...], preferred_element_type=F32).astype(o_ref.dtype)


def _matmul(a, w, layer, tn, name, n=None, out_dtype=F32):
    m, k = a.shape
    n = w.shape[-1] if n is None else n
    tm = 1024 if m % 1024 == 0 else 512
    return pl.pallas_call(
        _mm_kernel,
        grid=(m // tm, n // tn),
        in_specs=[
            pl.BlockSpec((tm, k), lambda i, j: (i, 0)),
            pl.BlockSpec((None, k, tn), lambda i, j: (layer, 0, j)),
        ],
        out_specs=pl.BlockSpec((tm, tn), lambda i, j: (i, j)),
        out_shape=jax.ShapeDtypeStruct((m, n), out_dtype),
        compiler_params=_cparams(2),
        name=name,
    )(a, w)


W_IN_COLS = 12992
PACK_ROWS = 1024
SHIFT_W = 640
N_B_PACKED = -(-N_B // SHIFT_W) * SHIFT_W


def _cast_kernel(w_ref, o_ref):
    o_ref[...] = w_ref[...].astype(BF16)


def _pack_w_a(w_in):
    depth = w_in.shape[0]
    tn = 2048
    return pl.pallas_call(
        _cast_kernel,
        grid=(depth, D_MODEL // PACK_ROWS, N_A // tn),
        in_specs=[pl.BlockSpec((None, PACK_ROWS, tn), lambda l, r, j: (l, r, j))],
        out_specs=pl.BlockSpec((None, PACK_ROWS, tn), lambda l, r, j: (l, r, j)),
        out_shape=jax.ShapeDtypeStruct((depth, D_MODEL, N_A), BF16),
        compiler_params=_cparams(3),
        name="pack_w_a",
    )(w_in)


def _shift_kernel(a_ref, b_ref, o_ref):
    x = jnp.concatenate([a_ref[...], b_ref[...]], axis=-1)
    x = pltpu.roll(x, x.shape[-1] - (B_START % HEAD_DIM), 1)
    o_ref[...] = x[:, :SHIFT_W].astype(BF16)


def _pack_w_b(w_in):
    depth = w_in.shape[0]
    first = (B_START - B_START % HEAD_DIM) // SHIFT_W
    last_lane_block = (W_IN_COLS - 1) // HEAD_DIM
    per = SHIFT_W // HEAD_DIM
    return pl.pallas_call(
        _shift_kernel,
        grid=(depth, D_MODEL // PACK_ROWS, N_B_PACKED // SHIFT_W),
        in_specs=[
            pl.BlockSpec((None, PACK_ROWS, SHIFT_W), lambda l, r, j: (l, r, first + j)),
            pl.BlockSpec((None, PACK_ROWS, HEAD_DIM),
                         lambda l, r, j: (l, r, jnp.minimum(per * (first + j + 1), last_lane_block))),
        ],
        out_specs=pl.BlockSpec((None, PACK_ROWS, SHIFT_W), lambda l, r, j: (l, r, j)),
        out_shape=jax.ShapeDtypeStruct((depth, D_MODEL, N_B_PACKED), BF16),
        compiler_params=_cparams(3),
        name="pack_w_b",
    )(w_in, w_in)


def _gated(o, gp):
    return (o * _silu(gp)).astype(BF16)


def _pipelined(n_steps, scores, finish):
    nxt = scores(0)
    for n in range(n_steps):
        cur = nxt
        if n + 1 < n_steps:
            nxt = scores(n + 1)
        finish(n, cur)


def _head(h, w=HEAD_DIM):
    return slice(h * w, (h + 1) * w)


def _ctx_a_kernel(q_ref, k_ref, v_ref, gp_ref, o_ref):
    def scores(h):
        q = (q_ref[:, _head(h)] * QSCALE).astype(BF16)
        return [(_nt(q, k_ref[:, _head(h)].astype(BF16)), v_ref[:, _head(h)].astype(BF16))]

    def finish(h, parts):
        o_ref[:, _head(h)] = _gated(_softmax_pv(parts), gp_ref[:, _head(h)])

    _pipelined(8, scores, finish)


def _ctx_b_kernel(q_ref, k_ref, v_ref, sink_ref, gp_ref, o_ref, *, layer):
    def scores(h):
        q = (q_ref[:, _head(h)] * QSCALE).astype(BF16)
        return [(_nt(q, k_ref[:, _head(h // 4)].astype(BF16)), v_ref[:, _head(h // 4)].astype(BF16))]

    def finish(h, parts):
        o = _softmax_pv(parts, extra=sink_ref[layer, h] * LOG2E)
        o_ref[:, _head(h)] = _gated(o, gp_ref[:, _head(h)])

    _pipelined(8, scores, finish)


def _ctx_c_kernel(qn_ref, qp_ref, kv_ref, kpe_ref, gp_ref, o_ref):
    kpe = kpe_ref[...].astype(BF16)

    def scores(h):
        kn = kv_ref[:, _head(2 * h)].astype(BF16)
        q = (jnp.concatenate([qn_ref[:, _head(h)], qp_ref[:, _head(h)]], axis=-1) * QSCALE_MLA).astype(BF16)
        return [(_nt(q, jnp.concatenate([kn, kpe], axis=-1)), kv_ref[:, _head(2 * h + 1)].astype(BF16))]

    def finish(h, parts):
        o_ref[:, _head(h)] = _gated(_softmax_pv(parts), gp_ref[:, _head(h)])

    _pipelined(8, scores, finish)


def _diff_lambda(lam_ref, lam_init):
    lp = lam_ref[...]
    a = jnp.sum(lp[0:1, :] * lp[1:2, :], axis=-1, keepdims=True)
    b = jnp.sum(lp[2:3, :] * lp[3:4, :], axis=-1, keepdims=True)
    return jnp.exp(a) - jnp.exp(b) + lam_init


def _diff_finish(o1, o2, lam, gsub, lam_init):
    return _rms(o1 - lam * o2, gsub) * (1.0 - lam_init)


def _ctx_d_kernel(q_ref, k_ref, v_ref, lam_ref, gsub_ref, gp_ref, o_ref, *, lam_init):
    lam = _diff_lambda(lam_ref, lam_init)
    gsub = gsub_ref[...]

    first_map = {}

    def scores(t):
        q = (q_ref[:, _head(t)] * QSCALE).astype(BF16)
        return [(_nt(q, k_ref[:, _head(t)].astype(BF16)), v_ref[:, _head(t // 2, 2 * HEAD_DIM)].astype(BF16))]

    def finish(t, parts):
        o = _softmax_pv(parts)
        if t % 2 == 0:
            first_map[t // 2] = o
            return
        sl = _head(t // 2, 2 * HEAD_DIM)
        o = _diff_finish(first_map.pop(t // 2), o, lam, gsub, lam_init)
        o_ref[:, sl] = _gated(o, gp_ref[:, sl])

    _pipelined(8, scores, finish)


def _ctx_attention(pa, pb, q_up, kv_up, sink, lam, gsub, layer, lam_init, ctx, n_tok):
    nb = ctx.rows // SEQ
    wide = 8 * HEAD_DIM

    def blk(col, w=wide):
        return pl.BlockSpec((SEQ, w), lambda b: (b, col // w))

    def small(shape):
        return pl.BlockSpec((None,) + shape, lambda b: (layer,) + (0,) * len(shape))

    def call(body, n_in, in_specs, args, mixer, mixed, name):
        return _aliased_call(
            body, n_in, [] if mixed is None else [mixed], grid=(nb,), in_specs=in_specs,
            out_specs=pl.BlockSpec((SEQ, wide), lambda b: (b, mixer)),
            out_shape=jax.ShapeDtypeStruct((n_tok, D_MODEL), BF16),
            compiler_params=_cparams(1), name=name,
        )(*args, *([] if mixed is None else [mixed]))

    mixed = call(_ctx_a_kernel, 4, [blk(A_NQ), blk(A_NK), blk(A_NV), blk(B_GP)],
                 (pa, pa, pa, pb), 0, None, "ctx_nat")
    mixed = call(functools.partial(_ctx_b_kernel, layer=layer), 5,
                 [blk(A_SWQ), blk(A_SWK, 256), blk(A_SWV, 256),
                  pl.BlockSpec(memory_space=pltpu.SMEM), blk(B_GP + wide)],
                 (pa, pa, pa, sink, pb), 1, mixed, "ctx_swa")
    mixed = call(_ctx_c_kernel, 5,
                 [blk(0), blk(wide), blk(0, 2 * wide), blk(A_KPE, HEAD_DIM), blk(B_GP + 2 * wide)],
                 (q_up, q_up, kv_up, pa, pb), 2, mixed, "ctx_mla")
    mixed = call(functools.partial(_ctx_d_kernel, lam_init=lam_init), 6,
                 [blk(B_DQ), blk(B_DK), blk(B_DV), small((4, HEAD_DIM)), small((1, 2 * HEAD_DIM)),
                  blk(B_GP + 3 * wide)],
                 (pb, pb, pb, lam, gsub, pb), 3, mixed, "ctx_diff")
    return mixed


def _nat_win_start(n):
    n_rows = DEC_SEQ // GRID_W
    return min(max(2 * n - NAT_KR // 2, 0), n_rows - NAT_SPAN)


def _nat_pattern_key(n):
    n_rows = DEC_SEQ // GRID_W
    q_per = TQ_WIN // GRID_W
    s = _nat_win_start(n)
    return tuple((q_per * n + j - s, min(max(q_per * n + j - NAT_KR // 2, 0), n_rows - NAT_KR) - s)
                 for j in range(q_per))


NAT_PATTERN_STEPS = tuple(sorted({_nat_pattern_key(n): n for n in reversed(range(DEC_SEQ // TQ_WIN))}.values()))
NAT_PATTERN = tuple([_nat_pattern_key(m) for m in NAT_PATTERN_STEPS].index(_nat_pattern_key(n))
                    for n in range(DEC_SEQ // TQ_WIN))


def _lat_a_kernel(q_ref, k_ref, v_ref, ck_ref, cv_ref, bias_ref, gp_ref, o_ref):
    k = k_ref[...].astype(BF16)
    v = v_ref[...].astype(BF16)
    ck = ck_ref[...].astype(BF16)
    cv = cv_ref[...].astype(BF16)
    def scores(n):
        win = slice(_nat_win_start(n) * GRID_W, (_nat_win_start(n) + NAT_SPAN) * GRID_W)
        q = (q_ref[_head(n, TQ_WIN), :] * QSCALE).astype(BF16)
        return [(_nt(q, ck), cv), (_nt(q, k[win]) + bias_ref[NAT_PATTERN[n]], v[win])]

    def finish(n, parts):
        rows = _head(n, TQ_WIN)
        o_ref[rows, :] = _gated(_softmax_pv(parts), gp_ref[rows, :])

    _pipelined(DEC_SEQ // TQ_WIN, scores, finish)


def _swa_win_start(n):
    return min(max(n - 1, 0), DEC_SEQ // TQ_WIN - 3) * TQ_WIN


def _lat_b_kernel(q_ref, k_ref, v_ref, ck_ref, cv_ref, sink_ref, mask_ref, cos_ref, sl_ref, sh_ref,
                  gp_ref, o_ref, *, layer):
    k = _rope(k_ref[...], cos_ref[...], sl_ref[...], sh_ref[...], 32).astype(BF16)
    v = v_ref[...].astype(BF16)
    ck = ck_ref[...].astype(BF16)
    cv = cv_ref[...].astype(BF16)
    sink = sink_ref[layer, pl.program_id(1)] * LOG2E

    def scores(n):
        rows = _head(n, TQ_WIN)
        w0 = _swa_win_start(n)
        win = slice(w0, w0 + 3 * TQ_WIN)
        q = _rope(q_ref[rows, :], cos_ref[rows, :], sl_ref[rows, :], sh_ref[rows, :], 32)
        q = (q * QSCALE).astype(BF16)
        return [(_nt(q, ck), cv), (_nt(q, k[win]) + mask_ref[n - w0 // TQ_WIN], v[win])]

    def finish(n, parts):
        rows = _head(n, TQ_WIN)
        o_ref[rows, :] = _gated(_softmax_pv(parts, extra=sink), gp_ref[rows, :])

    _pipelined(DEC_SEQ // TQ_WIN, scores, finish)


def _lat_c_kernel(qn_ref, qp_ref, kn_ref, v_ref, kpe_ref, ckn_ref, cv_ref, ckpe_ref,
                  cos_ref, sl_ref, sh_ref, gp_ref, o_ref):
    kpe = _rope(kpe_ref[...], cos_ref[...], sl_ref[...], sh_ref[...], 16)
    keys = jnp.concatenate([
        jnp.concatenate([ckn_ref[...], ckpe_ref[...].astype(BF16)], axis=-1),
        jnp.concatenate([kn_ref[...], kpe.astype(BF16)], axis=-1)], axis=0)
    vals = jnp.concatenate([cv_ref[...], v_ref[...]], axis=0)

    def scores(n):
        rows = _head(n, TQ_DENSE)
        qp = _rope(qp_ref[rows, :], cos_ref[rows, :], sl_ref[rows, :], sh_ref[rows, :], 16)
        q = (jnp.concatenate([qn_ref[rows, :], qp], axis=-1) * QSCALE_MLA).astype(BF16)
        return [(_nt(q, keys), vals)]

    def finish(n, parts):
        rows = _head(n, TQ_DENSE)
        o_ref[rows, :] = _gated(_softmax_pv(parts), gp_ref[rows, :])

    _pipelined(DEC_SEQ // TQ_DENSE, scores, finish)


def _lat_d_kernel(q1_ref, q2_ref, k1_ref, k2_ref, v_ref, ck1_ref, ck2_ref, cv_ref,
                  lam_ref, gsub_ref, cos_ref, sl_ref, sh_ref, gp_ref, o_ref, *, lam_init):
    cos, sl, sh = cos_ref[...], sl_ref[...], sh_ref[...]
    keys1 = jnp.concatenate([ck1_ref[...], _rope(k1_ref[...], cos, sl, sh, 32)], axis=0).astype(BF16)
    keys2 = jnp.concatenate([ck2_ref[...], _rope(k2_ref[...], cos, sl, sh, 32)], axis=0).astype(BF16)
    vals = jnp.concatenate([cv_ref[...], v_ref[...]], axis=0).astype(BF16)
    lam = _diff_lambda(lam_ref, lam_init)
    gsub = gsub_ref[...]
    first_map = {}

    def scores(t):
        rows = _head(t // 2, TQ_DENSE)
        q_ref, keys = ((q1_ref, keys1), (q2_ref, keys2))[t % 2]
        q = _rope(q_ref[rows, :], cos_ref[rows, :], sl_ref[rows, :], sh_ref[rows, :], 32)
        return [(_nt((q * QSCALE).astype(BF16), keys), vals)]

    def finish(t, parts):
        o = _softmax_pv(parts)
        if t % 2 == 0:
            first_map[t // 2] = o
            return
        rows = _head(t // 2, TQ_DENSE)
        o = _diff_finish(first_map.pop(t // 2), o, lam, gsub, lam_init)
        o_ref[rows, :] = _gated(o, gp_ref[rows, :])

    _pipelined(2 * (DEC_SEQ // TQ_DENSE), scores, finish)


def _lat_attention(mixed, pa, pb, q_up, kv_up, caches, layer, nat_bias, swa_mask, sink, lam,
                   gsub, lam_init, rope128, rope64, lat, n_tok):
    nb = lat.rows // DEC_SEQ
    rb0 = lat.row0 // DEC_SEQ
    cb0 = n_tok // PAST_LEN
    c_nat_k, c_nat_v, c_swa_k, c_swa_v, c_kpe, c_dk, c_dv = caches
    hd = HEAD_DIM

    def rows(col_of, w=hd):
        return pl.BlockSpec((DEC_SEQ, w), lambda b, h: (rb0 + b, col_of(h)))

    def cache(col_of, w=hd):
        return pl.BlockSpec((None, None, PAST_LEN, w), lambda b, h: (b, layer, 0, col_of(h)))

    def table():
        return pl.BlockSpec((DEC_SEQ, hd), lambda b, h: (0, 0))

    def small(shape):
        return pl.BlockSpec((None,) + shape, lambda b, h: (layer,) + (0,) * len(shape))

    def call(body, n_in, heads, in_specs, args, mixed, w, col0, name):
        return _aliased_call(
            body, n_in, [mixed], grid=(nb, heads), in_specs=in_specs,
            out_specs=rows(lambda h: col0 // w + h, w),
            out_shape=jax.ShapeDtypeStruct((n_tok, D_MODEL), BF16),
            compiler_params=_cparams(2), name=name,
        )(*args, mixed)

    mixed = call(
        _lat_a_kernel, 7, 8,
        [rows(lambda h: A_NQ // hd + h), rows(lambda h: A_NK // hd + h), rows(lambda h: A_NV // hd + h),
         cache(lambda h: h), cache(lambda h: h),
         pl.BlockSpec((None, None, len(NAT_PATTERN_STEPS), TQ_WIN, NAT_SPAN * GRID_W),
                      lambda b, h: (layer, h, 0, 0, 0)),
         rows(lambda h: B_GP // hd + h)],
        (pa, pa, pa, c_nat_k, c_nat_v, nat_bias, pb), mixed, hd, 0, "lat_nat")
    mixed = call(
        functools.partial(_lat_b_kernel, layer=layer), 11, 8,
        [rows(lambda h: A_SWQ // hd + h), rows(lambda h: A_SWK // hd + h // 4),
         rows(lambda h: A_SWV // hd + h // 4), cache(lambda h: h // 4), cache(lambda h: h // 4),
         pl.BlockSpec(memory_space=pltpu.SMEM),
         pl.BlockSpec((3, TQ_WIN, 3 * TQ_WIN), lambda b, h: (0, 0, 0)), table(), table(), table(),
         rows(lambda h: B_GP // hd + 8 + h)],
        (pa, pa, pa, c_swa_k, c_swa_v, sink, swa_mask, *rope128, pb), mixed, hd, 8 * hd, "lat_swa")
    mixed = call(
        _lat_c_kernel, 12, 8,
        [rows(lambda h: h), rows(lambda h: 8 + h),
         rows(lambda h: 2 * h), rows(lambda h: 2 * h + 1), rows(lambda h: A_KPE // hd),
         pl.BlockSpec((PAST_LEN, hd), lambda b, h: (cb0 + b, 2 * h)),
         pl.BlockSpec((PAST_LEN, hd), lambda b, h: (cb0 + b, 2 * h + 1)),
         cache(lambda h: 0), table(), table(), table(),
         rows(lambda h: B_GP // hd + 16 + h)],
        (q_up, q_up, kv_up, kv_up, pa, kv_up, kv_up, c_kpe, *rope64, pb), mixed, hd, 16 * hd, "lat_mla")
    mixed = call(
        functools.partial(_lat_d_kernel, lam_init=lam_init), 14, 4,
        [rows(lambda h: B_DQ // hd + 2 * h), rows(lambda h: B_DQ // hd + 2 * h + 1),
         rows(lambda h: B_DK // hd + 2 * h), rows(lambda h: B_DK // hd + 2 * h + 1),
         rows(lambda h: B_DV // (2 * hd) + h, 2 * hd),
         cache(lambda h: 2 * h), cache(lambda h: 2 * h + 1), cache(lambda h: h, 2 * hd),
         small((4, hd)), small((1, 2 * hd)), table(), table(), table(),
         rows(lambda h: B_GP // (2 * hd) + 12 + h, 2 * hd)],
        (pb, pb, pb, pb, pb, c_dk, c_dk, c_dv, lam, gsub, *rope128, pb), mixed, 2 * hd, 24 * hd,
        "lat_diff")
    return mixed


def _state_kernel(nk_ref, nv_ref, sk_ref, sv_ref, ckv_ref, kpe_ref, dk_ref, dv_ref,
                  o_nk, o_nv, o_sk, o_sv, o_ckv, o_kpe, o_dk, o_dv):
    for src, dst, heads, w in ((nk_ref, o_nk, 8, HEAD_DIM), (nv_ref, o_nv, 8, HEAD_DIM),
                               (sk_ref, o_sk, 2, HEAD_DIM), (sv_ref, o_sv, 2, HEAD_DIM),
                               (dk_ref, o_dk, 4, 2 * HEAD_DIM), (dv_ref, o_dv, 4, 2 * HEAD_DIM)):
        for h in range(heads):
            dst[:, h, :] = src[:, h * w:(h + 1) * w]
    o_ckv[...] = ckv_ref[...]
    o_kpe[...] = kpe_ref[:, :MLA_ROPE]


def _write_states(pa, pb, ckvn, layer, depth, bc, prev):
    def src(col, w):
        return pl.BlockSpec((SEQ, w), lambda b: (b, col // w))

    def dst(*tail):
        return pl.BlockSpec((None, None, SEQ) + tail, lambda b: (b, layer, 0) + (0,) * len(tail))

    tails = [(8, HEAD_DIM), (8, HEAD_DIM), (2, HEAD_DIM), (2, HEAD_DIM), (MLA_KV_LORA,), (MLA_ROPE,),
             (4, 2 * HEAD_DIM), (4, 2 * HEAD_DIM)]
    return _aliased_call(
        _state_kernel, 8, list(prev), grid=(bc,),
        in_specs=[src(A_NK, 1024), src(A_NV, 1024), src(A_SWK, 256), src(A_SWV, 256),
                  pl.BlockSpec((SEQ, MLA_KV_LORA), lambda b: (b, 0)), src(A_KPE, HEAD_DIM),
                  src(B_DK, 1024), src(B_DV, 1024)],
        out_specs=[dst(*t) for t in tails],
        out_shape=[jax.ShapeDtypeStruct((bc, depth, SEQ) + t, F32) for t in tails],
        compiler_params=_cparams(1), name="write_states",
    )(pa, pa, pa, pa, ckvn, pa, pb, pb, *prev)


def _rope_tables(rot_dim):
    axis_dim = rot_dim // 2
    inv = 1.0 / (ROPE_BASE ** (jnp.arange(0, axis_dim, 2, dtype=F32) / axis_dim))
    t = jnp.arange(DEC_SEQ)
    ang_row = (t // GRID_W).astype(F32)[:, None] * inv
    ang_col = (t % GRID_W).astype(F32)[:, None] * inv
    zeros = jnp.zeros_like(ang_row)
    pad = HEAD_DIM - rot_dim
    cos = jnp.concatenate([jnp.cos(ang_row)] * 2 + [jnp.cos(ang_col)] * 2
                          + [jnp.ones((DEC_SEQ, pad), F32)], axis=-1)
    sin_lo = jnp.concatenate([-jnp.sin(ang_row), zeros, -jnp.sin(ang_col), zeros,
                              jnp.zeros((DEC_SEQ, pad), F32)], axis=-1)
    sin_hi = jnp.concatenate([zeros, jnp.sin(ang_row), zeros, jnp.sin(ang_col),
                              jnp.zeros((DEC_SEQ, pad), F32)], axis=-1)
    return cos, sin_lo, sin_hi


def _nat_bias(rpb):
    n_rows = DEC_SEQ // GRID_W
    steps = len(NAT_PATTERN_STEPS)
    q_per = TQ_WIN // GRID_W
    rep = jnp.asarray(NAT_PATTERN_STEPS)
    qr = rep[:, None] * q_per + jnp.arange(q_per)[None, :]
    kr = jnp.asarray([_nat_win_start(n) for n in NAT_PATTERN_STEPS])[:, None] + jnp.arange(NAT_SPAN)[None, :]
    c = jnp.arange(GRID_W)
    row0 = jnp.clip(qr - NAT_KR // 2, 0, n_rows - NAT_KR)
    col0 = jnp.clip(c - NAT_KC // 2, 0, GRID_W - NAT_KC)
    ok_r = (kr[:, None, :] >= row0[:, :, None]) & (kr[:, None, :] < row0[:, :, None] + NAT_KR)
    ok_c = (c[None, :] >= col0[:, None]) & (c[None, :] < col0[:, None] + NAT_KC)
    dr = jnp.clip(kr[:, None, :] - qr[:, :, None], -(NAT_KR - 1), NAT_KR - 1) + (NAT_KR - 1)
    dc = jnp.clip(c[None, :] - c[:, None], -(NAT_KC - 1), NAT_KC - 1) + (NAT_KC - 1)
    sel_r = jax.nn.one_hot(dr, 2 * NAT_KR - 1, dtype=F32)
    sel_c = jax.nn.one_hot(dc, 2 * NAT_KC - 1, dtype=F32)
    bias = jnp.einsum("nqkr,lhrc,xyc->lhnqxky", sel_r, rpb.astype(F32), sel_c,
                      precision=lax.Precision.HIGHEST)
    ok = ok_r[:, :, None, :, None] & ok_c[None, None, :, None, :]
    bias = jnp.where(ok[None, None], bias * LOG2E, NEG_INF)
    return bias.reshape(rpb.shape[0], rpb.shape[1], steps, TQ_WIN, NAT_SPAN * GRID_W)


def _swa_mask():
    qi = jnp.arange(TQ_WIN)[None, :, None] + TQ_WIN * jnp.arange(3)[:, None, None]
    kj = jnp.arange(3 * TQ_WIN)[None, None, :]
    return jnp.where(jnp.abs(qi - kj) <= SWA_WINDOW, 0.0, NEG_INF).astype(F32)


def _pack_w_uq(w_uq):
    depth = w_uq.shape[0]
    w = w_uq.reshape(depth, MLA_Q_LORA, 8, MLA_NOPE + MLA_ROPE)
    nope = w[..., :MLA_NOPE].reshape(depth, MLA_Q_LORA, 8 * MLA_NOPE)
    pe = jnp.pad(w[..., MLA_NOPE:], ((0, 0), (0, 0), (0, 0), (0, HEAD_DIM - MLA_ROPE)))
    return jnp.concatenate([nope, pe.reshape(depth, MLA_Q_LORA, 8 * HEAD_DIM)], axis=-1).astype(BF16)


def kernel(x_prompt, x_sample, cache_nat_k, cache_nat_v, cache_swa_k, cache_swa_v, cache_mla_ckv, cache_mla_kpe, cache_diff_k, cache_diff_v, c, c_ctx, w_mod, b_mod, g_pre, g_post, w_in, w_out, nat_rpb, swa_sink, mla_g_q, mla_g_kv, mla_w_uq, mla_w_ukv, diff_lambda, diff_g_subln):
    depth = w_mod.shape[0]
    bc, bl = x_prompt.shape[0], x_sample.shape[0]
    n_ctx, n_lat = bc * SEQ, bl * DEC_SEQ
    n_tok = n_ctx + n_lat
    assert n_ctx % DEC_SEQ == 0 and x_prompt.shape[1] == SEQ and x_sample.shape[1] == DEC_SEQ
    ctx = Group(0, n_ctx, lambda i: 0)
    lat = Group(n_ctx, n_lat, lambda i: 1 + i // (DEC_SEQ // ROW_BLOCK))

    n_cond = -(-(1 + bl) // 8) * 8
    cond = jnp.concatenate([c_ctx[None], c, jnp.zeros((n_cond - 1 - bl, D_MODEL), F32)], axis=0)
    mod = _modulation(cond, w_mod, b_mod).reshape(depth, n_cond, 3, D_MODEL)

    w_a = _pack_w_a(w_in)
    w_b = _pack_w_b(w_in)
    w_out_b = w_out.astype(BF16)
    w_uq_p = _pack_w_uq(mla_w_uq)
    w_ukv_b = mla_w_ukv.astype(BF16)
    g_pre3 = g_pre.reshape(depth, 1, D_MODEL)
    g_post3 = g_post.reshape(depth, 1, D_MODEL)
    g_q3 = mla_g_q.reshape(depth, 1, MLA_Q_LORA)
    g_kv3 = mla_g_kv.reshape(depth, 1, MLA_KV_LORA)
    gsub3 = diff_g_subln.reshape(depth, 1, 2 * HEAD_DIM)
    rope128 = _rope_tables(HEAD_DIM)
    rope64 = _rope_tables(MLA_ROPE)
    nat_bias = _nat_bias(nat_rpb)
    swa_mask = _swa_mask()

    caches = (cache_nat_k.reshape(bl, depth, PAST_LEN, -1), cache_nat_v.reshape(bl, depth, PAST_LEN, -1),
              cache_swa_k.reshape(bl, depth, PAST_LEN, -1), cache_swa_v.reshape(bl, depth, PAST_LEN, -1),
              jnp.pad(cache_mla_kpe, ((0, 0), (0, 0), (0, 0), (0, HEAD_DIM - MLA_ROPE))),
              cache_diff_k.reshape(bl, depth, PAST_LEN, -1), cache_diff_v.reshape(bl, depth, PAST_LEN, -1))

    x_c = x_prompt.reshape(n_ctx, D_MODEL)
    x_l = x_sample.reshape(n_lat, D_MODEL)
    h = _norm_mod(x_c, g_pre3, mod, 0, ctx, n_tok, None)
    h = _norm_mod(x_l, g_pre3, mod, 0, lat, n_tok, h)
    states = ()
    for l in range(depth):
        lam_init = 0.8 - 0.6 * math.exp(-0.3 * l)
        pa = _matmul(h, w_a, l, 768, "in_proj_a")
        pb = _matmul(h, w_b, l, 1024, "in_proj_b", n=N_B)
        cqn = _rms_cols(pa, A_CQ, 512, MLA_Q_LORA, g_q3, l, BF16)
        ckvn = _rms_cols(pa, A_CKV, 128, MLA_KV_LORA, g_kv3, l, F32)
        q_up = _matmul(cqn, w_uq_p, l, 1024, "mla_q_up")
        ckv_all = jnp.concatenate([ckvn, cache_mla_ckv[:, l].reshape(bl * PAST_LEN, MLA_KV_LORA)], axis=0)
        kv_up = _matmul(ckv_all.astype(BF16), w_ukv_b, l, 1024, "mla_kv_up", out_dtype=BF16)
        states = _write_states(pa, pb, ckvn, l, depth, bc, states)

        mixed = _ctx_attention(pa, pb, q_up, kv_up, swa_sink, diff_lambda, gsub3, l, lam_init, ctx, n_tok)
        mixed = _lat_attention(mixed, pa, pb, q_up, kv_up, caches, l, nat_bias, swa_mask, swa_sink,
                               diff_lambda, gsub3, lam_init, rope128, rope64, lat, n_tok)
        o = _matmul(mixed, w_out_b, l, 1024, "out_proj")
        if l + 1 < depth:
            x_c, h = _post_norm(x_c, o, g_post3, g_pre3, mod, l, ctx, n_tok, None)
            x_l, h = _post_norm(x_l, o, g_post3, g_pre3, mod, l, lat, n_tok, h)
        else:
            x_c = _post(x_c, o, g_post3, mod, l, ctx)
            x_l = _post(x_l, o, g_post3, mod, l, lat)

    return (x_c.reshape(bc, SEQ, D_MODEL), x_l.reshape(bl, DEC_SEQ, D_MODEL)) + tuple(states)
```

```python
import functools
import math
from typing import Callable, NamedTuple

import jax
import jax.numpy as jnp
from jax import lax
from jax.experimental import pallas as pl
from jax.experimental.pallas import tpu as pltpu

F32 = jnp.float32
BF16 = jnp.bfloat16

D_MODEL = 4096
HEAD_DIM = 128
GRID_W = 64
SEQ = 256
DEC_SEQ = 1024
PAST_LEN = 512
NAT_KR, NAT_KC = 8, 16
SWA_WINDOW = 128
MLA_NOPE, MLA_ROPE, MLA_V = 128, 64, 128
MLA_Q_LORA, MLA_KV_LORA = 896, 256
ROPE_BASE = 10000.0
EPS = 1e-6
NEG_INF = -1e30
LOG2E = 1.4426950408889634
QSCALE = HEAD_DIM ** -0.5 * LOG2E
QSCALE_MLA = (MLA_NOPE + MLA_ROPE) ** -0.5 * LOG2E

N_A = 6144
B_START = 5824
N_B = 7168
A_NQ, A_NK, A_NV = 0, 1024, 2048
A_SWQ, A_SWK, A_SWV = 3072, 4096, 4352
A_CQ, A_CKV, A_KPE = 4608, 5504, 5760
B_DQ, B_DK, B_DV, B_GP = 0, 1024, 2048, 3072

ROW_BLOCK = 256
TQ_WIN = 128
TQ_DENSE = 256
NAT_SPAN = 10
VMEM_LIMIT = 56 * 1024 * 1024


class Group(NamedTuple):
    row0: int
    rows: int
    mod_row: Callable


def _cparams(n_axes):
    return pltpu.CompilerParams(dimension_semantics=("arbitrary",) * n_axes,
                                vmem_limit_bytes=VMEM_LIMIT)


def _silu(x):
    return x * (1.0 / (1.0 + jnp.exp(-x)))


def _nt(a, b):
    return lax.dot_general(a, b, (((1,), (1,)), ((), ())), preferred_element_type=F32)


def _softmax_pv(parts, extra=None):
    m = parts[0][0].max(axis=-1, keepdims=True)
    for s, _ in parts[1:]:
        m = jnp.maximum(m, s.max(axis=-1, keepdims=True))
    if extra is not None:
        m = jnp.maximum(m, extra)
    den = None
    acc = None
    for s, v in parts:
        p = jnp.exp2(s - m)
        ps = p.sum(axis=-1, keepdims=True)
        pv = jnp.dot(p.astype(BF16), v, preferred_element_type=F32)
        den = ps if den is None else den + ps
        acc = pv if acc is None else acc + pv
    if extra is not None:
        den = den + jnp.exp2(extra - m)
    return acc / den


def _rope(x, cos, sin_lo, sin_hi, half):
    lanes = x.shape[-1]
    return (x * cos + pltpu.roll(x, lanes - half, 1) * sin_lo + pltpu.roll(x, half, 1) * sin_hi)


def _rms(x, g):
    return x * lax.rsqrt(jnp.mean(x * x, axis=-1, keepdims=True) + EPS) * g


def _aliased_call(body, n_in, alias_args, **kw):
    n_alias = len(alias_args)
    n_out = len(kw["out_shape"]) if isinstance(kw["out_shape"], (tuple, list)) else 1
    if n_alias == 0:
        return pl.pallas_call(body, **kw)

    def with_aliases(*refs):
        body(*refs[:n_in], *refs[n_in + n_alias:])

    kw["in_specs"] = list(kw["in_specs"]) + [pl.BlockSpec(memory_space=pl.ANY)] * n_alias
    kw["input_output_aliases"] = {n_in + k: n_out - n_alias + k for k in range(n_alias)}
    return pl.pallas_call(with_aliases, **kw)


def _mod_kernel(c_ref, w_ref, b_ref, o_ref):
    s = _silu(c_ref[...]).astype(BF16)
    o_ref[...] = jnp.dot(s, w_ref[...].astype(BF16), preferred_element_type=F32) + b_ref[...]


def _modulation(cond, w_mod, b_mod):
    depth = w_mod.shape[0]
    rows = cond.shape[0]
    tn = 512
    n = 3 * D_MODEL
    return pl.pallas_call(
        _mod_kernel,
        grid=(depth, n // tn),
        in_specs=[
            pl.BlockSpec((rows, D_MODEL), lambda l, j: (0, 0)),
            pl.BlockSpec((None, D_MODEL, tn), lambda l, j: (l, 0, j)),
            pl.BlockSpec((None, 1, tn), lambda l, j: (l, 0, j)),
        ],
        out_specs=pl.BlockSpec((None, rows, tn), lambda l, j: (l, 0, j)),
        out_shape=jax.ShapeDtypeStruct((depth, rows, n), F32),
        compiler_params=_cparams(2),
        name="modulation",
    )(cond, w_mod, b_mod.reshape(depth, 1, n))


def _mod_spec(grp, layer):
    return pl.BlockSpec((None, None, 3, D_MODEL), lambda i: (layer, grp.mod_row(i), 0, 0))


def _vec_spec(layer):
    return pl.BlockSpec((None, 1, D_MODEL), lambda i: (layer, 0, 0))


def _tok_spec(grp):
    return pl.BlockSpec((ROW_BLOCK, D_MODEL), lambda i: (grp.row0 // ROW_BLOCK + i, 0))


def _own_spec():
    return pl.BlockSpec((ROW_BLOCK, D_MODEL), lambda i: (i, 0))


def _norm_mod_kernel(x_ref, g_ref, m_ref, h_ref):
    y = _rms(x_ref[...], g_ref[...])
    h_ref[...] = (y * (1.0 + m_ref[1:2, :]) + m_ref[0:1, :]).astype(BF16)


def _norm_mod(x, g_pre, mod, layer, grp, n_tok, h_prev):
    return _aliased_call(
        _norm_mod_kernel, 3, [] if h_prev is None else [h_prev],
        grid=(grp.rows // ROW_BLOCK,),
        in_specs=[_own_spec(), _vec_spec(layer), _mod_spec(grp, layer)],
        out_specs=_tok_spec(grp),
        out_shape=jax.ShapeDtypeStruct((n_tok, D_MODEL), BF16),
        compiler_params=_cparams(1),
        name="norm_mod",
    )(x, g_pre, mod, *([] if h_prev is None else [h_prev]))


def _post_kernel(x_ref, o_ref, g_ref, m_ref, y_ref):
    y_ref[...] = x_ref[...] + m_ref[2:3, :] * _rms(o_ref[...].astype(F32), g_ref[...])


def _post(x, o, g_post, mod, layer, grp):
    return pl.pallas_call(
        _post_kernel,
        grid=(grp.rows // ROW_BLOCK,),
        in_specs=[_own_spec(), _tok_spec(grp), _vec_spec(layer), _mod_spec(grp, layer)],
        out_specs=_own_spec(),
        out_shape=jax.ShapeDtypeStruct((grp.rows, D_MODEL), F32),
        compiler_params=_cparams(1),
        name="post_residual",
    )(x, o, g_post, mod)


def _post_norm_kernel(x_ref, o_ref, g_ref, m_ref, gn_ref, mn_ref, y_ref, h_ref):
    y = x_ref[...] + m_ref[2:3, :] * _rms(o_ref[...].astype(F32), g_ref[...])
    y_ref[...] = y
    h_ref[...] = (_rms(y, gn_ref[...]) * (1.0 + mn_ref[1:2, :]) + mn_ref[0:1, :]).astype(BF16)


def _post_norm(x, o, g_post, g_pre, mod, layer, grp, n_tok, h_prev):
    return _aliased_call(
        _post_norm_kernel, 6, [] if h_prev is None else [h_prev],
        grid=(grp.rows // ROW_BLOCK,),
        in_specs=[_own_spec(), _tok_spec(grp), _vec_spec(layer), _mod_spec(grp, layer),
                  _vec_spec(layer + 1), _mod_spec(grp, layer + 1)],
        out_specs=[_own_spec(), _tok_spec(grp)],
        out_shape=[jax.ShapeDtypeStruct((grp.rows, D_MODEL), F32),
                   jax.ShapeDtypeStruct((n_tok, D_MODEL), BF16)],
        compiler_params=_cparams(1),
        name="post_norm",
    )(x, o, g_post, mod, g_pre, mod, *([] if h_prev is None else [h_prev]))


def _rms_cols_kernel(a_ref, b_ref, g_ref, y_ref, *, wa, wb):
    x = jnp.concatenate([a_ref[:, :wa], b_ref[:, :wb]], axis=-1)
    y_ref[...] = _rms(x, g_ref[...]).astype(y_ref.dtype)


def _rms_cols(p, col, block_w, width, g, layer, out_dtype):
    m = p.shape[0]
    tm = 1024
    first = col // block_w
    return pl.pallas_call(
        functools.partial(_rms_cols_kernel, wa=block_w, wb=width - block_w),
        grid=(m // tm,),
        in_specs=[
            pl.BlockSpec((tm, block_w), lambda i: (i, first)),
            pl.BlockSpec((tm, block_w), lambda i: (i, first + 1)),
            pl.BlockSpec((None, 1, width), lambda i: (layer, 0, 0)),
        ],
        out_specs=pl.BlockSpec((tm, width), lambda i: (i, 0)),
        out_shape=jax.ShapeDtypeStruct((m, width), out_dtype),
        compiler_params=_cparams(1),
        name="rms_cols",
    )(p, p, g)


def _mm_kernel(a_ref, b_ref, o_ref):
    o_ref[...] = jnp.dot(a_ref[...], b_ref[...], preferred_element_type=F32).astype(o_ref.dtype)


def _matmul(a, w, layer, tn, name, n=None, out_dtype=F32):
    m, k = a.shape
    n = w.shape[-1] if n is None else n
    tm = 1024 if m % 1024 == 0 else 512
    return pl.pallas_call(
        _mm_kernel,
        grid=(m // tm, n // tn),
        in_specs=[
            pl.BlockSpec((tm, k), lambda i, j: (i, 0)),
            pl.BlockSpec((None, k, tn), lambda i, j: (layer, 0, j)),
        ],
        out_specs=pl.BlockSpec((tm, tn), lambda i, j: (i, j)),
        out_shape=jax.ShapeDtypeStruct((m, n), out_dtype),
        compiler_params=_cparams(2),
        name=name,
    )(a, w)


def _cast_kernel(w_ref, o_ref):
    o_ref[...] = w_ref[...].astype(BF16)


def _cast_rows(w_t, row0, rows, block_rows, name):
    depth, _, k = w_t.shape
    first = row0 // block_rows
    assert row0 % block_rows == 0 and rows % block_rows == 0
    return pl.pallas_call(
        _cast_kernel,
        grid=(depth, rows // block_rows),
        in_specs=[pl.BlockSpec((None, block_rows, k), lambda l, j: (l, first + j, 0))],
        out_specs=pl.BlockSpec((None, block_rows, k), lambda l, j: (l, j, 0)),
        out_shape=jax.ShapeDtypeStruct((depth, rows, k), BF16),
        compiler_params=_cparams(2),
        name=name,
    )(w_t)


def _mm_nt_kernel(a_ref, b_ref, o_ref):
    o_ref[...] = _nt(a_ref[...], b_ref[...]).astype(o_ref.dtype)


def _matmul_nt(a, w_t, layer, tn, name):
    m, k = a.shape
    n = w_t.shape[1]
    tm = 1024 if m % 1024 == 0 else 512
    return pl.pallas_call(
        _mm_nt_kernel,
        grid=(m // tm, n // tn),
        in_specs=[
            pl.BlockSpec((tm, k), lambda i, j: (i, 0)),
            pl.BlockSpec((None, tn, k), lambda i, j: (layer, j, 0)),
        ],
        out_specs=pl.BlockSpec((tm, tn), lambda i, j: (i, j)),
        out_shape=jax.ShapeDtypeStruct((m, n), F32),
        compiler_params=_cparams(2),
        name=name,
    )(a, w_t)


def _gated(o, gp):
    return (o * _silu(gp)).astype(BF16)


def _pipelined(n_steps, scores, finish, ahead=1):
    queue = [scores(n) for n in range(min(ahead, n_steps))]
    for n in range(n_steps):
        cur = queue.pop(0)
        if n + ahead < n_steps:
            queue.append(scores(n + ahead))
        finish(n, cur)


def _head(h, w=HEAD_DIM):
    return slice(h * w, (h + 1) * w)


def _ctx_a_kernel(q_ref, k_ref, v_ref, gp_ref, o_ref):
    def scores(h):
        q = (q_ref[:, _head(h)] * QSCALE).astype(BF16)
        return [(_nt(q, k_ref[:, _head(h)].astype(BF16)), v_ref[:, _head(h)].astype(BF16))]

    def finish(h, parts):
        o_ref[:, _head(h)] = _gated(_softmax_pv(parts), gp_ref[:, _head(h)])

    _pipelined(8, scores, finish)


def _ctx_b_kernel(q_ref, k_ref, v_ref, sink_ref, gp_ref, o_ref, *, layer):
    def scores(h):
        q = (q_ref[:, _head(h)] * QSCALE).astype(BF16)
        return [(_nt(q, k_ref[:, _head(h // 4)].astype(BF16)), v_ref[:, _head(h // 4)].astype(BF16))]

    def finish(h, parts):
        o = _softmax_pv(parts, extra=sink_ref[layer, h] * LOG2E)
        o_ref[:, _head(h)] = _gated(o, gp_ref[:, _head(h)])

    _pipelined(8, scores, finish)


def _ctx_c_kernel(qn_ref, qp_ref, kv_ref, kpe_ref, gp_ref, o_ref):
    kpe = kpe_ref[...].astype(BF16)

    def scores(h):
        kn = kv_ref[:, _head(2 * h)].astype(BF16)
        q = (jnp.concatenate([qn_ref[:, _head(h)], qp_ref[:, _head(h)]], axis=-1) * QSCALE_MLA).astype(BF16)
        return [(_nt(q, jnp.concatenate([kn, kpe], axis=-1)), kv_ref[:, _head(2 * h + 1)].astype(BF16))]

    def finish(h, parts):
        o_ref[:, _head(h)] = _gated(_softmax_pv(parts), gp_ref[:, _head(h)])

    _pipelined(8, scores, finish)


def _diff_lambda(lam_ref, lam_init):
    lp = lam_ref[...]
    a = jnp.sum(lp[0:1, :] * lp[1:2, :], axis=-1, keepdims=True)
    b = jnp.sum(lp[2:3, :] * lp[3:4, :], axis=-1, keepdims=True)
    return jnp.exp(a) - jnp.exp(b) + lam_init


def _diff_finish(o1, o2, lam, gsub, lam_init):
    return _rms(o1 - lam * o2, gsub) * (1.0 - lam_init)


def _ctx_d_kernel(q_ref, k_ref, v_ref, lam_ref, gsub_ref, gp_ref, o_ref, *, lam_init):
    lam = _diff_lambda(lam_ref, lam_init)
    gsub = gsub_ref[...]

    first_map = {}

    def scores(t):
        q = (q_ref[:, _head(t)] * QSCALE).astype(BF16)
        return [(_nt(q, k_ref[:, _head(t)].astype(BF16)), v_ref[:, _head(t // 2, 2 * HEAD_DIM)].astype(BF16))]

    def finish(t, parts):
        o = _softmax_pv(parts)
        if t % 2 == 0:
            first_map[t // 2] = o
            return
        sl = _head(t // 2, 2 * HEAD_DIM)
        o = _diff_finish(first_map.pop(t // 2), o, lam, gsub, lam_init)
        o_ref[:, sl] = _gated(o, gp_ref[:, sl])

    _pipelined(8, scores, finish)


def _ctx_attention(pa, pb, q_up, kv_up, sink, lam, gsub, layer, lam_init, ctx, n_tok):
    nb = ctx.rows // SEQ
    wide = 8 * HEAD_DIM

    def blk(col, w=wide):
        return pl.BlockSpec((SEQ, w), lambda b: (b, col // w))

    def small(shape):
        return pl.BlockSpec((None,) + shape, lambda b: (layer,) + (0,) * len(shape))

    def call(body, n_in, in_specs, args, mixer, mixed, name):
        return _aliased_call(
            body, n_in, [] if mixed is None else [mixed], grid=(nb,), in_specs=in_specs,
            out_specs=pl.BlockSpec((SEQ, wide), lambda b: (b, mixer)),
            out_shape=jax.ShapeDtypeStruct((n_tok, D_MODEL), BF16),
            compiler_params=_cparams(1), name=name,
        )(*args, *([] if mixed is None else [mixed]))

    mixed = call(_ctx_a_kernel, 4, [blk(A_NQ), blk(A_NK), blk(A_NV), blk(B_GP)],
                 (pa, pa, pa, pb), 0, None, "ctx_nat")
    mixed = call(functools.partial(_ctx_b_kernel, layer=layer), 5,
                 [blk(A_SWQ), blk(A_SWK, 256), blk(A_SWV, 256),
                  pl.BlockSpec(memory_space=pltpu.SMEM), blk(B_GP + wide)],
                 (pa, pa, pa, sink, pb), 1, mixed, "ctx_swa")
    mixed = call(_ctx_c_kernel, 5,
                 [blk(0), blk(wide), blk(0, 2 * wide), blk(A_KPE, HEAD_DIM), blk(B_GP + 2 * wide)],
                 (q_up, q_up, kv_up, pa, pb), 2, mixed, "ctx_mla")
    mixed = call(functools.partial(_ctx_d_kernel, lam_init=lam_init), 6,
                 [blk(B_DQ), blk(B_DK), blk(B_DV), small((4, HEAD_DIM)), small((1, 2 * HEAD_DIM)),
                  blk(B_GP + 3 * wide)],
                 (pb, pb, pb, lam, gsub, pb), 3, mixed, "ctx_diff")
    return mixed


def _nat_win_start(n):
    n_rows = DEC_SEQ // GRID_W
    return min(max(2 * n - NAT_KR // 2, 0), n_rows - NAT_SPAN)


def _nat_pattern_key(n):
    n_rows = DEC_SEQ // GRID_W
    q_per = TQ_WIN // GRID_W
    s = _nat_win_start(n)
    return tuple((q_per * n + j - s, min(max(q_per * n + j - NAT_KR // 2, 0), n_rows - NAT_KR) - s)
                 for j in range(q_per))


NAT_PATTERN_STEPS = tuple(sorted({_nat_pattern_key(n): n for n in reversed(range(DEC_SEQ // TQ_WIN))}.values()))
NAT_PATTERN = tuple([_nat_pattern_key(m) for m in NAT_PATTERN_STEPS].index(_nat_pattern_key(n))
                    for n in range(DEC_SEQ // TQ_WIN))


def _lat_a_kernel(q_ref, k_ref, v_ref, ck_ref, cv_ref, bias_ref, gp_ref, o_ref):
    k = k_ref[...].astype(BF16)
    v = v_ref[...].astype(BF16)
    ck = ck_ref[...].astype(BF16)
    cv = cv_ref[...].astype(BF16)
    def scores(n):
        win = slice(_nat_win_start(n) * GRID_W, (_nat_win_start(n) + NAT_SPAN) * GRID_W)
        q = (q_ref[_head(n, TQ_WIN), :] * QSCALE).astype(BF16)
        return [(_nt(q, ck), cv), (_nt(q, k[win]) + bias_ref[NAT_PATTERN[n]], v[win])]

    def finish(n, parts):
        rows = _head(n, TQ_WIN)
        o_ref[rows, :] = _gated(_softmax_pv(parts), gp_ref[rows, :])

    _pipelined(DEC_SEQ // TQ_WIN, scores, finish, ahead=2)


def _swa_win_start(n):
    return min(max(n - 1, 0), DEC_SEQ // TQ_WIN - 3) * TQ_WIN


def _lat_b_kernel(q_ref, k_ref, v_ref, ck_ref, cv_ref, sink_ref, mask_ref, cos_ref, sl_ref, sh_ref,
                  gp_ref, o_ref, *, layer):
    k = _rope(k_ref[...], cos_ref[...], sl_ref[...], sh_ref[...], 32).astype(BF16)
    v = v_ref[...].astype(BF16)
    ck = ck_ref[...].astype(BF16)
    cv = cv_ref[...].astype(BF16)
    sink = sink_ref[layer, pl.program_id(1)] * LOG2E

    def scores(n):
        rows = _head(n, TQ_WIN)
        w0 = _swa_win_start(n)
        win = slice(w0, w0 + 3 * TQ_WIN)
        q = _rope(q_ref[rows, :], cos_ref[rows, :], sl_ref[rows, :], sh_ref[rows, :], 32)
        q = (q * QSCALE).astype(BF16)
        return [(_nt(q, ck), cv), (_nt(q, k[win]) + mask_ref[n - w0 // TQ_WIN], v[win])]

    def finish(n, parts):
        rows = _head(n, TQ_WIN)
        o_ref[rows, :] = _gated(_softmax_pv(parts, extra=sink), gp_ref[rows, :])

    _pipelined(DEC_SEQ // TQ_WIN, scores, finish, ahead=2)


def _lat_c_kernel(qn_ref, qp_ref, kn_ref, v_ref, kpe_ref, ckn_ref, cv_ref, ckpe_ref,
                  cos_ref, sl_ref, sh_ref, gp_ref, o_ref):
    kpe = _rope(kpe_ref[...], cos_ref[...], sl_ref[...], sh_ref[...], 16)
    keys = jnp.concatenate([
        jnp.concatenate([ckn_ref[...], ckpe_ref[...].astype(BF16)], axis=-1),
        jnp.concatenate([kn_ref[...], kpe.astype(BF16)], axis=-1)], axis=0)
    vals = jnp.concatenate([cv_ref[...], v_ref[...]], axis=0)

    def scores(n):
        rows = _head(n, TQ_DENSE)
        qp = _rope(qp_ref[rows, :], cos_ref[rows, :], sl_ref[rows, :], sh_ref[rows, :], 16)
        q = (jnp.concatenate([qn_ref[rows, :], qp], axis=-1) * QSCALE_MLA).astype(BF16)
        return [(_nt(q, keys), vals)]

    def finish(n, parts):
        rows = _head(n, TQ_DENSE)
        o_ref[rows, :] = _gated(_softmax_pv(parts), gp_ref[rows, :])

    _pipelined(DEC_SEQ // TQ_DENSE, scores, finish, ahead=2)


def _lat_d_kernel(q1_ref, q2_ref, k1_ref, k2_ref, v_ref, ck1_ref, ck2_ref, cv_ref,
                  lam_ref, gsub_ref, cos_ref, sl_ref, sh_ref, gp_ref, o_ref, *, lam_init):
    cos, sl, sh = cos_ref[...], sl_ref[...], sh_ref[...]
    keys1 = jnp.concatenate([ck1_ref[...], _rope(k1_ref[...], cos, sl, sh, 32)], axis=0).astype(BF16)
    keys2 = jnp.concatenate([ck2_ref[...], _rope(k2_ref[...], cos, sl, sh, 32)], axis=0).astype(BF16)
    vals = jnp.concatenate([cv_ref[...], v_ref[...]], axis=0).astype(BF16)
    lam = _diff_lambda(lam_ref, lam_init)
    gsub = gsub_ref[...]
    first_map = {}

    def scores(t):
        rows = _head(t // 2, TQ_DENSE)
        q_ref, keys = ((q1_ref, keys1), (q2_ref, keys2))[t % 2]
        q = _rope(q_ref[rows, :], cos_ref[rows, :], sl_ref[rows, :], sh_ref[rows, :], 32)
        return [(_nt((q * QSCALE).astype(BF16), keys), vals)]

    def finish(t, parts):
        o = _softmax_pv(parts)
        if t % 2 == 0:
            first_map[t // 2] = o
            return
        rows = _head(t // 2, TQ_DENSE)
        o = _diff_finish(first_map.pop(t // 2), o, lam, gsub, lam_init)
        o_ref[rows, :] = _gated(o, gp_ref[rows, :])

    _pipelined(2 * (DEC_SEQ // TQ_DENSE), scores, finish)


def _lat_attention(mixed, pa, pb, q_up, kv_up, caches, layer, nat_bias, swa_mask, sink, lam,
                   gsub, lam_init, rope128, rope64, lat, n_tok):
    nb = lat.rows // DEC_SEQ
    rb0 = lat.row0 // DEC_SEQ
    cb0 = n_tok // PAST_LEN
    c_nat_k, c_nat_v, c_swa_k, c_swa_v, c_kpe, c_dk, c_dv = caches
    hd = HEAD_DIM

    def rows(col_of, w=hd):
        return pl.BlockSpec((DEC_SEQ, w), lambda b, h: (rb0 + b, col_of(h)))

    def cache(col_of, w=hd):
        return pl.BlockSpec((None, None, PAST_LEN, w), lambda b, h: (b, layer, 0, col_of(h)))

    def table():
        return pl.BlockSpec((DEC_SEQ, hd), lambda b, h: (0, 0))

    def small(shape):
        return pl.BlockSpec((None,) + shape, lambda b, h: (layer,) + (0,) * len(shape))

    def call(body, n_in, heads, in_specs, args, mixed, w, col0, name):
        return _aliased_call(
            body, n_in, [mixed], grid=(nb, heads), in_specs=in_specs,
            out_specs=rows(lambda h: col0 // w + h, w),
            out_shape=jax.ShapeDtypeStruct((n_tok, D_MODEL), BF16),
            compiler_params=_cparams(2), name=name,
        )(*args, mixed)

    mixed = call(
        _lat_a_kernel, 7, 8,
        [rows(lambda h: A_NQ // hd + h), rows(lambda h: A_NK // hd + h), rows(lambda h: A_NV // hd + h),
         cache(lambda h: h), cache(lambda h: h),
         pl.BlockSpec((None, None, len(NAT_PATTERN_STEPS), TQ_WIN, NAT_SPAN * GRID_W),
                      lambda b, h: (layer, h, 0, 0, 0)),
         rows(lambda h: B_GP // hd + h)],
        (pa, pa, pa, c_nat_k, c_nat_v, nat_bias, pb), mixed, hd, 0, "lat_nat")
    mixed = call(
        functools.partial(_lat_b_kernel, layer=layer), 11, 8,
        [rows(lambda h: A_SWQ // hd + h), rows(lambda h: A_SWK // hd + h // 4),
         rows(lambda h: A_SWV // hd + h // 4), cache(lambda h: h // 4), cache(lambda h: h // 4),
         pl.BlockSpec(memory_space=pltpu.SMEM),
         pl.BlockSpec((3, TQ_WIN, 3 * TQ_WIN), lambda b, h: (0, 0, 0)), table(), table(), table(),
         rows(lambda h: B_GP // hd + 8 + h)],
        (pa, pa, pa, c_swa_k, c_swa_v, sink, swa_mask, *rope128, pb), mixed, hd, 8 * hd, "lat_swa")
    mixed = call(
        _lat_c_kernel, 12, 8,
        [rows(lambda h: h), rows(lambda h: 8 + h),
         rows(lambda h: 2 * h), rows(lambda h: 2 * h + 1), rows(lambda h: A_KPE // hd),
         pl.BlockSpec((PAST_LEN, hd), lambda b, h: (cb0 + b, 2 * h)),
         pl.BlockSpec((PAST_LEN, hd), lambda b, h: (cb0 + b, 2 * h + 1)),
         cache(lambda h: 0), table(), table(), table(),
         rows(lambda h: B_GP // hd + 16 + h)],
        (q_up, q_up, kv_up, kv_up, pa, kv_up, kv_up, c_kpe, *rope64, pb), mixed, hd, 16 * hd, "lat_mla")
    mixed = call(
        functools.partial(_lat_d_kernel, lam_init=lam_init), 14, 4,
        [rows(lambda h: B_DQ // hd + 2 * h), rows(lambda h: B_DQ // hd + 2 * h + 1),
         rows(lambda h: B_DK // hd + 2 * h), rows(lambda h: B_DK // hd + 2 * h + 1),
         rows(lambda h: B_DV // (2 * hd) + h, 2 * hd),
         cache(lambda h: 2 * h), cache(lambda h: 2 * h + 1), cache(lambda h: h, 2 * hd),
         small((4, hd)), small((1, 2 * hd)), table(), table(), table(),
         rows(lambda h: B_GP // (2 * hd) + 12 + h, 2 * hd)],
        (pb, pb, pb, pb, pb, c_dk, c_dk, c_dv, lam, gsub, *rope128, pb), mixed, 2 * hd, 24 * hd,
        "lat_diff")
    return mixed


def _state_kernel(nk_ref, nv_ref, sk_ref, sv_ref, ckv_ref, kpe_ref, dk_ref, dv_ref,
                  o_nk, o_nv, o_sk, o_sv, o_ckv, o_kpe, o_dk, o_dv):
    for src, dst, heads, w in ((nk_ref, o_nk, 8, HEAD_DIM), (nv_ref, o_nv, 8, HEAD_DIM),
                               (sk_ref, o_sk, 2, HEAD_DIM), (sv_ref, o_sv, 2, HEAD_DIM),
                               (dk_ref, o_dk, 4, 2 * HEAD_DIM), (dv_ref, o_dv, 4, 2 * HEAD_DIM)):
        for h in range(heads):
            dst[:, h, :] = src[:, h * w:(h + 1) * w]
    o_ckv[...] = ckv_ref[...]
    o_kpe[...] = kpe_ref[:, :MLA_ROPE]


def _write_states(pa, pb, ckvn, layer, depth, bc, prev):
    def src(col, w):
        return pl.BlockSpec((SEQ, w), lambda b: (b, col // w))

    def dst(*tail):
        return pl.BlockSpec((None, None, SEQ) + tail, lambda b: (b, layer, 0) + (0,) * len(tail))

    tails = [(8, HEAD_DIM), (8, HEAD_DIM), (2, HEAD_DIM), (2, HEAD_DIM), (MLA_KV_LORA,), (MLA_ROPE,),
             (4, 2 * HEAD_DIM), (4, 2 * HEAD_DIM)]
    return _aliased_call(
        _state_kernel, 8, list(prev), grid=(bc,),
        in_specs=[src(A_NK, 1024), src(A_NV, 1024), src(A_SWK, 256), src(A_SWV, 256),
                  pl.BlockSpec((SEQ, MLA_KV_LORA), lambda b: (b, 0)), src(A_KPE, HEAD_DIM),
                  src(B_DK, 1024), src(B_DV, 1024)],
        out_specs=[dst(*t) for t in tails],
        out_shape=[jax.ShapeDtypeStruct((bc, depth, SEQ) + t, F32) for t in tails],
        compiler_params=_cparams(1), name="write_states",
    )(pa, pa, pa, pa, ckvn, pa, pb, pb, *prev)


def _rope_tables(rot_dim):
    axis_dim = rot_dim // 2
    inv = 1.0 / (ROPE_BASE ** (jnp.arange(0, axis_dim, 2, dtype=F32) / axis_dim))
    t = jnp.arange(DEC_SEQ)
    ang_row = (t // GRID_W).astype(F32)[:, None] * inv
    ang_col = (t % GRID_W).astype(F32)[:, None] * inv
    zeros = jnp.zeros_like(ang_row)
    pad = HEAD_DIM - rot_dim
    cos = jnp.concatenate([jnp.cos(ang_row)] * 2 + [jnp.cos(ang_col)] * 2
                          + [jnp.ones((DEC_SEQ, pad), F32)], axis=-1)
    sin_lo = jnp.concatenate([-jnp.sin(ang_row), zeros, -jnp.sin(ang_col), zeros,
                              jnp.zeros((DEC_SEQ, pad), F32)], axis=-1)
    sin_hi = jnp.concatenate([zeros, jnp.sin(ang_row), zeros, jnp.sin(ang_col),
                              jnp.zeros((DEC_SEQ, pad), F32)], axis=-1)
    return cos, sin_lo, sin_hi


def _nat_bias(rpb):
    n_rows = DEC_SEQ // GRID_W
    steps = len(NAT_PATTERN_STEPS)
    q_per = TQ_WIN // GRID_W
    rep = jnp.asarray(NAT_PATTERN_STEPS)
    qr = rep[:, None] * q_per + jnp.arange(q_per)[None, :]
    kr = jnp.asarray([_nat_win_start(n) for n in NAT_PATTERN_STEPS])[:, None] + jnp.arange(NAT_SPAN)[None, :]
    c = jnp.arange(GRID_W)
    row0 = jnp.clip(qr - NAT_KR // 2, 0, n_rows - NAT_KR)
    col0 = jnp.clip(c - NAT_KC // 2, 0, GRID_W - NAT_KC)
    ok_r = (kr[:, None, :] >= row0[:, :, None]) & (kr[:, None, :] < row0[:, :, None] + NAT_KR)
    ok_c = (c[None, :] >= col0[:, None]) & (c[None, :] < col0[:, None] + NAT_KC)
    dr = jnp.clip(kr[:, None, :] - qr[:, :, None], -(NAT_KR - 1), NAT_KR - 1) + (NAT_KR - 1)
    dc = jnp.clip(c[None, :] - c[:, None], -(NAT_KC - 1), NAT_KC - 1) + (NAT_KC - 1)
    sel_r = jax.nn.one_hot(dr, 2 * NAT_KR - 1, dtype=F32)
    sel_c = jax.nn.one_hot(dc, 2 * NAT_KC - 1, dtype=F32)
    bias = jnp.einsum("nqkr,lhrc,xyc->lhnqxky", sel_r, rpb.astype(F32), sel_c,
                      precision=lax.Precision.HIGHEST)
    ok = ok_r[:, :, None, :, None] & ok_c[None, None, :, None, :]
    bias = jnp.where(ok[None, None], bias * LOG2E, NEG_INF)
    return bias.reshape(rpb.shape[0], rpb.shape[1], steps, TQ_WIN, NAT_SPAN * GRID_W)


def _swa_mask():
    qi = jnp.arange(TQ_WIN)[None, :, None] + TQ_WIN * jnp.arange(3)[:, None, None]
    kj = jnp.arange(3 * TQ_WIN)[None, None, :]
    return jnp.where(jnp.abs(qi - kj) <= SWA_WINDOW, 0.0, NEG_INF).astype(F32)


def _pack_w_uq(w_uq):
    depth = w_uq.shape[0]
    w = w_uq.reshape(depth, MLA_Q_LORA, 8, MLA_NOPE + MLA_ROPE)
    nope = w[..., :MLA_NOPE].reshape(depth, MLA_Q_LORA, 8 * MLA_NOPE)
    pe = jnp.pad(w[..., MLA_NOPE:], ((0, 0), (0, 0), (0, 0), (0, HEAD_DIM - MLA_ROPE)))
    return jnp.concatenate([nope, pe.reshape(depth, MLA_Q_LORA, 8 * HEAD_DIM)], axis=-1).astype(BF16)


def kernel(x_prompt, x_sample, cache_nat_k, cache_nat_v, cache_swa_k, cache_swa_v, cache_mla_ckv, cache_mla_kpe, cache_diff_k, cache_diff_v, c, c_ctx, w_mod, b_mod, g_pre, g_post, w_in, w_out, nat_rpb, swa_sink, mla_g_q, mla_g_kv, mla_w_uq, mla_w_ukv, diff_lambda, diff_g_subln):
    depth = w_mod.shape[0]
    bc, bl = x_prompt.shape[0], x_sample.shape[0]
    n_ctx, n_lat = bc * SEQ, bl * DEC_SEQ
    n_tok = n_ctx + n_lat
    assert n_ctx % DEC_SEQ == 0 and x_prompt.shape[1] == SEQ and x_sample.shape[1] == DEC_SEQ
    ctx = Group(0, n_ctx, lambda i: 0)
    lat = Group(n_ctx, n_lat, lambda i: 1 + i // (DEC_SEQ // ROW_BLOCK))

    n_cond = -(-(1 + bl) // 8) * 8
    cond = jnp.concatenate([c_ctx[None], c, jnp.zeros((n_cond - 1 - bl, D_MODEL), F32)], axis=0)
    mod = _modulation(cond, w_mod, b_mod).reshape(depth, n_cond, 3, D_MODEL)

    w_t = jnp.swapaxes(w_in, 1, 2)
    w_a = _cast_rows(w_t, 0, N_A, 1024, "pack_w_a")
    w_b = _cast_rows(w_t, B_START, N_B, 448, "pack_w_b")
    w_out_b = w_out.astype(BF16)
    w_uq_p = _pack_w_uq(mla_w_uq)
    w_ukv_b = mla_w_ukv.astype(BF16)
    g_pre3 = g_pre.reshape(depth, 1, D_MODEL)
    g_post3 = g_post.reshape(depth, 1, D_MODEL)
    g_q3 = mla_g_q.reshape(depth, 1, MLA_Q_LORA)
    g_kv3 = mla_g_kv.reshape(depth, 1, MLA_KV_LORA)
    gsub3 = diff_g_subln.reshape(depth, 1, 2 * HEAD_DIM)
    rope128 = _rope_tables(HEAD_DIM)
    rope64 = _rope_tables(MLA_ROPE)
    nat_bias = _nat_bias(nat_rpb)
    swa_mask = _swa_mask()

    caches = (cache_nat_k.reshape(bl, depth, PAST_LEN, -1), cache_nat_v.reshape(bl, depth, PAST_LEN, -1),
              cache_swa_k.reshape(bl, depth, PAST_LEN, -1), cache_swa_v.reshape(bl, depth, PAST_LEN, -1),
              jnp.pad(cache_mla_kpe, ((0, 0), (0, 0), (0, 0), (0, HEAD_DIM - MLA_ROPE))),
              cache_diff_k.reshape(bl, depth, PAST_LEN, -1), cache_diff_v.reshape(bl, depth, PAST_LEN, -1))

    x_c = x_prompt.reshape(n_ctx, D_MODEL)
    x_l = x_sample.reshape(n_lat, D_MODEL)
    h = _norm_mod(x_c, g_pre3, mod, 0, ctx, n_tok, None)
    h = _norm_mod(x_l, g_pre3, mod, 0, lat, n_tok, h)
    states = ()
    for l in range(depth):
        lam_init = 0.8 - 0.6 * math.exp(-0.3 * l)
        pa = _matmul_nt(h, w_a, l, 768, "in_proj_a")
        pb = _matmul_nt(h, w_b, l, 1024, "in_proj_b")
        cqn = _rms_cols(pa, A_CQ, 512, MLA_Q_LORA, g_q3, l, BF16)
        ckvn = _rms_cols(pa, A_CKV, 128, MLA_KV_LORA, g_kv3, l, F32)
        q_up = _matmul(cqn, w_uq_p, l, 1024, "mla_q_up")
        ckv_all = jnp.concatenate([ckvn, cache_mla_ckv[:, l].reshape(bl * PAST_LEN, MLA_KV_LORA)], axis=0)
        kv_up = _matmul(ckv_all.astype(BF16), w_ukv_b, l, 1024, "mla_kv_up", out_dtype=BF16)
        states = _write_states(pa, pb, ckvn, l, depth, bc, states)

        mixed = _ctx_attention(pa, pb, q_up, kv_up, swa_sink, diff_lambda, gsub3, l, lam_init, ctx, n_tok)
        mixed = _lat_attention(mixed, pa, pb, q_up, kv_up, caches, l, nat_bias, swa_mask, swa_sink,
                               diff_lambda, gsub3, lam_init, rope128, rope64, lat, n_tok)
        o = _matmul(mixed, w_out_b, l, 1024, "out_proj", out_dtype=BF16)
        if l + 1 < depth:
            x_c, h = _post_norm(x_c, o, g_post3, g_pre3, mod, l, ctx, n_tok, None)
            x_l, h = _post_norm(x_l, o, g_post3, g_pre3, mod, l, lat, n_tok, h)
        else:
            x_c = _post(x_c, o, g_post3, mod, l, ctx)
            x_l = _post(x_l, o, g_post3, mod, l, lat)

    return (x_c.reshape(bc, SEQ, D_MODEL), x_l.reshape(bl, DEC_SEQ, D_MODEL)) + tuple(states)
```

```python
import functools
import math
from typing import Callable, NamedTuple

import jax
import jax.numpy as jnp
from jax import lax
from jax.experimental import pallas as pl
from jax.experimental.pallas import tpu as pltpu

F32 = jnp.float32
BF16 = jnp.bfloat16

D_MODEL = 4096
HEAD_DIM = 128
GRID_W = 64
SEQ = 256
DEC_SEQ = 1024
PAST_LEN = 512
NAT_KR, NAT_KC = 8, 16
SWA_WINDOW = 128
MLA_NOPE, MLA_ROPE, MLA_V = 128, 64, 128
MLA_Q_LORA, MLA_KV_LORA = 896, 256
ROPE_BASE = 10000.0
EPS = 1e-6
NEG_INF = -1e30
LOG2E = 1.4426950408889634
QSCALE = HEAD_DIM ** -0.5 * LOG2E
QSCALE_MLA = (MLA_NOPE + MLA_ROPE) ** -0.5 * LOG2E

B_START = 5824
N_B = 7168
N_A = 5888
N_PROJ = N_B + N_A
B_DQ, B_DK, B_DV, B_GP = 0, 1024, 2048, 3072
A_NQ, A_NK, A_NV = N_B + 0, N_B + 1024, N_B + 2048
A_SWQ, A_SWK, A_SWV = N_B + 3072, N_B + 4096, N_B + 4352
A_CQ, A_CKV, A_KPE = N_B + 4608, N_B + 5504, N_B + 5760

ROW_BLOCK = 256
TQ_WIN = 128
TQ_DENSE = 256
NAT_SPAN = 10
VMEM_LIMIT = 56 * 1024 * 1024


class Group(NamedTuple):
    row0: int
    rows: int
    mod_row: Callable


def _cparams(n_axes):
    return pltpu.CompilerParams(dimension_semantics=("arbitrary",) * n_axes,
                                vmem_limit_bytes=VMEM_LIMIT)


def _silu(x):
    return x * (1.0 / (1.0 + jnp.exp(-x)))


def _nt(a, b):
    return lax.dot_general(a, b, (((1,), (1,)), ((), ())), preferred_element_type=F32)


def _softmax_pv(parts, extra=None):
    m = parts[0][0].max(axis=-1, keepdims=True)
    for s, _ in parts[1:]:
        m = jnp.maximum(m, s.max(axis=-1, keepdims=True))
    if extra is not None:
        m = jnp.maximum(m, extra)
    den = None
    acc = None
    for s, v in parts:
        p = jnp.exp2(s - m)
        ps = p.sum(axis=-1, keepdims=True)
        pv = jnp.dot(p.astype(BF16), v, preferred_element_type=F32)
        den = ps if den is None else den + ps
        acc = pv if acc is None else acc + pv
    if extra is not None:
        den = den + jnp.exp2(extra - m)
    return acc / den


def _rope(x, cos, sin_lo, sin_hi, half):
    lanes = x.shape[-1]
    return (x * cos + pltpu.roll(x, lanes - half, 1) * sin_lo + pltpu.roll(x, half, 1) * sin_hi)


def _rms(x, g):
    return x * lax.rsqrt(jnp.mean(x * x, axis=-1, keepdims=True) + EPS) * g


def _aliased_call(body, n_in, alias_args, **kw):
    n_alias = len(alias_args)
    n_out = len(kw["out_shape"]) if isinstance(kw["out_shape"], (tuple, list)) else 1
    if n_alias == 0:
        return pl.pallas_call(body, **kw)

    def with_aliases(*refs):
        body(*refs[:n_in], *refs[n_in + n_alias:])

    kw["in_specs"] = list(kw["in_specs"]) + [pl.BlockSpec(memory_space=pl.ANY)] * n_alias
    kw["input_output_aliases"] = {n_in + k: n_out - n_alias + k for k in range(n_alias)}
    return pl.pallas_call(with_aliases, **kw)


def _mod_kernel(c_ref, w_ref, b_ref, o_ref):
    s = _silu(c_ref[...]).astype(BF16)
    o_ref[...] = jnp.dot(s, w_ref[...].astype(BF16), preferred_element_type=F32) + b_ref[...]


def _modulation(cond, w_mod, b_mod):
    depth = w_mod.shape[0]
    rows = cond.shape[0]
    tn = 512
    n = 3 * D_MODEL
    return pl.pallas_call(
        _mod_kernel,
        grid=(depth, n // tn),
        in_specs=[
            pl.BlockSpec((rows, D_MODEL), lambda l, j: (0, 0)),
            pl.BlockSpec((None, D_MODEL, tn), lambda l, j: (l, 0, j)),
            pl.BlockSpec((None, 1, tn), lambda l, j: (l, 0, j)),
        ],
        out_specs=pl.BlockSpec((None, rows, tn), lambda l, j: (l, 0, j)),
        out_shape=jax.ShapeDtypeStruct((depth, rows, n), F32),
        compiler_params=_cparams(2),
        name="modulation",
    )(cond, w_mod, b_mod.reshape(depth, 1, n))


def _mod_spec(grp, layer):
    return pl.BlockSpec((None, None, 3, D_MODEL), lambda i: (layer, grp.mod_row(i), 0, 0))


def _vec_spec(layer):
    return pl.BlockSpec((None, 1, D_MODEL), lambda i: (layer, 0, 0))


def _tok_spec(grp):
    return pl.BlockSpec((ROW_BLOCK, D_MODEL), lambda i: (grp.row0 // ROW_BLOCK + i, 0))


def _own_spec():
    return pl.BlockSpec((ROW_BLOCK, D_MODEL), lambda i: (i, 0))


def _norm_mod_kernel(x_ref, g_ref, m_ref, h_ref):
    y = _rms(x_ref[...], g_ref[...])
    h_ref[...] = (y * (1.0 + m_ref[1:2, :]) + m_ref[0:1, :]).astype(BF16)


def _norm_mod(x, g_pre, mod, layer, grp, n_tok, h_prev):
    return _aliased_call(
        _norm_mod_kernel, 3, [] if h_prev is None else [h_prev],
        grid=(grp.rows // ROW_BLOCK,),
        in_specs=[_own_spec(), _vec_spec(layer), _mod_spec(grp, layer)],
        out_specs=_tok_spec(grp),
        out_shape=jax.ShapeDtypeStruct((n_tok, D_MODEL), BF16),
        compiler_params=_cparams(1),
        name="norm_mod",
    )(x, g_pre, mod, *([] if h_prev is None else [h_prev]))


def _post_kernel(x_ref, o_ref, g_ref, m_ref, y_ref):
    y_ref[...] = x_ref[...] + m_ref[2:3, :] * _rms(o_ref[...].astype(F32), g_ref[...])


def _post(x, o, g_post, mod, layer, grp):
    return pl.pallas_call(
        _post_kernel,
        grid=(grp.rows // ROW_BLOCK,),
        in_specs=[_own_spec(), _tok_spec(grp), _vec_spec(layer), _mod_spec(grp, layer)],
        out_specs=_own_spec(),
        out_shape=jax.ShapeDtypeStruct((grp.rows, D_MODEL), F32),
        compiler_params=_cparams(1),
        name="post_residual",
    )(x, o, g_post, mod)


def _post_norm_kernel(x_ref, o_ref, g_ref, m_ref, gn_ref, mn_ref, y_ref, h_ref):
    y = x_ref[...] + m_ref[2:3, :] * _rms(o_ref[...].astype(F32), g_ref[...])
    y_ref[...] = y
    h_ref[...] = (_rms(y, gn_ref[...]) * (1.0 + mn_ref[1:2, :]) + mn_ref[0:1, :]).astype(BF16)


def _post_norm(x, o, g_post, g_pre, mod, layer, grp, n_tok, h_prev):
    return _aliased_call(
        _post_norm_kernel, 6, [] if h_prev is None else [h_prev],
        grid=(grp.rows // ROW_BLOCK,),
        in_specs=[_own_spec(), _tok_spec(grp), _vec_spec(layer), _mod_spec(grp, layer),
                  _vec_spec(layer + 1), _mod_spec(grp, layer + 1)],
        out_specs=[_own_spec(), _tok_spec(grp)],
        out_shape=[jax.ShapeDtypeStruct((grp.rows, D_MODEL), F32),
                   jax.ShapeDtypeStruct((n_tok, D_MODEL), BF16)],
        compiler_params=_cparams(1),
        name="post_norm",
    )(x, o, g_post, mod, g_pre, mod, *([] if h_prev is None else [h_prev]))


def _rms_cols_kernel(a_ref, b_ref, g_ref, y_ref, *, wa, wb):
    x = jnp.concatenate([a_ref[:, :wa], b_ref[:, :wb]], axis=-1)
    y_ref[...] = _rms(x, g_ref[...]).astype(y_ref.dtype)


def _rms_cols(p, col, block_w, width, g, layer, out_dtype):
    m = p.shape[0]
    tm = 1024
    first = col // block_w
    return pl.pallas_call(
        functools.partial(_rms_cols_kernel, wa=block_w, wb=width - block_w),
        grid=(m // tm,),
        in_specs=[
            pl.BlockSpec((tm, block_w), lambda i: (i, first)),
            pl.BlockSpec((tm, block_w), lambda i: (i, first + 1)),
            pl.BlockSpec((None, 1, width), lambda i: (layer, 0, 0)),
        ],
        out_specs=pl.BlockSpec((tm, width), lambda i: (i, 0)),
        out_shape=jax.ShapeDtypeStruct((m, width), out_dtype),
        compiler_params=_cparams(1),
        name="rms_cols",
    )(p, p, g)


def _q_up_kernel(a_ref, b_ref, g_ref, w_ref, o_ref):
    cq = jnp.concatenate([a_ref[...], b_ref[:, :MLA_Q_LORA - 512]], axis=-1)
    o_ref[...] = jnp.dot(_rms(cq, g_ref[...]).astype(BF16), w_ref[...], preferred_element_type=F32)


def _mla_q_up(p, g, w_uq, layer):
    m = p.shape[0]
    tm = 512
    n = w_uq.shape[-1]
    first = A_CQ // 512
    return pl.pallas_call(
        _q_up_kernel,
        grid=(m // tm,),
        in_specs=[
            pl.BlockSpec((tm, 512), lambda i: (i, first)),
            pl.BlockSpec((tm, 512), lambda i: (i, first + 1)),
            pl.BlockSpec((None, 1, MLA_Q_LORA), lambda i: (layer, 0, 0)),
            pl.BlockSpec((None, MLA_Q_LORA, n), lambda i: (layer, 0, 0)),
        ],
        out_specs=pl.BlockSpec((tm, n), lambda i: (i, 0)),
        out_shape=jax.ShapeDtypeStruct((m, n), F32),
        compiler_params=_cparams(1),
        name="mla_q_up",
    )(p, p, g, w_uq)


def _mm_kernel(a_ref, b_ref, o_ref):
    o_ref[...] = jnp.dot(a_ref[...], b_ref[...], preferred_element_type=F32).astype(o_ref.dtype)


def _matmul(a, w, layer, tn, name, n=None, out_dtype=F32):
    m, k = a.shape
    n = w.shape[-1] if n is None else n
    tm = 1024 if m % 1024 == 0 else 512
    return pl.pallas_call(
        _mm_kernel,
        grid=(m // tm, n // tn),
        in_specs=[
            pl.BlockSpec((tm, k), lambda i, j: (i, 0)),
            pl.BlockSpec((None, k, tn), lambda i, j: (layer, 0, j)),
        ],
        out_specs=pl.BlockSpec((tm, tn), lambda i, j: (i, j)),
        out_shape=jax.ShapeDtypeStruct((m, n), out_dtype),
        compiler_params=_cparams(2),
        name=name,
    )(a, w)


def _cast_kernel(w_ref, o_ref):
    o_ref[...] = w_ref[...].astype(BF16)


def _cast_rows(w_t, row0, rows, block_rows, out_row0, out_rows, prev, name):
    depth, _, k = w_t.shape
    first, out_first = row0 // block_rows, out_row0 // block_rows
    assert row0 % block_rows == 0 and rows % block_rows == 0 and out_row0 % block_rows == 0
    return _aliased_call(
        _cast_kernel, 1, [] if prev is None else [prev],
        grid=(depth, rows // block_rows),
        in_specs=[pl.BlockSpec((None, block_rows, k), lambda l, j: (l, first + j, 0))],
        out_specs=pl.BlockSpec((None, block_rows, k), lambda l, j: (l, out_first + j, 0)),
        out_shape=jax.ShapeDtypeStruct((depth, out_rows, k), BF16),
        compiler_params=_cparams(2),
        name=name,
    )(w_t, *([] if prev is None else [prev]))


def _mm_nt_kernel(a_ref, b_ref, o_ref):
    o_ref[...] = _nt(a_ref[...], b_ref[...]).astype(o_ref.dtype)


def _matmul_nt(a, w_t, layer, tn, name):
    m, k = a.shape
    n = w_t.shape[1]
    tm = 1024 if m % 1024 == 0 else 512
    return pl.pallas_call(
        _mm_nt_kernel,
        grid=(m // tm, n // tn),
        in_specs=[
            pl.BlockSpec((tm, k), lambda i, j: (i, 0)),
            pl.BlockSpec((None, tn, k), lambda i, j: (layer, j, 0)),
        ],
        out_specs=pl.BlockSpec((tm, tn), lambda i, j: (i, j)),
        out_shape=jax.ShapeDtypeStruct((m, n), F32),
        compiler_params=_cparams(2),
        name=name,
    )(a, w_t)


def _gated(o, gp):
    return (o * _silu(gp)).astype(BF16)


def _pipelined(n_steps, scores, finish, ahead=1):
    queue = [scores(n) for n in range(min(ahead, n_steps))]
    for n in range(n_steps):
        cur = queue.pop(0)
        if n + ahead < n_steps:
            queue.append(scores(n + ahead))
        finish(n, cur)


def _head(h, w=HEAD_DIM):
    return slice(h * w, (h + 1) * w)


def _ctx_a_kernel(q_ref, k_ref, v_ref, gp_ref, o_ref):
    def scores(h):
        q = (q_ref[:, _head(h)] * QSCALE).astype(BF16)
        return [(_nt(q, k_ref[:, _head(h)].astype(BF16)), v_ref[:, _head(h)].astype(BF16))]

    def finish(h, parts):
        o_ref[:, _head(h)] = _gated(_softmax_pv(parts), gp_ref[:, _head(h)])

    _pipelined(8, scores, finish)


def _ctx_b_kernel(q_ref, k_ref, v_ref, sink_ref, gp_ref, o_ref, *, layer):
    def scores(h):
        q = (q_ref[:, _head(h)] * QSCALE).astype(BF16)
        return [(_nt(q, k_ref[:, _head(h // 4)].astype(BF16)), v_ref[:, _head(h // 4)].astype(BF16))]

    def finish(h, parts):
        o = _softmax_pv(parts, extra=sink_ref[layer, h] * LOG2E)
        o_ref[:, _head(h)] = _gated(o, gp_ref[:, _head(h)])

    _pipelined(8, scores, finish)


def _ctx_c_kernel(qn_ref, qp_ref, kv_ref, kpe_ref, gp_ref, o_ref):
    kpe = kpe_ref[...].astype(BF16)

    def scores(h):
        kn = kv_ref[:, _head(2 * h)].astype(BF16)
        q = (jnp.concatenate([qn_ref[:, _head(h)], qp_ref[:, _head(h)]], axis=-1) * QSCALE_MLA).astype(BF16)
        return [(_nt(q, jnp.concatenate([kn, kpe], axis=-1)), kv_ref[:, _head(2 * h + 1)].astype(BF16))]

    def finish(h, parts):
        o_ref[:, _head(h)] = _gated(_softmax_pv(parts), gp_ref[:, _head(h)])

    _pipelined(8, scores, finish)


def _diff_lambda(lam_ref, lam_init):
    lp = lam_ref[...]
    a = jnp.sum(lp[0:1, :] * lp[1:2, :], axis=-1, keepdims=True)
    b = jnp.sum(lp[2:3, :] * lp[3:4, :], axis=-1, keepdims=True)
    return jnp.exp(a) - jnp.exp(b) + lam_init


def _diff_finish(o1, o2, lam, gsub, lam_init):
    return _rms(o1 - lam * o2, gsub) * (1.0 - lam_init)


def _ctx_d_kernel(q_ref, k_ref, v_ref, lam_ref, gsub_ref, gp_ref, o_ref, *, lam_init):
    lam = _diff_lambda(lam_ref, lam_init)
    gsub = gsub_ref[...]

    first_map = {}

    def scores(t):
        q = (q_ref[:, _head(t)] * QSCALE).astype(BF16)
        return [(_nt(q, k_ref[:, _head(t)].astype(BF16)), v_ref[:, _head(t // 2, 2 * HEAD_DIM)].astype(BF16))]

    def finish(t, parts):
        o = _softmax_pv(parts)
        if t % 2 == 0:
            first_map[t // 2] = o
            return
        sl = _head(t // 2, 2 * HEAD_DIM)
        o = _diff_finish(first_map.pop(t // 2), o, lam, gsub, lam_init)
        o_ref[:, sl] = _gated(o, gp_ref[:, sl])

    _pipelined(8, scores, finish)


def _ctx_attention(pa, pb, q_up, kv_up, sink, lam, gsub, layer, lam_init, ctx, n_tok):
    nb = ctx.rows // SEQ
    wide = 8 * HEAD_DIM

    def blk(col, w=wide):
        return pl.BlockSpec((SEQ, w), lambda b: (b, col // w))

    def small(shape):
        return pl.BlockSpec((None,) + shape, lambda b: (layer,) + (0,) * len(shape))

    def body(aq, ak, av, bq, bk, bv, sink_ref, cqn, cqp, ckv, ckpe, dq, dk, dv, lam_ref, gsub_ref,
             gp_a, gp_b, gp_c, gp_d, o_ref):
        def out(mixer):
            return o_ref.at[:, mixer * wide:(mixer + 1) * wide]

        _ctx_a_kernel(aq, ak, av, gp_a, out(0))
        _ctx_b_kernel(bq, bk, bv, sink_ref, gp_b, out(1), layer=layer)
        _ctx_c_kernel(cqn, cqp, ckv, ckpe, gp_c, out(2))
        _ctx_d_kernel(dq, dk, dv, lam_ref, gsub_ref, gp_d, out(3), lam_init=lam_init)

    return pl.pallas_call(
        body, grid=(nb,),
        in_specs=[blk(A_NQ), blk(A_NK), blk(A_NV),
                  blk(A_SWQ), blk(A_SWK, 256), blk(A_SWV, 256), pl.BlockSpec(memory_space=pltpu.SMEM),
                  blk(0), blk(wide), blk(0, 2 * wide), blk(A_KPE, HEAD_DIM),
                  blk(B_DQ), blk(B_DK), blk(B_DV), small((4, HEAD_DIM)), small((1, 2 * HEAD_DIM))]
                 + [blk(B_GP + mixer * wide) for mixer in range(4)],
        out_specs=pl.BlockSpec((SEQ, 4 * wide), lambda b: (b, 0)),
        out_shape=jax.ShapeDtypeStruct((n_tok, D_MODEL), BF16),
        compiler_params=_cparams(1), name="ctx_mixers",
    )(pa, pa, pa, pa, pa, pa, sink, q_up, q_up, kv_up, pa, pb, pb, pb, lam, gsub, pb, pb, pb, pb)


def _nat_win_start(n):
    n_rows = DEC_SEQ // GRID_W
    return min(max(2 * n - NAT_KR // 2, 0), n_rows - NAT_SPAN)


def _nat_pattern_key(n):
    n_rows = DEC_SEQ // GRID_W
    q_per = TQ_WIN // GRID_W
    s = _nat_win_start(n)
    return tuple((q_per * n + j - s, min(max(q_per * n + j - NAT_KR // 2, 0), n_rows - NAT_KR) - s)
                 for j in range(q_per))


NAT_PATTERN_STEPS = tuple(sorted({_nat_pattern_key(n): n for n in reversed(range(DEC_SEQ // TQ_WIN))}.values()))
NAT_PATTERN = tuple([_nat_pattern_key(m) for m in NAT_PATTERN_STEPS].index(_nat_pattern_key(n))
                    for n in range(DEC_SEQ // TQ_WIN))


def _lat_a_kernel(q_ref, k_ref, v_ref, ck_ref, cv_ref, bias_ref, gp_ref, o_ref):
    k = k_ref[...].astype(BF16)
    v = v_ref[...].astype(BF16)
    ck = ck_ref[...].astype(BF16)
    cv = cv_ref[...].astype(BF16)

    def scores(n):
        win = slice(_nat_win_start(n) * GRID_W, (_nat_win_start(n) + NAT_SPAN) * GRID_W)
        q = (q_ref[_head(n, TQ_WIN), :] * QSCALE).astype(BF16)
        return [(_nt(q, ck), cv), (_nt(q, k[win]) + bias_ref[NAT_PATTERN[n]], v[win])]

    def finish(n, parts):
        rows = _head(n, TQ_WIN)
        o_ref[rows, :] = _gated(_softmax_pv(parts), gp_ref[rows, :])

    _pipelined(DEC_SEQ // TQ_WIN, scores, finish, ahead=2)


def _swa_win_start(n):
    return min(max(n - 1, 0), DEC_SEQ // TQ_WIN - 3) * TQ_WIN


def _lat_b_kernel(q_ref, k_ref, v_ref, ck_ref, cv_ref, sink_ref, mask_ref, cos_ref, sl_ref, sh_ref,
                  gp_ref, o_ref, *, layer):
    k = _rope(k_ref[...], cos_ref[...], sl_ref[...], sh_ref[...], 32).astype(BF16)
    v = v_ref[...].astype(BF16)
    ck = ck_ref[...].astype(BF16)
    cv = cv_ref[...].astype(BF16)
    sink = sink_ref[layer, pl.program_id(1)] * LOG2E

    def scores(n):
        rows = _head(n, TQ_WIN)
        w0 = _swa_win_start(n)
        win = slice(w0, w0 + 3 * TQ_WIN)
        q = _rope(q_ref[rows, :], cos_ref[rows, :], sl_ref[rows, :], sh_ref[rows, :], 32)
        q = (q * QSCALE).astype(BF16)
        return [(_nt(q, ck), cv), (_nt(q, k[win]) + mask_ref[n - w0 // TQ_WIN], v[win])]

    def finish(n, parts):
        rows = _head(n, TQ_WIN)
        o_ref[rows, :] = _gated(_softmax_pv(parts, extra=sink), gp_ref[rows, :])

    _pipelined(DEC_SEQ // TQ_WIN, scores, finish, ahead=2)


def _lat_c_kernel(qn_ref, qp_ref, kn_ref, v_ref, kpe_ref, ckn_ref, cv_ref, ckpe_ref,
                  cos_ref, sl_ref, sh_ref, gp_ref, o_ref):
    kpe = _rope(kpe_ref[...], cos_ref[...], sl_ref[...], sh_ref[...], 16)
    keys = jnp.concatenate([
        jnp.concatenate([ckn_ref[...], ckpe_ref[...].astype(BF16)], axis=-1),
        jnp.concatenate([kn_ref[...], kpe.astype(BF16)], axis=-1)], axis=0)
    vals = jnp.concatenate([cv_ref[...], v_ref[...]], axis=0)

    def scores(n):
        rows = _head(n, TQ_DENSE)
        qp = _rope(qp_ref[rows, :], cos_ref[rows, :], sl_ref[rows, :], sh_ref[rows, :], 16)
        q = (jnp.concatenate([qn_ref[rows, :], qp], axis=-1) * QSCALE_MLA).astype(BF16)
        return [(_nt(q, keys), vals)]

    def finish(n, parts):
        rows = _head(n, TQ_DENSE)
        o_ref[rows, :] = _gated(_softmax_pv(parts), gp_ref[rows, :])

    _pipelined(DEC_SEQ // TQ_DENSE, scores, finish, ahead=2)


def _lat_d_kernel(q1_ref, q2_ref, k1_ref, k2_ref, v_ref, ck1_ref, ck2_ref, cv_ref,
                  lam_ref, gsub_ref, cos_ref, sl_ref, sh_ref, gp_ref, o_ref, *, lam_init):
    cos, sl, sh = cos_ref[...], sl_ref[...], sh_ref[...]
    keys1 = jnp.concatenate([ck1_ref[...], _rope(k1_ref[...], cos, sl, sh, 32)], axis=0).astype(BF16)
    keys2 = jnp.concatenate([ck2_ref[...], _rope(k2_ref[...], cos, sl, sh, 32)], axis=0).astype(BF16)
    vals = jnp.concatenate([cv_ref[...], v_ref[...]], axis=0).astype(BF16)
    lam = _diff_lambda(lam_ref, lam_init)
    gsub = gsub_ref[...]
    first_map = {}

    def scores(t):
        rows = _head(t // 2, TQ_DENSE)
        q_ref, keys = ((q1_ref, keys1), (q2_ref, keys2))[t % 2]
        q = _rope(q_ref[rows, :], cos_ref[rows, :], sl_ref[rows, :], sh_ref[rows, :], 32)
        return [(_nt((q * QSCALE).astype(BF16), keys), vals)]

    def finish(t, parts):
        o = _softmax_pv(parts)
        if t % 2 == 0:
            first_map[t // 2] = o
            return
        rows = _head(t // 2, TQ_DENSE)
        o = _diff_finish(first_map.pop(t // 2), o, lam, gsub, lam_init)
        o_ref[rows, :] = _gated(o, gp_ref[rows, :])

    _pipelined(2 * (DEC_SEQ // TQ_DENSE), scores, finish)


def _lat_attention(mixed, pa, pb, q_up, kv_up, caches, layer, nat_bias, swa_mask, sink, lam,
                   gsub, lam_init, rope128, rope64, lat, n_tok):
    nb = lat.rows // DEC_SEQ
    rb0 = lat.row0 // DEC_SEQ
    cb0 = n_tok // PAST_LEN
    c_nat_k, c_nat_v, c_swa_k, c_swa_v, c_kpe, c_dk, c_dv = caches
    hd = HEAD_DIM

    def rows(col_of, w=hd):
        return pl.BlockSpec((DEC_SEQ, w), lambda b, h: (rb0 + b, col_of(h)))

    def cache(col_of, w=hd):
        return pl.BlockSpec((None, None, PAST_LEN, w), lambda b, h: (b, layer, 0, col_of(h)))

    def table():
        return pl.BlockSpec((DEC_SEQ, hd), lambda b, h: (0, 0))

    def small(shape):
        return pl.BlockSpec((None,) + shape, lambda b, h: (layer,) + (0,) * len(shape))

    def call(body, n_in, heads, in_specs, args, mixed, w, col0, name):
        return _aliased_call(
            body, n_in, [mixed], grid=(nb, heads), in_specs=in_specs,
            out_specs=rows(lambda h: col0 // w + h, w),
            out_shape=jax.ShapeDtypeStruct((n_tok, D_MODEL), BF16),
            compiler_params=_cparams(2), name=name,
        )(*args, mixed)

    mixed = call(
        _lat_a_kernel, 7, 8,
        [rows(lambda h: A_NQ // hd + h), rows(lambda h: A_NK // hd + h), rows(lambda h: A_NV // hd + h),
         cache(lambda h: h), cache(lambda h: h),
         pl.BlockSpec((None, None, len(NAT_PATTERN_STEPS), TQ_WIN, NAT_SPAN * GRID_W),
                      lambda b, h: (layer, h, 0, 0, 0)),
         rows(lambda h: B_GP // hd + h)],
        (pa, pa, pa, c_nat_k, c_nat_v, nat_bias, pb), mixed, hd, 0, "lat_nat")
    mixed = call(
        functools.partial(_lat_b_kernel, layer=layer), 11, 8,
        [rows(lambda h: A_SWQ // hd + h), rows(lambda h: A_SWK // hd + h // 4),
         rows(lambda h: A_SWV // hd + h // 4), cache(lambda h: h // 4), cache(lambda h: h // 4),
         pl.BlockSpec(memory_space=pltpu.SMEM),
         pl.BlockSpec((3, TQ_WIN, 3 * TQ_WIN), lambda b, h: (0, 0, 0)), table(), table(), table(),
         rows(lambda h: B_GP // hd + 8 + h)],
        (pa, pa, pa, c_swa_k, c_swa_v, sink, swa_mask, *rope128, pb), mixed, hd, 8 * hd, "lat_swa")
    mixed = call(
        _lat_c_kernel, 12, 8,
        [rows(lambda h: h), rows(lambda h: 8 + h),
         rows(lambda h: 2 * h), rows(lambda h: 2 * h + 1), rows(lambda h: A_KPE // hd),
         pl.BlockSpec((PAST_LEN, hd), lambda b, h: (cb0 + b, 2 * h)),
         pl.BlockSpec((PAST_LEN, hd), lambda b, h: (cb0 + b, 2 * h + 1)),
         cache(lambda h: 0), table(), table(), table(),
         rows(lambda h: B_GP // hd + 16 + h)],
        (q_up, q_up, kv_up, kv_up, pa, kv_up, kv_up, c_kpe, *rope64, pb), mixed, hd, 16 * hd, "lat_mla")
    mixed = call(
        functools.partial(_lat_d_kernel, lam_init=lam_init), 14, 4,
        [rows(lambda h: B_DQ // hd + 2 * h), rows(lambda h: B_DQ // hd + 2 * h + 1),
         rows(lambda h: B_DK // hd + 2 * h), rows(lambda h: B_DK // hd + 2 * h + 1),
         rows(lambda h: B_DV // (2 * hd) + h, 2 * hd),
         cache(lambda h: 2 * h), cache(lambda h: 2 * h + 1), cache(lambda h: h, 2 * hd),
         small((4, hd)), small((1, 2 * hd)), table(), table(), table(),
         rows(lambda h: B_GP // (2 * hd) + 12 + h, 2 * hd)],
        (pb, pb, pb, pb, pb, c_dk, c_dk, c_dv, lam, gsub, *rope128, pb), mixed, 2 * hd, 24 * hd,
        "lat_diff")
    return mixed


def _state_kernel(nk_ref, nv_ref, sk_ref, sv_ref, ckv_ref, kpe_ref, dk_ref, dv_ref,
                  o_nk, o_nv, o_sk, o_sv, o_ckv, o_kpe, o_dk, o_dv):
    for src, dst, heads, w in ((nk_ref, o_nk, 8, HEAD_DIM), (nv_ref, o_nv, 8, HEAD_DIM),
                               (sk_ref, o_sk, 2, HEAD_DIM), (sv_ref, o_sv, 2, HEAD_DIM),
                               (dk_ref, o_dk, 4, 2 * HEAD_DIM), (dv_ref, o_dv, 4, 2 * HEAD_DIM)):
        for h in range(heads):
            dst[:, h, :] = src[:, h * w:(h + 1) * w]
    o_ckv[...] = ckv_ref[...]
    o_kpe[...] = kpe_ref[:, :MLA_ROPE]


def _write_states(pa, pb, ckvn, layer, depth, bc, prev):
    def src(col, w):
        return pl.BlockSpec((SEQ, w), lambda b: (b, col // w))

    def dst(*tail):
        return pl.BlockSpec((None, None, SEQ) + tail, lambda b: (b, layer, 0) + (0,) * len(tail))

    tails = [(8, HEAD_DIM), (8, HEAD_DIM), (2, HEAD_DIM), (2, HEAD_DIM), (MLA_KV_LORA,), (MLA_ROPE,),
             (4, 2 * HEAD_DIM), (4, 2 * HEAD_DIM)]
    return _aliased_call(
        _state_kernel, 8, list(prev), grid=(bc,),
        in_specs=[src(A_NK, 1024), src(A_NV, 1024), src(A_SWK, 256), src(A_SWV, 256),
                  pl.BlockSpec((SEQ, MLA_KV_LORA), lambda b: (b, 0)), src(A_KPE, HEAD_DIM),
                  src(B_DK, 1024), src(B_DV, 1024)],
        out_specs=[dst(*t) for t in tails],
        out_shape=[jax.ShapeDtypeStruct((bc, depth, SEQ) + t, F32) for t in tails],
        compiler_params=_cparams(1), name="write_states",
    )(pa, pa, pa, pa, ckvn, pa, pb, pb, *prev)


def _rope_tables(rot_dim):
    axis_dim = rot_dim // 2
    inv = 1.0 / (ROPE_BASE ** (jnp.arange(0, axis_dim, 2, dtype=F32) / axis_dim))
    t = jnp.arange(DEC_SEQ)
    ang_row = (t // GRID_W).astype(F32)[:, None] * inv
    ang_col = (t % GRID_W).astype(F32)[:, None] * inv
    zeros = jnp.zeros_like(ang_row)
    pad = HEAD_DIM - rot_dim
    cos = jnp.concatenate([jnp.cos(ang_row)] * 2 + [jnp.cos(ang_col)] * 2
                          + [jnp.ones((DEC_SEQ, pad), F32)], axis=-1)
    sin_lo = jnp.concatenate([-jnp.sin(ang_row), zeros, -jnp.sin(ang_col), zeros,
                              jnp.zeros((DEC_SEQ, pad), F32)], axis=-1)
    sin_hi = jnp.concatenate([zeros, jnp.sin(ang_row), zeros, jnp.sin(ang_col),
                              jnp.zeros((DEC_SEQ, pad), F32)], axis=-1)
    return cos, sin_lo, sin_hi


def _nat_bias(rpb):
    n_rows = DEC_SEQ // GRID_W
    steps = len(NAT_PATTERN_STEPS)
    q_per = TQ_WIN // GRID_W
    rep = jnp.asarray(NAT_PATTERN_STEPS)
    qr = rep[:, None] * q_per + jnp.arange(q_per)[None, :]
    kr = jnp.asarray([_nat_win_start(n) for n in NAT_PATTERN_STEPS])[:, None] + jnp.arange(NAT_SPAN)[None, :]
    c = jnp.arange(GRID_W)
    row0 = jnp.clip(qr - NAT_KR // 2, 0, n_rows - NAT_KR)
    col0 = jnp.clip(c - NAT_KC // 2, 0, GRID_W - NAT_KC)
    ok_r = (kr[:, None, :] >= row0[:, :, None]) & (kr[:, None, :] < row0[:, :, None] + NAT_KR)
    ok_c = (c[None, :] >= col0[:, None]) & (c[None, :] < col0[:, None] + NAT_KC)
    dr = jnp.clip(kr[:, None, :] - qr[:, :, None], -(NAT_KR - 1), NAT_KR - 1) + (NAT_KR - 1)
    dc = jnp.clip(c[None, :] - c[:, None], -(NAT_KC - 1), NAT_KC - 1) + (NAT_KC - 1)
    sel_r = jax.nn.one_hot(dr, 2 * NAT_KR - 1, dtype=F32)
    sel_c = jax.nn.one_hot(dc, 2 * NAT_KC - 1, dtype=F32)
    bias = jnp.einsum("nqkr,lhrc,xyc->lhnqxky", sel_r, rpb.astype(F32), sel_c,
                      precision=lax.Precision.HIGHEST)
    ok = ok_r[:, :, None, :, None] & ok_c[None, None, :, None, :]
    bias = jnp.where(ok[None, None], bias * LOG2E, NEG_INF)
    return bias.reshape(rpb.shape[0], rpb.shape[1], steps, TQ_WIN, NAT_SPAN * GRID_W)


def _swa_mask():
    qi = jnp.arange(TQ_WIN)[None, :, None] + TQ_WIN * jnp.arange(3)[:, None, None]
    kj = jnp.arange(3 * TQ_WIN)[None, None, :]
    return jnp.where(jnp.abs(qi - kj) <= SWA_WINDOW, 0.0, NEG_INF).astype(F32)


def _pack_w_uq(w_uq):
    depth = w_uq.shape[0]
    w = w_uq.reshape(depth, MLA_Q_LORA, 8, MLA_NOPE + MLA_ROPE)
    nope = w[..., :MLA_NOPE].reshape(depth, MLA_Q_LORA, 8 * MLA_NOPE)
    pe = jnp.pad(w[..., MLA_NOPE:], ((0, 0), (0, 0), (0, 0), (0, HEAD_DIM - MLA_ROPE)))
    return jnp.concatenate([nope, pe.reshape(depth, MLA_Q_LORA, 8 * HEAD_DIM)], axis=-1).astype(BF16)


def kernel(x_prompt, x_sample, cache_nat_k, cache_nat_v, cache_swa_k, cache_swa_v, cache_mla_ckv, cache_mla_kpe, cache_diff_k, cache_diff_v, c, c_ctx, w_mod, b_mod, g_pre, g_post, w_in, w_out, nat_rpb, swa_sink, mla_g_q, mla_g_kv, mla_w_uq, mla_w_ukv, diff_lambda, diff_g_subln):
    depth = w_mod.shape[0]
    bc, bl = x_prompt.shape[0], x_sample.shape[0]
    n_ctx, n_lat = bc * SEQ, bl * DEC_SEQ
    n_tok = n_ctx + n_lat
    assert n_ctx % DEC_SEQ == 0 and x_prompt.shape[1] == SEQ and x_sample.shape[1] == DEC_SEQ
    ctx = Group(0, n_ctx, lambda i: 0)
    lat = Group(n_ctx, n_lat, lambda i: 1 + i // (DEC_SEQ // ROW_BLOCK))

    n_cond = -(-(1 + bl) // 8) * 8
    cond = jnp.concatenate([c_ctx[None], c, jnp.zeros((n_cond - 1 - bl, D_MODEL), F32)], axis=0)
    mod = _modulation(cond, w_mod, b_mod).reshape(depth, n_cond, 3, D_MODEL)

    w_t = jnp.swapaxes(w_in, 1, 2)
    w_p = _cast_rows(w_t, B_START, N_B, 448, 0, N_PROJ, None, "pack_w_b")
    w_p = _cast_rows(w_t, 0, N_A, 256, N_B, N_PROJ, w_p, "pack_w_a")
    w_out_b = w_out.astype(BF16)
    w_uq_p = _pack_w_uq(mla_w_uq)
    w_ukv_b = mla_w_ukv.astype(BF16)
    g_pre3 = g_pre.reshape(depth, 1, D_MODEL)
    g_post3 = g_post.reshape(depth, 1, D_MODEL)
    g_q3 = mla_g_q.reshape(depth, 1, MLA_Q_LORA)
    g_kv3 = mla_g_kv.reshape(depth, 1, MLA_KV_LORA)
    gsub3 = diff_g_subln.reshape(depth, 1, 2 * HEAD_DIM)
    rope128 = _rope_tables(HEAD_DIM)
    rope64 = _rope_tables(MLA_ROPE)
    nat_bias = _nat_bias(nat_rpb)
    swa_mask = _swa_mask()

    caches = (cache_nat_k.reshape(bl, depth, PAST_LEN, -1), cache_nat_v.reshape(bl, depth, PAST_LEN, -1),
              cache_swa_k.reshape(bl, depth, PAST_LEN, -1), cache_swa_v.reshape(bl, depth, PAST_LEN, -1),
              jnp.pad(cache_mla_kpe, ((0, 0), (0, 0), (0, 0), (0, HEAD_DIM - MLA_ROPE))),
              cache_diff_k.reshape(bl, depth, PAST_LEN, -1), cache_diff_v.reshape(bl, depth, PAST_LEN, -1))

    x_c = x_prompt.reshape(n_ctx, D_MODEL)
    x_l = x_sample.reshape(n_lat, D_MODEL)
    h = _norm_mod(x_c, g_pre3, mod, 0, ctx, n_tok, None)
    h = _norm_mod(x_l, g_pre3, mod, 0, lat, n_tok, h)
    states = ()
    for l in range(depth):
        lam_init = 0.8 - 0.6 * math.exp(-0.3 * l)
        p = _matmul_nt(h, w_p, l, 768, "in_proj")
        pa = pb = p
        ckvn = _rms_cols(p, A_CKV, 128, MLA_KV_LORA, g_kv3, l, F32)
        q_up = _mla_q_up(p, g_q3, w_uq_p, l)
        ckv_all = jnp.concatenate([ckvn, cache_mla_ckv[:, l].reshape(bl * PAST_LEN, MLA_KV_LORA)], axis=0)
        kv_up = _matmul(ckv_all.astype(BF16), w_ukv_b, l, 1024, "mla_kv_up", out_dtype=BF16)
        states = _write_states(pa, pb, ckvn, l, depth, bc, states)

        mixed = _ctx_attention(pa, pb, q_up, kv_up, swa_sink, diff_lambda, gsub3, l, lam_init, ctx, n_tok)
        mixed = _lat_attention(mixed, pa, pb, q_up, kv_up, caches, l, nat_bias, swa_mask, swa_sink,
                               diff_lambda, gsub3, lam_init, rope128, rope64, lat, n_tok)
        o = _matmul(mixed, w_out_b, l, 1024, "out_proj", out_dtype=BF16)
        if l + 1 < depth:
            x_c, h = _post_norm(x_c, o, g_post3, g_pre3, mod, l, ctx, n_tok, None)
            x_l, h = _post_norm(x_l, o, g_post3, g_pre3, mod, l, lat, n_tok, h)
        else:
            x_c = _post(x_c, o, g_post3, mod, l, ctx)
            x_l = _post(x_l, o, g_post3, mod, l, lat)

    return (x_c.reshape(bc, SEQ, D_MODEL), x_l.reshape(bl, DEC_SEQ, D_MODEL)) + tuple(states)
```

```python
import functools
import math
from typing import Callable, NamedTuple

import jax
import jax.numpy as jnp
from jax import lax
from jax.experimental import pallas as pl
from jax.experimental.pallas import tpu as pltpu

F32 = jnp.float32
BF16 = jnp.bfloat16

D_MODEL = 4096
HEAD_DIM = 128
GRID_W = 64
SEQ = 256
DEC_SEQ = 1024
PAST_LEN = 512
NAT_KR, NAT_KC = 8, 16
NAT_HEADS = 8
SWA_WINDOW = 128
MLA_NOPE, MLA_ROPE, MLA_V = 128, 64, 128
MLA_Q_LORA, MLA_KV_LORA = 896, 256
ROPE_BASE = 10000.0
EPS = 1e-6
NEG_INF = -1e30
LOG2E = 1.4426950408889634
QSCALE = HEAD_DIM ** -0.5 * LOG2E
QSCALE_MLA = (MLA_NOPE + MLA_ROPE) ** -0.5 * LOG2E

B_START = 5824
N_B = 7168
N_A = 5888
N_PROJ = N_B + N_A
B_DQ, B_DK, B_DV, B_GP = 0, 1024, 2048, 3072
A_NQ, A_NK, A_NV = N_B + 0, N_B + 1024, N_B + 2048
A_SWQ, A_SWK, A_SWV = N_B + 3072, N_B + 4096, N_B + 4352
A_CQ, A_CKV, A_KPE = N_B + 4608, N_B + 5504, N_B + 5760

ROW_BLOCK = 256
TQ_WIN = 128
TQ_DENSE = 256
NAT_SPAN = 10
VMEM_LIMIT = 56 * 1024 * 1024


class Group(NamedTuple):
    row0: int
    rows: int
    mod_row: Callable


def _cparams(n_axes):
    return pltpu.CompilerParams(dimension_semantics=("arbitrary",) * n_axes,
                                vmem_limit_bytes=VMEM_LIMIT)


def _silu(x):
    return x * (1.0 / (1.0 + jnp.exp(-x)))


def _nt(a, b):
    return lax.dot_general(a, b, (((1,), (1,)), ((), ())), preferred_element_type=F32)


def _softmax_pv(parts, extra=None):
    m = parts[0][0].max(axis=-1, keepdims=True)
    for s, _ in parts[1:]:
        m = jnp.maximum(m, s.max(axis=-1, keepdims=True))
    if extra is not None:
        m = jnp.maximum(m, extra)
    den = None
    acc = None
    for s, v in parts:
        p = jnp.exp2(s - m)
        ps = p.sum(axis=-1, keepdims=True)
        pv = jnp.dot(p.astype(BF16), v, preferred_element_type=F32)
        den = ps if den is None else den + ps
        acc = pv if acc is None else acc + pv
    if extra is not None:
        den = den + jnp.exp2(extra - m)
    return acc / den


def _rope(x, cos, sin_lo, sin_hi, half):
    lanes = x.shape[-1]
    return (x * cos + pltpu.roll(x, lanes - half, 1) * sin_lo + pltpu.roll(x, half, 1) * sin_hi)


def _rms(x, g):
    return x * lax.rsqrt(jnp.mean(x * x, axis=-1, keepdims=True) + EPS) * g


def _aliased_call(body, n_in, alias_args, **kw):
    n_alias = len(alias_args)
    n_out = len(kw["out_shape"]) if isinstance(kw["out_shape"], (tuple, list)) else 1
    if n_alias == 0:
        return pl.pallas_call(body, **kw)

    def with_aliases(*refs):
        body(*refs[:n_in], *refs[n_in + n_alias:])

    kw["in_specs"] = list(kw["in_specs"]) + [pl.BlockSpec(memory_space=pl.ANY)] * n_alias
    kw["input_output_aliases"] = {n_in + k: n_out - n_alias + k for k in range(n_alias)}
    return pl.pallas_call(with_aliases, **kw)


def _mod_kernel(c_ref, w_ref, b_ref, o_ref):
    s = _silu(c_ref[...]).astype(BF16)
    o_ref[...] = jnp.dot(s, w_ref[...].astype(BF16), preferred_element_type=F32) + b_ref[...]


def _modulation(cond, w_mod, b_mod):
    depth = w_mod.shape[0]
    rows = cond.shape[0]
    tn = 512
    n = 3 * D_MODEL
    return pl.pallas_call(
        _mod_kernel,
        grid=(depth, n // tn),
        in_specs=[
            pl.BlockSpec((rows, D_MODEL), lambda l, j: (0, 0)),
            pl.BlockSpec((None, D_MODEL, tn), lambda l, j: (l, 0, j)),
            pl.BlockSpec((None, 1, tn), lambda l, j: (l, 0, j)),
        ],
        out_specs=pl.BlockSpec((None, rows, tn), lambda l, j: (l, 0, j)),
        out_shape=jax.ShapeDtypeStruct((depth, rows, n), F32),
        compiler_params=_cparams(2),
        name="modulation",
    )(cond, w_mod, b_mod.reshape(depth, 1, n))


def _mod_spec(grp, layer):
    return pl.BlockSpec((None, None, 3, D_MODEL), lambda i: (layer, grp.mod_row(i), 0, 0))


def _vec_spec(layer):
    return pl.BlockSpec((None, 1, D_MODEL), lambda i: (layer, 0, 0))


def _tok_spec(grp):
    return pl.BlockSpec((ROW_BLOCK, D_MODEL), lambda i: (grp.row0 // ROW_BLOCK + i, 0))


def _own_spec():
    return pl.BlockSpec((ROW_BLOCK, D_MODEL), lambda i: (i, 0))


def _norm_mod_kernel(x_ref, g_ref, m_ref, h_ref):
    y = _rms(x_ref[...], g_ref[...])
    h_ref[...] = (y * (1.0 + m_ref[1:2, :]) + m_ref[0:1, :]).astype(BF16)


def _norm_mod(x, g_pre, mod, layer, grp, n_tok, h_prev):
    return _aliased_call(
        _norm_mod_kernel, 3, [] if h_prev is None else [h_prev],
        grid=(grp.rows // ROW_BLOCK,),
        in_specs=[_own_spec(), _vec_spec(layer), _mod_spec(grp, layer)],
        out_specs=_tok_spec(grp),
        out_shape=jax.ShapeDtypeStruct((n_tok, D_MODEL), BF16),
        compiler_params=_cparams(1),
        name="norm_mod",
    )(x, g_pre, mod, *([] if h_prev is None else [h_prev]))


def _post_kernel(x_ref, o_ref, g_ref, m_ref, y_ref):
    y_ref[...] = x_ref[...] + m_ref[2:3, :] * _rms(o_ref[...].astype(F32), g_ref[...])


def _post(x, o, g_post, mod, layer, grp):
    return pl.pallas_call(
        _post_kernel,
        grid=(grp.rows // ROW_BLOCK,),
        in_specs=[_own_spec(), _tok_spec(grp), _vec_spec(layer), _mod_spec(grp, layer)],
        out_specs=_own_spec(),
        out_shape=jax.ShapeDtypeStruct((grp.rows, D_MODEL), F32),
        compiler_params=_cparams(1),
        name="post_residual",
    )(x, o, g_post, mod)


def _post_norm_kernel(x_ref, o_ref, g_ref, m_ref, gn_ref, mn_ref, y_ref, h_ref):
    y = x_ref[...] + m_ref[2:3, :] * _rms(o_ref[...].astype(F32), g_ref[...])
    y_ref[...] = y
    h_ref[...] = (_rms(y, gn_ref[...]) * (1.0 + mn_ref[1:2, :]) + mn_ref[0:1, :]).astype(BF16)


def _post_norm(x, o, g_post, g_pre, mod, layer, grp, n_tok, h_prev):
    return _aliased_call(
        _post_norm_kernel, 6, [] if h_prev is None else [h_prev],
        grid=(grp.rows // ROW_BLOCK,),
        in_specs=[_own_spec(), _tok_spec(grp), _vec_spec(layer), _mod_spec(grp, layer),
                  _vec_spec(layer + 1), _mod_spec(grp, layer + 1)],
        out_specs=[_own_spec(), _tok_spec(grp)],
        out_shape=[jax.ShapeDtypeStruct((grp.rows, D_MODEL), F32),
                   jax.ShapeDtypeStruct((n_tok, D_MODEL), BF16)],
        compiler_params=_cparams(1),
        name="post_norm",
    )(x, o, g_post, mod, g_pre, mod, *([] if h_prev is None else [h_prev]))


def _rms_cols_kernel(a_ref, b_ref, g_ref, y_ref, *, wa, wb):
    x = jnp.concatenate([a_ref[:, :wa], b_ref[:, :wb]], axis=-1)
    y_ref[...] = _rms(x, g_ref[...]).astype(y_ref.dtype)


def _rms_cols(p, col, block_w, width, g, layer, out_dtype):
    m = p.shape[0]
    tm = 1024
    first = col // block_w
    return pl.pallas_call(
        functools.partial(_rms_cols_kernel, wa=block_w, wb=width - block_w),
        grid=(m // tm,),
        in_specs=[
            pl.BlockSpec((tm, block_w), lambda i: (i, first)),
            pl.BlockSpec((tm, block_w), lambda i: (i, first + 1)),
            pl.BlockSpec((None, 1, width), lambda i: (layer, 0, 0)),
        ],
        out_specs=pl.BlockSpec((tm, width), lambda i: (i, 0)),
        out_shape=jax.ShapeDtypeStruct((m, width), out_dtype),
        compiler_params=_cparams(1),
        name="rms_cols",
    )(p, p, g)


def _q_up_kernel(a_ref, b_ref, g_ref, w_ref, o_ref):
    cq = jnp.concatenate([a_ref[...], b_ref[:, :MLA_Q_LORA - 512]], axis=-1)
    o_ref[...] = jnp.dot(_rms(cq, g_ref[...]).astype(BF16), w_ref[...], preferred_element_type=F32)


def _mla_q_up(p, g, w_uq, layer):
    m = p.shape[0]
    tm = 512
    n = w_uq.shape[-1]
    first = A_CQ // 512
    return pl.pallas_call(
        _q_up_kernel,
        grid=(m // tm,),
        in_specs=[
            pl.BlockSpec((tm, 512), lambda i: (i, first)),
            pl.BlockSpec((tm, 512), lambda i: (i, first + 1)),
            pl.BlockSpec((None, 1, MLA_Q_LORA), lambda i: (layer, 0, 0)),
            pl.BlockSpec((None, MLA_Q_LORA, n), lambda i: (layer, 0, 0)),
        ],
        out_specs=pl.BlockSpec((tm, n), lambda i: (i, 0)),
        out_shape=jax.ShapeDtypeStruct((m, n), F32),
        compiler_params=_cparams(1),
        name="mla_q_up",
    )(p, p, g, w_uq)


def _mm_kernel(a_ref, b_ref, o_ref):
    o_ref[...] = jnp.dot(a_ref[...], b_ref[...], preferred_element_type=F32).astype(o_ref.dtype)


def _matmul(a, w, layer, tn, name, n=None, out_dtype=F32):
    m, k = a.shape
    n = w.shape[-1] if n is None else n
    tm = 1024 if m % 1024 == 0 else 512
    return pl.pallas_call(
        _mm_kernel,
        grid=(m // tm, n // tn),
        in_specs=[
            pl.BlockSpec((tm, k), lambda i, j: (i, 0)),
            pl.BlockSpec((None, k, tn), lambda i, j: (layer, 0, j)),
        ],
        out_specs=pl.BlockSpec((tm, tn), lambda i, j: (i, j)),
        out_shape=jax.ShapeDtypeStruct((m, n), out_dtype),
        compiler_params=_cparams(2),
        name=name,
    )(a, w)


def _cast_kernel(w_ref, o_ref):
    o_ref[...] = w_ref[...].astype(BF16)


def _cast_rows(w_t, row0, rows, block_rows, out_row0, out_rows, prev, name):
    depth, _, k = w_t.shape
    first, out_first = row0 // block_rows, out_row0 // block_rows
    assert row0 % block_rows == 0 and rows % block_rows == 0 and out_row0 % block_rows == 0
    return _aliased_call(
        _cast_kernel, 1, [] if prev is None else [prev],
        grid=(depth, rows // block_rows),
        in_specs=[pl.BlockSpec((None, block_rows, k), lambda l, j: (l, first + j, 0))],
        out_specs=pl.BlockSpec((None, block_rows, k), lambda l, j: (l, out_first + j, 0)),
        out_shape=jax.ShapeDtypeStruct((depth, out_rows, k), BF16),
        compiler_params=_cparams(2),
        name=name,
    )(w_t, *([] if prev is None else [prev]))


def _mm_nt_kernel(a_ref, b_ref, o_ref):
    o_ref[...] = _nt(a_ref[...], b_ref[...]).astype(o_ref.dtype)


def _matmul_nt(a, w_t, layer, tn, name):
    m, k = a.shape
    n = w_t.shape[1]
    tm = 1024 if m % 1024 == 0 else 512
    return pl.pallas_call(
        _mm_nt_kernel,
        grid=(m // tm, n // tn),
        in_specs=[
            pl.BlockSpec((tm, k), lambda i, j: (i, 0)),
            pl.BlockSpec((None, tn, k), lambda i, j: (layer, j, 0)),
        ],
        out_specs=pl.BlockSpec((tm, tn), lambda i, j: (i, j)),
        out_shape=jax.ShapeDtypeStruct((m, n), F32),
        compiler_params=_cparams(2),
        name=name,
    )(a, w_t)


def _gated(o, gp):
    return (o * _silu(gp)).astype(BF16)


def _pipelined(n_steps, scores, finish, ahead=1):
    queue = [scores(n) for n in range(min(ahead, n_steps))]
    for n in range(n_steps):
        cur = queue.pop(0)
        if n + ahead < n_steps:
            queue.append(scores(n + ahead))
        finish(n, cur)


def _head(h, w=HEAD_DIM):
    return slice(h * w, (h + 1) * w)


def _ctx_a_kernel(q_ref, k_ref, v_ref, gp_ref, o_ref):
    def scores(h):
        q = (q_ref[:, _head(h)] * QSCALE).astype(BF16)
        return [(_nt(q, k_ref[:, _head(h)].astype(BF16)), v_ref[:, _head(h)].astype(BF16))]

    def finish(h, parts):
        o_ref[:, _head(h)] = _gated(_softmax_pv(parts), gp_ref[:, _head(h)])

    _pipelined(8, scores, finish)


def _ctx_b_kernel(q_ref, k_ref, v_ref, sink_ref, gp_ref, o_ref, *, layer):
    def scores(h):
        q = (q_ref[:, _head(h)] * QSCALE).astype(BF16)
        return [(_nt(q, k_ref[:, _head(h // 4)].astype(BF16)), v_ref[:, _head(h // 4)].astype(BF16))]

    def finish(h, parts):
        o = _softmax_pv(parts, extra=sink_ref[layer, h] * LOG2E)
        o_ref[:, _head(h)] = _gated(o, gp_ref[:, _head(h)])

    _pipelined(8, scores, finish)


def _ctx_c_kernel(qn_ref, qp_ref, kv_ref, kpe_ref, gp_ref, o_ref):
    kpe = kpe_ref[...].astype(BF16)

    def scores(h):
        kn = kv_ref[:, _head(2 * h)].astype(BF16)
        q = (jnp.concatenate([qn_ref[:, _head(h)], qp_ref[:, _head(h)]], axis=-1) * QSCALE_MLA).astype(BF16)
        return [(_nt(q, jnp.concatenate([kn, kpe], axis=-1)), kv_ref[:, _head(2 * h + 1)].astype(BF16))]

    def finish(h, parts):
        o_ref[:, _head(h)] = _gated(_softmax_pv(parts), gp_ref[:, _head(h)])

    _pipelined(8, scores, finish)


def _diff_lambda(lam_ref, lam_init):
    lp = lam_ref[...]
    a = jnp.sum(lp[0:1, :] * lp[1:2, :], axis=-1, keepdims=True)
    b = jnp.sum(lp[2:3, :] * lp[3:4, :], axis=-1, keepdims=True)
    return jnp.exp(a) - jnp.exp(b) + lam_init


def _diff_finish(o1, o2, lam, gsub, lam_init):
    return _rms(o1 - lam * o2, gsub) * (1.0 - lam_init)


def _ctx_d_kernel(q_ref, k_ref, v_ref, lam_ref, gsub_ref, gp_ref, o_ref, *, lam_init):
    lam = _diff_lambda(lam_ref, lam_init)
    gsub = gsub_ref[...]

    first_map = {}

    def scores(t):
        q = (q_ref[:, _head(t)] * QSCALE).astype(BF16)
        return [(_nt(q, k_ref[:, _head(t)].astype(BF16)), v_ref[:, _head(t // 2, 2 * HEAD_DIM)].astype(BF16))]

    def finish(t, parts):
        o = _softmax_pv(parts)
        if t % 2 == 0:
            first_map[t // 2] = o
            return
        sl = _head(t // 2, 2 * HEAD_DIM)
        o = _diff_finish(first_map.pop(t // 2), o, lam, gsub, lam_init)
        o_ref[:, sl] = _gated(o, gp_ref[:, sl])

    _pipelined(8, scores, finish)


def _ctx_attention(pa, pb, q_up, kv_up, sink, lam, gsub, layer, lam_init, ctx, n_tok):
    nb = ctx.rows // SEQ
    wide = 8 * HEAD_DIM

    def blk(col, w=wide):
        return pl.BlockSpec((SEQ, w), lambda b: (b, col // w))

    def small(shape):
        return pl.BlockSpec((None,) + shape, lambda b: (layer,) + (0,) * len(shape))

    def body(aq, ak, av, bq, bk, bv, sink_ref, cqn, cqp, ckv, ckpe, dq, dk, dv, lam_ref, gsub_ref,
             gp_a, gp_b, gp_c, gp_d, o_ref):
        def out(mixer):
            return o_ref.at[:, mixer * wide:(mixer + 1) * wide]

        _ctx_a_kernel(aq, ak, av, gp_a, out(0))
        _ctx_b_kernel(bq, bk, bv, sink_ref, gp_b, out(1), layer=layer)
        _ctx_c_kernel(cqn, cqp, ckv, ckpe, gp_c, out(2))
        _ctx_d_kernel(dq, dk, dv, lam_ref, gsub_ref, gp_d, out(3), lam_init=lam_init)

    return pl.pallas_call(
        body, grid=(nb,),
        in_specs=[blk(A_NQ), blk(A_NK), blk(A_NV),
                  blk(A_SWQ), blk(A_SWK, 256), blk(A_SWV, 256), pl.BlockSpec(memory_space=pltpu.SMEM),
                  blk(0), blk(wide), blk(0, 2 * wide), blk(A_KPE, HEAD_DIM),
                  blk(B_DQ), blk(B_DK), blk(B_DV), small((4, HEAD_DIM)), small((1, 2 * HEAD_DIM))]
                 + [blk(B_GP + mixer * wide) for mixer in range(4)],
        out_specs=pl.BlockSpec((SEQ, 4 * wide), lambda b: (b, 0)),
        out_shape=jax.ShapeDtypeStruct((n_tok, D_MODEL), BF16),
        compiler_params=_cparams(1), name="ctx_mixers",
    )(pa, pa, pa, pa, pa, pa, sink, q_up, q_up, kv_up, pa, pb, pb, pb, lam, gsub, pb, pb, pb, pb)


def _nat_win_start(n):
    n_rows = DEC_SEQ // GRID_W
    return min(max(2 * n - NAT_KR // 2, 0), n_rows - NAT_SPAN)


def _nat_pattern_key(n):
    n_rows = DEC_SEQ // GRID_W
    q_per = TQ_WIN // GRID_W
    s = _nat_win_start(n)
    return tuple((q_per * n + j - s, min(max(q_per * n + j - NAT_KR // 2, 0), n_rows - NAT_KR) - s)
                 for j in range(q_per))


NAT_PATTERN_STEPS = tuple(sorted({_nat_pattern_key(n): n for n in reversed(range(DEC_SEQ // TQ_WIN))}.values()))
NAT_PATTERN = tuple([_nat_pattern_key(m) for m in NAT_PATTERN_STEPS].index(_nat_pattern_key(n))
                    for n in range(DEC_SEQ // TQ_WIN))


def _lat_a_kernel(q_ref, k_ref, v_ref, ck_ref, cv_ref, bias_ref, gp_ref, o_ref):
    head_rows = pl.ds(pl.program_id(1), PAST_LEN, stride=NAT_HEADS)
    k = k_ref[...].astype(BF16)
    v = v_ref[...].astype(BF16)
    ck = ck_ref[head_rows, :].astype(BF16)
    cv = cv_ref[head_rows, :].astype(BF16)

    def scores(n):
        win = slice(_nat_win_start(n) * GRID_W, (_nat_win_start(n) + NAT_SPAN) * GRID_W)
        q = (q_ref[_head(n, TQ_WIN), :] * QSCALE).astype(BF16)
        return [(_nt(q, ck), cv), (_nt(q, k[win]) + bias_ref[NAT_PATTERN[n]], v[win])]

    def finish(n, parts):
        rows = _head(n, TQ_WIN)
        o_ref[rows, :] = _gated(_softmax_pv(parts), gp_ref[rows, :])

    _pipelined(DEC_SEQ // TQ_WIN, scores, finish, ahead=2)


def _swa_win_start(n):
    return min(max(n - 1, 0), DEC_SEQ // TQ_WIN - 3) * TQ_WIN


def _lat_b_kernel(q_ref, k_ref, v_ref, ck_ref, cv_ref, sink_ref, mask_ref, cos_ref, sl_ref, sh_ref,
                  gp_ref, o_ref, *, layer):
    k = _rope(k_ref[...], cos_ref[...], sl_ref[...], sh_ref[...], 32).astype(BF16)
    v = v_ref[...].astype(BF16)
    ck = ck_ref[...].astype(BF16)
    cv = cv_ref[...].astype(BF16)
    sink = sink_ref[layer, pl.program_id(1)] * LOG2E

    def scores(n):
        rows = _head(n, TQ_WIN)
        w0 = _swa_win_start(n)
        win = slice(w0, w0 + 3 * TQ_WIN)
        q = _rope(q_ref[rows, :], cos_ref[rows, :], sl_ref[rows, :], sh_ref[rows, :], 32)
        q = (q * QSCALE).astype(BF16)
        return [(_nt(q, ck), cv), (_nt(q, k[win]) + mask_ref[n - w0 // TQ_WIN], v[win])]

    def finish(n, parts):
        rows = _head(n, TQ_WIN)
        o_ref[rows, :] = _gated(_softmax_pv(parts, extra=sink), gp_ref[rows, :])

    _pipelined(DEC_SEQ // TQ_WIN, scores, finish, ahead=2)


def _lat_c_kernel(qn_ref, qp_ref, kn_ref, v_ref, kpe_ref, ckn_ref, cv_ref, ckpe_ref,
                  cos_ref, sl_ref, sh_ref, gp_ref, o_ref):
    kpe = _rope(kpe_ref[...], cos_ref[...], sl_ref[...], sh_ref[...], 16)
    keys = jnp.concatenate([
        jnp.concatenate([ckn_ref[...], ckpe_ref[...].astype(BF16)], axis=-1),
        jnp.concatenate([kn_ref[...], kpe.astype(BF16)], axis=-1)], axis=0)
    vals = jnp.concatenate([cv_ref[...], v_ref[...]], axis=0)

    def scores(n):
        rows = _head(n, TQ_DENSE)
        qp = _rope(qp_ref[rows, :], cos_ref[rows, :], sl_ref[rows, :], sh_ref[rows, :], 16)
        q = (jnp.concatenate([qn_ref[rows, :], qp], axis=-1) * QSCALE_MLA).astype(BF16)
        return [(_nt(q, keys), vals)]

    def finish(n, parts):
        rows = _head(n, TQ_DENSE)
        o_ref[rows, :] = _gated(_softmax_pv(parts), gp_ref[rows, :])

    _pipelined(DEC_SEQ // TQ_DENSE, scores, finish, ahead=2)


def _lat_d_kernel(q1_ref, q2_ref, k1_ref, k2_ref, v_ref, ck1_ref, ck2_ref, cv_ref,
                  lam_ref, gsub_ref, cos_ref, sl_ref, sh_ref, gp_ref, o_ref, *, lam_init):
    cos, sl, sh = cos_ref[...], sl_ref[...], sh_ref[...]
    keys1 = jnp.concatenate([ck1_ref[...], _rope(k1_ref[...], cos, sl, sh, 32)], axis=0).astype(BF16)
    keys2 = jnp.concatenate([ck2_ref[...], _rope(k2_ref[...], cos, sl, sh, 32)], axis=0).astype(BF16)
    vals = jnp.concatenate([cv_ref[...], v_ref[...]], axis=0).astype(BF16)
    lam = _diff_lambda(lam_ref, lam_init)
    gsub = gsub_ref[...]
    first_map = {}

    def scores(t):
        rows = _head(t // 2, TQ_DENSE)
        q_ref, keys = ((q1_ref, keys1), (q2_ref, keys2))[t % 2]
        q = _rope(q_ref[rows, :], cos_ref[rows, :], sl_ref[rows, :], sh_ref[rows, :], 32)
        return [(_nt((q * QSCALE).astype(BF16), keys), vals)]

    def finish(t, parts):
        o = _softmax_pv(parts)
        if t % 2 == 0:
            first_map[t // 2] = o
            return
        rows = _head(t // 2, TQ_DENSE)
        o = _diff_finish(first_map.pop(t // 2), o, lam, gsub, lam_init)
        o_ref[rows, :] = _gated(o, gp_ref[rows, :])

    _pipelined(2 * (DEC_SEQ // TQ_DENSE), scores, finish)


def _lat_attention(mixed, pa, pb, q_up, kv_up, caches, layer, nat_bias, swa_mask, sink, lam,
                   gsub, lam_init, rope128, rope64, lat, n_tok):
    nb = lat.rows // DEC_SEQ
    rb0 = lat.row0 // DEC_SEQ
    cb0 = n_tok // PAST_LEN
    c_nat_k, c_nat_v, c_swa_k, c_swa_v, c_kpe, c_dk, c_dv = caches
    hd = HEAD_DIM

    def rows(col_of, w=hd):
        return pl.BlockSpec((DEC_SEQ, w), lambda b, h: (rb0 + b, col_of(h)))

    def cache(col_of, w=hd):
        return pl.BlockSpec((None, None, PAST_LEN, w), lambda b, h: (b, layer, 0, col_of(h)))

    def table():
        return pl.BlockSpec((DEC_SEQ, hd), lambda b, h: (0, 0))

    def small(shape):
        return pl.BlockSpec((None,) + shape, lambda b, h: (layer,) + (0,) * len(shape))

    all_heads = pl.BlockSpec((None, None, PAST_LEN * NAT_HEADS, hd), lambda b, h: (b, layer, 0, 0))

    def call(body, n_in, heads, in_specs, args, mixed, w, col0, name):
        return _aliased_call(
            body, n_in, [mixed], grid=(nb, heads), in_specs=in_specs,
            out_specs=rows(lambda h: col0 // w + h, w),
            out_shape=jax.ShapeDtypeStruct((n_tok, D_MODEL), BF16),
            compiler_params=_cparams(2), name=name,
        )(*args, mixed)

    mixed = call(
        _lat_a_kernel, 7, 8,
        [rows(lambda h: A_NQ // hd + h), rows(lambda h: A_NK // hd + h), rows(lambda h: A_NV // hd + h),
         all_heads, all_heads,
         pl.BlockSpec((None, None, len(NAT_PATTERN_STEPS), TQ_WIN, NAT_SPAN * GRID_W),
                      lambda b, h: (layer, h, 0, 0, 0)),
         rows(lambda h: B_GP // hd + h)],
        (pa, pa, pa, c_nat_k, c_nat_v, nat_bias, pb), mixed, hd, 0, "lat_nat")
    mixed = call(
        functools.partial(_lat_b_kernel, layer=layer), 11, 8,
        [rows(lambda h: A_SWQ // hd + h), rows(lambda h: A_SWK // hd + h // 4),
         rows(lambda h: A_SWV // hd + h // 4), cache(lambda h: h // 4), cache(lambda h: h // 4),
         pl.BlockSpec(memory_space=pltpu.SMEM),
         pl.BlockSpec((3, TQ_WIN, 3 * TQ_WIN), lambda b, h: (0, 0, 0)), table(), table(), table(),
         rows(lambda h: B_GP // hd + 8 + h)],
        (pa, pa, pa, c_swa_k, c_swa_v, sink, swa_mask, *rope128, pb), mixed, hd, 8 * hd, "lat_swa")
    mixed = call(
        _lat_c_kernel, 12, 8,
        [rows(lambda h: h), rows(lambda h: 8 + h),
         rows(lambda h: 2 * h), rows(lambda h: 2 * h + 1), rows(lambda h: A_KPE // hd),
         pl.BlockSpec((PAST_LEN, hd), lambda b, h: (cb0 + b, 2 * h)),
         pl.BlockSpec((PAST_LEN, hd), lambda b, h: (cb0 + b, 2 * h + 1)),
         cache(lambda h: 0), table(), table(), table(),
         rows(lambda h: B_GP // hd + 16 + h)],
        (q_up, q_up, kv_up, kv_up, pa, kv_up, kv_up, c_kpe, *rope64, pb), mixed, hd, 16 * hd, "lat_mla")
    mixed = call(
        functools.partial(_lat_d_kernel, lam_init=lam_init), 14, 4,
        [rows(lambda h: B_DQ // hd + 2 * h), rows(lambda h: B_DQ // hd + 2 * h + 1),
         rows(lambda h: B_DK // hd + 2 * h), rows(lambda h: B_DK // hd + 2 * h + 1),
         rows(lambda h: B_DV // (2 * hd) + h, 2 * hd),
         cache(lambda h: 2 * h), cache(lambda h: 2 * h + 1), cache(lambda h: h, 2 * hd),
         small((4, hd)), small((1, 2 * hd)), table(), table(), table(),
         rows(lambda h: B_GP // (2 * hd) + 12 + h, 2 * hd)],
        (pb, pb, pb, pb, pb, c_dk, c_dk, c_dv, lam, gsub, *rope128, pb), mixed, 2 * hd, 24 * hd,
        "lat_diff")
    return mixed


def _state_kernel(nk_ref, nv_ref, sk_ref, sv_ref, ckv_ref, kpe_ref, dk_ref, dv_ref,
                  o_nk, o_nv, o_sk, o_sv, o_ckv, o_kpe, o_dk, o_dv):
    for src, dst, heads, w in ((nk_ref, o_nk, 8, HEAD_DIM), (nv_ref, o_nv, 8, HEAD_DIM),
                               (sk_ref, o_sk, 2, HEAD_DIM), (sv_ref, o_sv, 2, HEAD_DIM),
                               (dk_ref, o_dk, 4, 2 * HEAD_DIM), (dv_ref, o_dv, 4, 2 * HEAD_DIM)):
        for h in range(heads):
            dst[:, h, :] = src[:, h * w:(h + 1) * w]
    o_ckv[...] = ckv_ref[...]
    o_kpe[...] = kpe_ref[:, :MLA_ROPE]


def _write_states(pa, pb, ckvn, layer, depth, bc, prev):
    def src(col, w):
        return pl.BlockSpec((SEQ, w), lambda b: (b, col // w))

    def dst(*tail):
        return pl.BlockSpec((None, None, SEQ) + tail, lambda b: (b, layer, 0) + (0,) * len(tail))

    tails = [(8, HEAD_DIM), (8, HEAD_DIM), (2, HEAD_DIM), (2, HEAD_DIM), (MLA_KV_LORA,), (MLA_ROPE,),
             (4, 2 * HEAD_DIM), (4, 2 * HEAD_DIM)]
    return _aliased_call(
        _state_kernel, 8, list(prev), grid=(bc,),
        in_specs=[src(A_NK, 1024), src(A_NV, 1024), src(A_SWK, 256), src(A_SWV, 256),
                  pl.BlockSpec((SEQ, MLA_KV_LORA), lambda b: (b, 0)), src(A_KPE, HEAD_DIM),
                  src(B_DK, 1024), src(B_DV, 1024)],
        out_specs=[dst(*t) for t in tails],
        out_shape=[jax.ShapeDtypeStruct((bc, depth, SEQ) + t, F32) for t in tails],
        compiler_params=_cparams(1), name="write_states",
    )(pa, pa, pa, pa, ckvn, pa, pb, pb, *prev)


def _rope_tables(rot_dim):
    axis_dim = rot_dim // 2
    inv = 1.0 / (ROPE_BASE ** (jnp.arange(0, axis_dim, 2, dtype=F32) / axis_dim))
    t = jnp.arange(DEC_SEQ)
    ang_row = (t // GRID_W).astype(F32)[:, None] * inv
    ang_col = (t % GRID_W).astype(F32)[:, None] * inv
    zeros = jnp.zeros_like(ang_row)
    pad = HEAD_DIM - rot_dim
    cos = jnp.concatenate([jnp.cos(ang_row)] * 2 + [jnp.cos(ang_col)] * 2
                          + [jnp.ones((DEC_SEQ, pad), F32)], axis=-1)
    sin_lo = jnp.concatenate([-jnp.sin(ang_row), zeros, -jnp.sin(ang_col), zeros,
                              jnp.zeros((DEC_SEQ, pad), F32)], axis=-1)
    sin_hi = jnp.concatenate([zeros, jnp.sin(ang_row), zeros, jnp.sin(ang_col),
                              jnp.zeros((DEC_SEQ, pad), F32)], axis=-1)
    return cos, sin_lo, sin_hi


def _nat_bias(rpb):
    n_rows = DEC_SEQ // GRID_W
    steps = len(NAT_PATTERN_STEPS)
    q_per = TQ_WIN // GRID_W
    rep = jnp.asarray(NAT_PATTERN_STEPS)
    qr = rep[:, None] * q_per + jnp.arange(q_per)[None, :]
    kr = jnp.asarray([_nat_win_start(n) for n in NAT_PATTERN_STEPS])[:, None] + jnp.arange(NAT_SPAN)[None, :]
    c = jnp.arange(GRID_W)
    row0 = jnp.clip(qr - NAT_KR // 2, 0, n_rows - NAT_KR)
    col0 = jnp.clip(c - NAT_KC // 2, 0, GRID_W - NAT_KC)
    ok_r = (kr[:, None, :] >= row0[:, :, None]) & (kr[:, None, :] < row0[:, :, None] + NAT_KR)
    ok_c = (c[None, :] >= col0[:, None]) & (c[None, :] < col0[:, None] + NAT_KC)
    dr = jnp.clip(kr[:, None, :] - qr[:, :, None], -(NAT_KR - 1), NAT_KR - 1) + (NAT_KR - 1)
    dc = jnp.clip(c[None, :] - c[:, None], -(NAT_KC - 1), NAT_KC - 1) + (NAT_KC - 1)
    dr, ok_r = jnp.repeat(dr, GRID_W, axis=-1), jnp.repeat(ok_r, GRID_W, axis=-1)
    dc, ok_c = jnp.tile(dc, (1, NAT_SPAN)), jnp.tile(ok_c, (1, NAT_SPAN))
    sel_r = jax.nn.one_hot(dr, 2 * NAT_KR - 1, dtype=F32)
    sel_c = jax.nn.one_hot(dc, 2 * NAT_KC - 1, dtype=F32)
    bias = jnp.einsum("nqKr,lhrc,xKc->lhnqxK", sel_r, rpb.astype(F32), sel_c,
                      precision=lax.Precision.HIGHEST)
    ok = ok_r[:, :, None, :] & ok_c[None, None, :, :]
    bias = jnp.where(ok[None, None], bias * LOG2E, NEG_INF)
    return bias.reshape(rpb.shape[0], rpb.shape[1], steps, TQ_WIN, NAT_SPAN * GRID_W)


def _swa_mask():
    qi = jnp.arange(TQ_WIN)[None, :, None] + TQ_WIN * jnp.arange(3)[:, None, None]
    kj = jnp.arange(3 * TQ_WIN)[None, None, :]
    return jnp.where(jnp.abs(qi - kj) <= SWA_WINDOW, 0.0, NEG_INF).astype(F32)


def _pack_w_uq(w_uq):
    depth = w_uq.shape[0]
    w = w_uq.reshape(depth, MLA_Q_LORA, 8, MLA_NOPE + MLA_ROPE)
    nope = w[..., :MLA_NOPE].reshape(depth, MLA_Q_LORA, 8 * MLA_NOPE)
    pe = jnp.pad(w[..., MLA_NOPE:], ((0, 0), (0, 0), (0, 0), (0, HEAD_DIM - MLA_ROPE)))
    return jnp.concatenate([nope, pe.reshape(depth, MLA_Q_LORA, 8 * HEAD_DIM)], axis=-1).astype(BF16)


def kernel(x_prompt, x_sample, cache_nat_k, cache_nat_v, cache_swa_k, cache_swa_v, cache_mla_ckv, cache_mla_kpe, cache_diff_k, cache_diff_v, c, c_ctx, w_mod, b_mod, g_pre, g_post, w_in, w_out, nat_rpb, swa_sink, mla_g_q, mla_g_kv, mla_w_uq, mla_w_ukv, diff_lambda, diff_g_subln):
    depth = w_mod.shape[0]
    bc, bl = x_prompt.shape[0], x_sample.shape[0]
    n_ctx, n_lat = bc * SEQ, bl * DEC_SEQ
    n_tok = n_ctx + n_lat
    assert n_ctx % DEC_SEQ == 0 and x_prompt.shape[1] == SEQ and x_sample.shape[1] == DEC_SEQ
    ctx = Group(0, n_ctx, lambda i: 0)
    lat = Group(n_ctx, n_lat, lambda i: 1 + i // (DEC_SEQ // ROW_BLOCK))

    n_cond = -(-(1 + bl) // 8) * 8
    cond = jnp.concatenate([c_ctx[None], c, jnp.zeros((n_cond - 1 - bl, D_MODEL), F32)], axis=0)
    mod = _modulation(cond, w_mod, b_mod).reshape(depth, n_cond, 3, D_MODEL)

    w_t = jnp.swapaxes(w_in, 1, 2)
    w_p = _cast_rows(w_t, B_START, N_B, 448, 0, N_PROJ, None, "pack_w_b")
    w_p = _cast_rows(w_t, 0, N_A, 256, N_B, N_PROJ, w_p, "pack_w_a")
    w_out_b = w_out.astype(BF16)
    w_uq_p = _pack_w_uq(mla_w_uq)
    w_ukv_b = mla_w_ukv.astype(BF16)
    g_pre3 = g_pre.reshape(depth, 1, D_MODEL)
    g_post3 = g_post.reshape(depth, 1, D_MODEL)
    g_q3 = mla_g_q.reshape(depth, 1, MLA_Q_LORA)
    g_kv3 = mla_g_kv.reshape(depth, 1, MLA_KV_LORA)
    gsub3 = diff_g_subln.reshape(depth, 1, 2 * HEAD_DIM)
    rope128 = _rope_tables(HEAD_DIM)
    rope64 = _rope_tables(MLA_ROPE)
    nat_bias = _nat_bias(nat_rpb)
    swa_mask = _swa_mask()

    caches = (cache_nat_k.reshape(bl, depth, PAST_LEN * NAT_HEADS, HEAD_DIM),
              cache_nat_v.reshape(bl, depth, PAST_LEN * NAT_HEADS, HEAD_DIM),
              cache_swa_k.reshape(bl, depth, PAST_LEN, -1), cache_swa_v.reshape(bl, depth, PAST_LEN, -1),
              jnp.pad(cache_mla_kpe, ((0, 0), (0, 0), (0, 0), (0, HEAD_DIM - MLA_ROPE))),
              cache_diff_k.reshape(bl, depth, PAST_LEN, -1), cache_diff_v.reshape(bl, depth, PAST_LEN, -1))

    x_c = x_prompt.reshape(n_ctx, D_MODEL)
    x_l = x_sample.reshape(n_lat, D_MODEL)
    h = _norm_mod(x_c, g_pre3, mod, 0, ctx, n_tok, None)
    h = _norm_mod(x_l, g_pre3, mod, 0, lat, n_tok, h)
    states = ()
    for l in range(depth):
        lam_init = 0.8 - 0.6 * math.exp(-0.3 * l)
        p = _matmul_nt(h, w_p, l, 768, "in_proj")
        pa = pb = p
        ckvn = _rms_cols(p, A_CKV, 128, MLA_KV_LORA, g_kv3, l, F32)
        q_up = _mla_q_up(p, g_q3, w_uq_p, l)
        ckv_all = jnp.concatenate([ckvn, cache_mla_ckv[:, l].reshape(bl * PAST_LEN, MLA_KV_LORA)], axis=0)
        kv_up = _matmul(ckv_all.astype(BF16), w_ukv_b, l, 1024, "mla_kv_up", out_dtype=BF16)
        states = _write_states(pa, pb, ckvn, l, depth, bc, states)

        mixed = _ctx_attention(pa, pb, q_up, kv_up, swa_sink, diff_lambda, gsub3, l, lam_init, ctx, n_tok)
        mixed = _lat_attention(mixed, pa, pb, q_up, kv_up, caches, l, nat_bias, swa_mask, swa_sink,
                               diff_lambda, gsub3, lam_init, rope128, rope64, lat, n_tok)
        o = _matmul(mixed, w_out_b, l, 1024, "out_proj", out_dtype=BF16)
        if l + 1 < depth:
            x_c, h = _post_norm(x_c, o, g_post3, g_pre3, mod, l, ctx, n_tok, None)
            x_l, h = _post_norm(x_l, o, g_post3, g_pre3, mod, l, lat, n_tok, h)
        else:
            x_c = _post(x_c, o, g_post3, mod, l, ctx)
            x_l = _post(x_l, o, g_post3, mod, l, lat)

    return (x_c.reshape(bc, SEQ, D_MODEL), x_l.reshape(bl, DEC_SEQ, D_MODEL)) + tuple(states)
```

```python
import functools
import math
from typing import Callable, NamedTuple

import jax
import jax.numpy as jnp
from jax import lax
from jax.experimental import pallas as pl
from jax.experimental.pallas import tpu as pltpu

F32 = jnp.float32
BF16 = jnp.bfloat16

D_MODEL = 4096
HEAD_DIM = 128
GRID_W = 64
SEQ = 256
DEC_SEQ = 1024
PAST_LEN = 512
NAT_KR, NAT_KC = 8, 16
NAT_HEADS = 8
SWA_WINDOW = 128
MLA_NOPE, MLA_ROPE, MLA_V = 128, 64, 128
MLA_Q_LORA, MLA_KV_LORA = 896, 256
ROPE_BASE = 10000.0
EPS = 1e-6
NEG_INF = -1e30
LOG2E = 1.4426950408889634
QSCALE = HEAD_DIM ** -0.5 * LOG2E
QSCALE_MLA = (MLA_NOPE + MLA_ROPE) ** -0.5 * LOG2E

B_START = 5824
N_B = 7168
N_A = 5888
N_PROJ = N_B + N_A
B_DQ, B_DK, B_DV, B_GP = 0, 1024, 2048, 3072
A_NQ, A_NK, A_NV = N_B + 0, N_B + 1024, N_B + 2048
A_SWQ, A_SWK, A_SWV = N_B + 3072, N_B + 4096, N_B + 4352
A_CQ, A_CKV, A_KPE = N_B + 4608, N_B + 5504, N_B + 5760

ROW_BLOCK = 256
TQ_WIN = 128
TQ_DENSE = 256
NAT_SPAN = 10
VMEM_LIMIT = 56 * 1024 * 1024


class Group(NamedTuple):
    row0: int
    rows: int
    mod_row: Callable


def _cparams(n_axes):
    return pltpu.CompilerParams(dimension_semantics=("arbitrary",) * n_axes,
                                vmem_limit_bytes=VMEM_LIMIT)


def _silu(x):
    return x * (1.0 / (1.0 + jnp.exp(-x)))


def _nt(a, b):
    return lax.dot_general(a, b, (((1,), (1,)), ((), ())), preferred_element_type=F32)


def _softmax_pv(parts, extra=None):
    m = parts[0][0].max(axis=-1, keepdims=True)
    for s, _ in parts[1:]:
        m = jnp.maximum(m, s.max(axis=-1, keepdims=True))
    if extra is not None:
        m = jnp.maximum(m, extra)
    den = None
    acc = None
    for s, v in parts:
        p = jnp.exp2(s - m)
        ps = p.sum(axis=-1, keepdims=True)
        pv = jnp.dot(p.astype(BF16), v, preferred_element_type=F32)
        den = ps if den is None else den + ps
        acc = pv if acc is None else acc + pv
    if extra is not None:
        den = den + jnp.exp2(extra - m)
    return acc / den


def _rope(x, cos, sin_lo, sin_hi, half):
    lanes = x.shape[-1]
    return (x * cos + pltpu.roll(x, lanes - half, 1) * sin_lo + pltpu.roll(x, half, 1) * sin_hi)


def _rms(x, g):
    return x * lax.rsqrt(jnp.mean(x * x, axis=-1, keepdims=True) + EPS) * g


def _aliased_call(body, n_in, alias_args, **kw):
    n_alias = len(alias_args)
    n_out = len(kw["out_shape"]) if isinstance(kw["out_shape"], (tuple, list)) else 1
    if n_alias == 0:
        return pl.pallas_call(body, **kw)

    def with_aliases(*refs):
        body(*refs[:n_in], *refs[n_in + n_alias:])

    kw["in_specs"] = list(kw["in_specs"]) + [pl.BlockSpec(memory_space=pl.ANY)] * n_alias
    kw["input_output_aliases"] = {n_in + k: n_out - n_alias + k for k in range(n_alias)}
    return pl.pallas_call(with_aliases, **kw)


def _mod_kernel(c_ref, w_ref, b_ref, o_ref):
    s = _silu(c_ref[...]).astype(BF16)
    o_ref[...] = jnp.dot(s, w_ref[...].astype(BF16), preferred_element_type=F32) + b_ref[...]


def _modulation(cond, w_mod, b_mod):
    depth = w_mod.shape[0]
    rows = cond.shape[0]
    tn = 512
    n = 3 * D_MODEL
    return pl.pallas_call(
        _mod_kernel,
        grid=(depth, n // tn),
        in_specs=[
            pl.BlockSpec((rows, D_MODEL), lambda l, j: (0, 0)),
            pl.BlockSpec((None, D_MODEL, tn), lambda l, j: (l, 0, j)),
            pl.BlockSpec((None, 1, tn), lambda l, j: (l, 0, j)),
        ],
        out_specs=pl.BlockSpec((None, rows, tn), lambda l, j: (l, 0, j)),
        out_shape=jax.ShapeDtypeStruct((depth, rows, n), F32),
        compiler_params=_cparams(2),
        name="modulation",
    )(cond, w_mod, b_mod.reshape(depth, 1, n))


def _mod_spec(grp, layer):
    return pl.BlockSpec((None, None, 3, D_MODEL), lambda i: (layer, grp.mod_row(i), 0, 0))


def _vec_spec(layer):
    return pl.BlockSpec((None, 1, D_MODEL), lambda i: (layer, 0, 0))


def _tok_spec(grp):
    return pl.BlockSpec((ROW_BLOCK, D_MODEL), lambda i: (grp.row0 // ROW_BLOCK + i, 0))


def _own_spec():
    return pl.BlockSpec((ROW_BLOCK, D_MODEL), lambda i: (i, 0))


def _norm_mod_kernel(x_ref, g_ref, m_ref, h_ref):
    y = _rms(x_ref[...], g_ref[...])
    h_ref[...] = (y * (1.0 + m_ref[1:2, :]) + m_ref[0:1, :]).astype(BF16)


def _norm_mod(x, g_pre, mod, layer, grp, n_tok, h_prev):
    return _aliased_call(
        _norm_mod_kernel, 3, [] if h_prev is None else [h_prev],
        grid=(grp.rows // ROW_BLOCK,),
        in_specs=[_own_spec(), _vec_spec(layer), _mod_spec(grp, layer)],
        out_specs=_tok_spec(grp),
        out_shape=jax.ShapeDtypeStruct((n_tok, D_MODEL), BF16),
        compiler_params=_cparams(1),
        name="norm_mod",
    )(x, g_pre, mod, *([] if h_prev is None else [h_prev]))


def _post_kernel(x_ref, o_ref, g_ref, m_ref, y_ref):
    y_ref[...] = x_ref[...] + m_ref[2:3, :] * _rms(o_ref[...].astype(F32), g_ref[...])


def _post(x, o, g_post, mod, layer, grp):
    return pl.pallas_call(
        _post_kernel,
        grid=(grp.rows // ROW_BLOCK,),
        in_specs=[_own_spec(), _tok_spec(grp), _vec_spec(layer), _mod_spec(grp, layer)],
        out_specs=_own_spec(),
        out_shape=jax.ShapeDtypeStruct((grp.rows, D_MODEL), F32),
        compiler_params=_cparams(1),
        name="post_residual",
    )(x, o, g_post, mod)


def _post_norm_kernel(x_ref, o_ref, g_ref, m_ref, gn_ref, mn_ref, y_ref, h_ref):
    y = x_ref[...] + m_ref[2:3, :] * _rms(o_ref[...].astype(F32), g_ref[...])
    y_ref[...] = y
    h_ref[...] = (_rms(y, gn_ref[...]) * (1.0 + mn_ref[1:2, :]) + mn_ref[0:1, :]).astype(BF16)


def _post_norm(x, o, g_post, g_pre, mod, layer, grp, n_tok, h_prev):
    return _aliased_call(
        _post_norm_kernel, 6, [] if h_prev is None else [h_prev],
        grid=(grp.rows // ROW_BLOCK,),
        in_specs=[_own_spec(), _tok_spec(grp), _vec_spec(layer), _mod_spec(grp, layer),
                  _vec_spec(layer + 1), _mod_spec(grp, layer + 1)],
        out_specs=[_own_spec(), _tok_spec(grp)],
        out_shape=[jax.ShapeDtypeStruct((grp.rows, D_MODEL), F32),
                   jax.ShapeDtypeStruct((n_tok, D_MODEL), BF16)],
        compiler_params=_cparams(1),
        name="post_norm",
    )(x, o, g_post, mod, g_pre, mod, *([] if h_prev is None else [h_prev]))


def _rms_cols_kernel(a_ref, b_ref, g_ref, y_ref, *, wa, wb):
    x = jnp.concatenate([a_ref[:, :wa], b_ref[:, :wb]], axis=-1)
    y_ref[...] = _rms(x, g_ref[...]).astype(y_ref.dtype)


def _rms_cols(p, col, block_w, width, g, layer, out_dtype):
    m = p.shape[0]
    tm = 1024
    first = col // block_w
    return pl.pallas_call(
        functools.partial(_rms_cols_kernel, wa=block_w, wb=width - block_w),
        grid=(m // tm,),
        in_specs=[
            pl.BlockSpec((tm, block_w), lambda i: (i, first)),
            pl.BlockSpec((tm, block_w), lambda i: (i, first + 1)),
            pl.BlockSpec((None, 1, width), lambda i: (layer, 0, 0)),
        ],
        out_specs=pl.BlockSpec((tm, width), lambda i: (i, 0)),
        out_shape=jax.ShapeDtypeStruct((m, width), out_dtype),
        compiler_params=_cparams(1),
        name="rms_cols",
    )(p, p, g)


def _q_up_kernel(a_ref, b_ref, g_ref, w_ref, o_ref):
    cq = jnp.concatenate([a_ref[...], b_ref[:, :MLA_Q_LORA - 512]], axis=-1)
    o_ref[...] = jnp.dot(_rms(cq, g_ref[...]).astype(BF16), w_ref[...], preferred_element_type=F32)


def _mla_q_up(p, g, w_uq, layer):
    m = p.shape[0]
    tm = 512
    n = w_uq.shape[-1]
    first = A_CQ // 512
    return pl.pallas_call(
        _q_up_kernel,
        grid=(m // tm,),
        in_specs=[
            pl.BlockSpec((tm, 512), lambda i: (i, first)),
            pl.BlockSpec((tm, 512), lambda i: (i, first + 1)),
            pl.BlockSpec((None, 1, MLA_Q_LORA), lambda i: (layer, 0, 0)),
            pl.BlockSpec((None, MLA_Q_LORA, n), lambda i: (layer, 0, 0)),
        ],
        out_specs=pl.BlockSpec((tm, n), lambda i: (i, 0)),
        out_shape=jax.ShapeDtypeStruct((m, n), F32),
        compiler_params=_cparams(1),
        name="mla_q_up",
    )(p, p, g, w_uq)


def _mm_kernel(a_ref, b_ref, o_ref):
    o_ref[...] = jnp.dot(a_ref[...], b_ref[...], preferred_element_type=F32).astype(o_ref.dtype)


def _matmul(a, w, layer, tn, name, n=None, out_dtype=F32):
    m, k = a.shape
    n = w.shape[-1] if n is None else n
    tm = 1024 if m % 1024 == 0 else 512
    return pl.pallas_call(
        _mm_kernel,
        grid=(m // tm, n // tn),
        in_specs=[
            pl.BlockSpec((tm, k), lambda i, j: (i, 0)),
            pl.BlockSpec((None, k, tn), lambda i, j: (layer, 0, j)),
        ],
        out_specs=pl.BlockSpec((tm, tn), lambda i, j: (i, j)),
        out_shape=jax.ShapeDtypeStruct((m, n), out_dtype),
        compiler_params=_cparams(2),
        name=name,
    )(a, w)


def _cast_kernel(w_ref, o_ref):
    o_ref[...] = w_ref[...].astype(BF16)


def _cast_rows(w_t, row0, rows, block_rows, out_row0, out_rows, prev, name):
    depth, _, k = w_t.shape
    first, out_first = row0 // block_rows, out_row0 // block_rows
    assert row0 % block_rows == 0 and rows % block_rows == 0 and out_row0 % block_rows == 0
    return _aliased_call(
        _cast_kernel, 1, [] if prev is None else [prev],
        grid=(depth, rows // block_rows),
        in_specs=[pl.BlockSpec((None, block_rows, k), lambda l, j: (l, first + j, 0))],
        out_specs=pl.BlockSpec((None, block_rows, k), lambda l, j: (l, out_first + j, 0)),
        out_shape=jax.ShapeDtypeStruct((depth, out_rows, k), BF16),
        compiler_params=_cparams(2),
        name=name,
    )(w_t, *([] if prev is None else [prev]))


def _mm_nt_kernel(a_ref, b_ref, o_ref):
    o_ref[...] = _nt(a_ref[...], b_ref[...]).astype(o_ref.dtype)


def _matmul_nt(a, w_t, layer, tn, name):
    m, k = a.shape
    n = w_t.shape[1]
    tm = 1024 if m % 1024 == 0 else 512
    return pl.pallas_call(
        _mm_nt_kernel,
        grid=(m // tm, n // tn),
        in_specs=[
            pl.BlockSpec((tm, k), lambda i, j: (i, 0)),
            pl.BlockSpec((None, tn, k), lambda i, j: (layer, j, 0)),
        ],
        out_specs=pl.BlockSpec((tm, tn), lambda i, j: (i, j)),
        out_shape=jax.ShapeDtypeStruct((m, n), F32),
        compiler_params=_cparams(2),
        name=name,
    )(a, w_t)


def _gated(o, gp):
    return (o * _silu(gp)).astype(BF16)


def _pipelined(n_steps, scores, finish, ahead=1):
    queue = [scores(n) for n in range(min(ahead, n_steps))]
    for n in range(n_steps):
        cur = queue.pop(0)
        if n + ahead < n_steps:
            queue.append(scores(n + ahead))
        finish(n, cur)


def _head(h, w=HEAD_DIM):
    return slice(h * w, (h + 1) * w)


def _ctx_a_kernel(q_ref, k_ref, v_ref, gp_ref, o_ref):
    def scores(h):
        q = (q_ref[:, _head(h)] * QSCALE).astype(BF16)
        return [(_nt(q, k_ref[:, _head(h)].astype(BF16)), v_ref[:, _head(h)].astype(BF16))]

    def finish(h, parts):
        o_ref[:, _head(h)] = _gated(_softmax_pv(parts), gp_ref[:, _head(h)])

    _pipelined(8, scores, finish)


def _ctx_b_kernel(q_ref, k_ref, v_ref, sink_ref, gp_ref, o_ref, *, layer):
    def scores(h):
        q = (q_ref[:, _head(h)] * QSCALE).astype(BF16)
        return [(_nt(q, k_ref[:, _head(h // 4)].astype(BF16)), v_ref[:, _head(h // 4)].astype(BF16))]

    def finish(h, parts):
        o = _softmax_pv(parts, extra=sink_ref[layer, h] * LOG2E)
        o_ref[:, _head(h)] = _gated(o, gp_ref[:, _head(h)])

    _pipelined(8, scores, finish)


def _ctx_c_kernel(qn_ref, qp_ref, kv_ref, kpe_ref, gp_ref, o_ref):
    kpe = kpe_ref[...].astype(BF16)

    def scores(h):
        kn = kv_ref[:, _head(2 * h)].astype(BF16)
        q = (jnp.concatenate([qn_ref[:, _head(h)], qp_ref[:, _head(h)]], axis=-1) * QSCALE_MLA).astype(BF16)
        return [(_nt(q, jnp.concatenate([kn, kpe], axis=-1)), kv_ref[:, _head(2 * h + 1)].astype(BF16))]

    def finish(h, parts):
        o_ref[:, _head(h)] = _gated(_softmax_pv(parts), gp_ref[:, _head(h)])

    _pipelined(8, scores, finish)


def _diff_lambda(lam_ref, lam_init):
    lp = lam_ref[...]
    a = jnp.sum(lp[0:1, :] * lp[1:2, :], axis=-1, keepdims=True)
    b = jnp.sum(lp[2:3, :] * lp[3:4, :], axis=-1, keepdims=True)
    return jnp.exp(a) - jnp.exp(b) + lam_init


def _diff_finish(o1, o2, lam, gsub, lam_init):
    return _rms(o1 - lam * o2, gsub) * (1.0 - lam_init)


def _ctx_d_kernel(q_ref, k_ref, v_ref, lam_ref, gsub_ref, gp_ref, o_ref, *, lam_init):
    lam = _diff_lambda(lam_ref, lam_init)
    gsub = gsub_ref[...]

    first_map = {}

    def scores(t):
        q = (q_ref[:, _head(t)] * QSCALE).astype(BF16)
        return [(_nt(q, k_ref[:, _head(t)].astype(BF16)), v_ref[:, _head(t // 2, 2 * HEAD_DIM)].astype(BF16))]

    def finish(t, parts):
        o = _softmax_pv(parts)
        if t % 2 == 0:
            first_map[t // 2] = o
            return
        sl = _head(t // 2, 2 * HEAD_DIM)
        o = _diff_finish(first_map.pop(t // 2), o, lam, gsub, lam_init)
        o_ref[:, sl] = _gated(o, gp_ref[:, sl])

    _pipelined(8, scores, finish)


STATE_TAILS = ((8, HEAD_DIM), (8, HEAD_DIM), (2, HEAD_DIM), (2, HEAD_DIM), (MLA_KV_LORA,), (MLA_ROPE,),
               (4, 2 * HEAD_DIM), (4, 2 * HEAD_DIM))


def _ctx_attention(pa, pb, q_up, kv_up, ckvn, sink, lam, gsub, layer, depth, lam_init, ctx, n_tok, states):
    nb = ctx.rows // SEQ
    wide = 8 * HEAD_DIM

    def blk(col, w=wide):
        return pl.BlockSpec((SEQ, w), lambda b: (b, col // w))

    def small(shape):
        return pl.BlockSpec((None,) + shape, lambda b: (layer,) + (0,) * len(shape))

    def state(*tail):
        return pl.BlockSpec((None, None, SEQ) + tail, lambda b: (b, layer, 0) + (0,) * len(tail))

    def body(aq, ak, av, bq, bk, bv, sink_ref, cqn, cqp, ckv, ckpe, dq, dk, dv, lam_ref, gsub_ref,
             gp_a, gp_b, gp_c, gp_d, ckvn_ref, o_ref, *state_refs):
        def out(mixer):
            return o_ref.at[:, mixer * wide:(mixer + 1) * wide]

        _ctx_a_kernel(aq, ak, av, gp_a, out(0))
        _ctx_b_kernel(bq, bk, bv, sink_ref, gp_b, out(1), layer=layer)
        _ctx_c_kernel(cqn, cqp, ckv, ckpe, gp_c, out(2))
        _ctx_d_kernel(dq, dk, dv, lam_ref, gsub_ref, gp_d, out(3), lam_init=lam_init)
        _state_kernel(ak, av, bk, bv, ckvn_ref, ckpe, dk, dv, *state_refs)

    outs = _aliased_call(
        body, 21, list(states), grid=(nb,),
        in_specs=[blk(A_NQ), blk(A_NK), blk(A_NV),
                  blk(A_SWQ), blk(A_SWK, 256), blk(A_SWV, 256), pl.BlockSpec(memory_space=pltpu.SMEM),
                  blk(0), blk(wide), blk(0, 2 * wide), blk(A_KPE, HEAD_DIM),
                  blk(B_DQ), blk(B_DK), blk(B_DV), small((4, HEAD_DIM)), small((1, 2 * HEAD_DIM))]
                 + [blk(B_GP + mixer * wide) for mixer in range(4)]
                 + [pl.BlockSpec((SEQ, MLA_KV_LORA), lambda b: (b, 0))],
        out_specs=[pl.BlockSpec((SEQ, 4 * wide), lambda b: (b, 0))] + [state(*t) for t in STATE_TAILS],
        out_shape=[jax.ShapeDtypeStruct((n_tok, D_MODEL), BF16)]
                  + [jax.ShapeDtypeStruct((nb, depth, SEQ) + t, F32) for t in STATE_TAILS],
        compiler_params=_cparams(1), name="ctx_mixers",
    )(pa, pa, pa, pa, pa, pa, sink, q_up, q_up, kv_up, pa, pb, pb, pb, lam, gsub, pb, pb, pb, pb, ckvn,
      *states)
    return outs[0], tuple(outs[1:])


def _nat_win_start(n):
    n_rows = DEC_SEQ // GRID_W
    return min(max(2 * n - NAT_KR // 2, 0), n_rows - NAT_SPAN)


def _nat_pattern_key(n):
    n_rows = DEC_SEQ // GRID_W
    q_per = TQ_WIN // GRID_W
    s = _nat_win_start(n)
    return tuple((q_per * n + j - s, min(max(q_per * n + j - NAT_KR // 2, 0), n_rows - NAT_KR) - s)
                 for j in range(q_per))


NAT_PATTERN_STEPS = tuple(sorted({_nat_pattern_key(n): n for n in reversed(range(DEC_SEQ // TQ_WIN))}.values()))
NAT_PATTERN = tuple([_nat_pattern_key(m) for m in NAT_PATTERN_STEPS].index(_nat_pattern_key(n))
                    for n in range(DEC_SEQ // TQ_WIN))


def _lat_a_kernel(q_ref, k_ref, v_ref, ck_ref, cv_ref, bias_ref, gp_ref, o_ref):
    head_rows = pl.ds(pl.program_id(1), PAST_LEN, stride=NAT_HEADS)
    k = k_ref[...].astype(BF16)
    v = v_ref[...].astype(BF16)
    ck = ck_ref[head_rows, :].astype(BF16)
    cv = cv_ref[head_rows, :].astype(BF16)

    def scores(n):
        win = slice(_nat_win_start(n) * GRID_W, (_nat_win_start(n) + NAT_SPAN) * GRID_W)
        q = (q_ref[_head(n, TQ_WIN), :] * QSCALE).astype(BF16)
        return [(_nt(q, ck), cv), (_nt(q, k[win]) + bias_ref[NAT_PATTERN[n]], v[win])]

    def finish(n, parts):
        rows = _head(n, TQ_WIN)
        o_ref[rows, :] = _gated(_softmax_pv(parts), gp_ref[rows, :])

    _pipelined(DEC_SEQ // TQ_WIN, scores, finish, ahead=2)


def _swa_win_start(n):
    return min(max(n - 1, 0), DEC_SEQ // TQ_WIN - 3) * TQ_WIN


def _lat_b_kernel(q_ref, k_ref, v_ref, ck_ref, cv_ref, sink_ref, mask_ref, cos_ref, sl_ref, sh_ref,
                  gp_ref, o_ref, *, layer):
    k = _rope(k_ref[...], cos_ref[...], sl_ref[...], sh_ref[...], 32).astype(BF16)
    v = v_ref[...].astype(BF16)
    ck = ck_ref[...].astype(BF16)
    cv = cv_ref[...].astype(BF16)
    sink = sink_ref[layer, pl.program_id(1)] * LOG2E

    def scores(n):
        rows = _head(n, TQ_WIN)
        w0 = _swa_win_start(n)
        win = slice(w0, w0 + 3 * TQ_WIN)
        q = _rope(q_ref[rows, :], cos_ref[rows, :], sl_ref[rows, :], sh_ref[rows, :], 32)
        q = (q * QSCALE).astype(BF16)
        return [(_nt(q, ck), cv), (_nt(q, k[win]) + mask_ref[n - w0 // TQ_WIN], v[win])]

    def finish(n, parts):
        rows = _head(n, TQ_WIN)
        o_ref[rows, :] = _gated(_softmax_pv(parts, extra=sink), gp_ref[rows, :])

    _pipelined(DEC_SEQ // TQ_WIN, scores, finish, ahead=2)


def _lat_c_kernel(qn_ref, qp_ref, kn_ref, v_ref, kpe_ref, ckn_ref, cv_ref, ckpe_ref,
                  cos_ref, sl_ref, sh_ref, gp_ref, o_ref):
    kpe = _rope(kpe_ref[...], cos_ref[...], sl_ref[...], sh_ref[...], 16)
    keys = jnp.concatenate([
        jnp.concatenate([ckn_ref[...], ckpe_ref[...].astype(BF16)], axis=-1),
        jnp.concatenate([kn_ref[...], kpe.astype(BF16)], axis=-1)], axis=0)
    vals = jnp.concatenate([cv_ref[...], v_ref[...]], axis=0)

    def scores(n):
        rows = _head(n, TQ_DENSE)
        qp = _rope(qp_ref[rows, :], cos_ref[rows, :], sl_ref[rows, :], sh_ref[rows, :], 16)
        q = (jnp.concatenate([qn_ref[rows, :], qp], axis=-1) * QSCALE_MLA).astype(BF16)
        return [(_nt(q, keys), vals)]

    def finish(n, parts):
        rows = _head(n, TQ_DENSE)
        o_ref[rows, :] = _gated(_softmax_pv(parts), gp_ref[rows, :])

    _pipelined(DEC_SEQ // TQ_DENSE, scores, finish, ahead=2)


def _lat_d_kernel(q1_ref, q2_ref, k1_ref, k2_ref, v_ref, ck1_ref, ck2_ref, cv_ref,
                  lam_ref, gsub_ref, cos_ref, sl_ref, sh_ref, gp_ref, o_ref, *, lam_init):
    cos, sl, sh = cos_ref[...], sl_ref[...], sh_ref[...]
    keys1 = jnp.concatenate([ck1_ref[...], _rope(k1_ref[...], cos, sl, sh, 32)], axis=0).astype(BF16)
    keys2 = jnp.concatenate([ck2_ref[...], _rope(k2_ref[...], cos, sl, sh, 32)], axis=0).astype(BF16)
    vals = jnp.concatenate([cv_ref[...], v_ref[...]], axis=0).astype(BF16)
    lam = _diff_lambda(lam_ref, lam_init)
    gsub = gsub_ref[...]
    first_map = {}

    def scores(t):
        rows = _head(t // 2, TQ_DENSE)
        q_ref, keys = ((q1_ref, keys1), (q2_ref, keys2))[t % 2]
        q = _rope(q_ref[rows, :], cos_ref[rows, :], sl_ref[rows, :], sh_ref[rows, :], 32)
        return [(_nt((q * QSCALE).astype(BF16), keys), vals)]

    def finish(t, parts):
        o = _softmax_pv(parts)
        if t % 2 == 0:
            first_map[t // 2] = o
            return
        rows = _head(t // 2, TQ_DENSE)
        o = _diff_finish(first_map.pop(t // 2), o, lam, gsub, lam_init)
        o_ref[rows, :] = _gated(o, gp_ref[rows, :])

    _pipelined(2 * (DEC_SEQ // TQ_DENSE), scores, finish)


def _lat_attention(mixed, pa, pb, q_up, kv_up, caches, layer, nat_bias, swa_mask, sink, lam,
                   gsub, lam_init, rope128, rope64, lat, n_tok):
    nb = lat.rows // DEC_SEQ
    rb0 = lat.row0 // DEC_SEQ
    cb0 = n_tok // PAST_LEN
    c_nat_k, c_nat_v, c_swa_k, c_swa_v, c_kpe, c_dk, c_dv = caches
    hd = HEAD_DIM

    def rows(col_of, w=hd):
        return pl.BlockSpec((DEC_SEQ, w), lambda b, h: (rb0 + b, col_of(h)))

    def cache(col_of, w=hd):
        return pl.BlockSpec((None, None, PAST_LEN, w), lambda b, h: (b, layer, 0, col_of(h)))

    def table():
        return pl.BlockSpec((DEC_SEQ, hd), lambda b, h: (0, 0))

    def small(shape):
        return pl.BlockSpec((None,) + shape, lambda b, h: (layer,) + (0,) * len(shape))

    all_heads = pl.BlockSpec((None, None, PAST_LEN * NAT_HEADS, hd), lambda b, h: (b, layer, 0, 0))

    def call(body, n_in, heads, in_specs, args, mixed, w, col0, name):
        return _aliased_call(
            body, n_in, [mixed], grid=(nb, heads), in_specs=in_specs,
            out_specs=rows(lambda h: col0 // w + h, w),
            out_shape=jax.ShapeDtypeStruct((n_tok, D_MODEL), BF16),
            compiler_params=_cparams(2), name=name,
        )(*args, mixed)

    mixed = call(
        _lat_a_kernel, 7, 8,
        [rows(lambda h: A_NQ // hd + h), rows(lambda h: A_NK // hd + h), rows(lambda h: A_NV // hd + h),
         all_heads, all_heads,
         pl.BlockSpec((None, None, len(NAT_PATTERN_STEPS), TQ_WIN, NAT_SPAN * GRID_W),
                      lambda b, h: (layer, h, 0, 0, 0)),
         rows(lambda h: B_GP // hd + h)],
        (pa, pa, pa, c_nat_k, c_nat_v, nat_bias, pb), mixed, hd, 0, "lat_nat")
    mixed = call(
        functools.partial(_lat_b_kernel, layer=layer), 11, 8,
        [rows(lambda h: A_SWQ // hd + h), rows(lambda h: A_SWK // hd + h // 4),
         rows(lambda h: A_SWV // hd + h // 4), cache(lambda h: h // 4), cache(lambda h: h // 4),
         pl.BlockSpec(memory_space=pltpu.SMEM),
         pl.BlockSpec((3, TQ_WIN, 3 * TQ_WIN), lambda b, h: (0, 0, 0)), table(), table(), table(),
         rows(lambda h: B_GP // hd + 8 + h)],
        (pa, pa, pa, c_swa_k, c_swa_v, sink, swa_mask, *rope128, pb), mixed, hd, 8 * hd, "lat_swa")
    mixed = call(
        _lat_c_kernel, 12, 8,
        [rows(lambda h: h), rows(lambda h: 8 + h),
         rows(lambda h: 2 * h), rows(lambda h: 2 * h + 1), rows(lambda h: A_KPE // hd),
         pl.BlockSpec((PAST_LEN, hd), lambda b, h: (cb0 + b, 2 * h)),
         pl.BlockSpec((PAST_LEN, hd), lambda b, h: (cb0 + b, 2 * h + 1)),
         cache(lambda h: 0), table(), table(), table(),
         rows(lambda h: B_GP // hd + 16 + h)],
        (q_up, q_up, kv_up, kv_up, pa, kv_up, kv_up, c_kpe, *rope64, pb), mixed, hd, 16 * hd, "lat_mla")
    mixed = call(
        functools.partial(_lat_d_kernel, lam_init=lam_init), 14, 4,
        [rows(lambda h: B_DQ // hd + 2 * h), rows(lambda h: B_DQ // hd + 2 * h + 1),
         rows(lambda h: B_DK // hd + 2 * h), rows(lambda h: B_DK // hd + 2 * h + 1),
         rows(lambda h: B_DV // (2 * hd) + h, 2 * hd),
         cache(lambda h: 2 * h), cache(lambda h: 2 * h + 1), cache(lambda h: h, 2 * hd),
         small((4, hd)), small((1, 2 * hd)), table(), table(), table(),
         rows(lambda h: B_GP // (2 * hd) + 12 + h, 2 * hd)],
        (pb, pb, pb, pb, pb, c_dk, c_dk, c_dv, lam, gsub, *rope128, pb), mixed, 2 * hd, 24 * hd,
        "lat_diff")
    return mixed


def _state_kernel(nk_ref, nv_ref, sk_ref, sv_ref, ckv_ref, kpe_ref, dk_ref, dv_ref,
                  o_nk, o_nv, o_sk, o_sv, o_ckv, o_kpe, o_dk, o_dv):
    for src, dst, heads, w in ((nk_ref, o_nk, 8, HEAD_DIM), (nv_ref, o_nv, 8, HEAD_DIM),
                               (sk_ref, o_sk, 2, HEAD_DIM), (sv_ref, o_sv, 2, HEAD_DIM),
                               (dk_ref, o_dk, 4, 2 * HEAD_DIM), (dv_ref, o_dv, 4, 2 * HEAD_DIM)):
        for h in range(heads):
            dst[:, h, :] = src[:, h * w:(h + 1) * w]
    o_ckv[...] = ckv_ref[...]
    o_kpe[...] = kpe_ref[:, :MLA_ROPE]


def _rope_tables(rot_dim):
    axis_dim = rot_dim // 2
    inv = 1.0 / (ROPE_BASE ** (jnp.arange(0, axis_dim, 2, dtype=F32) / axis_dim))
    t = jnp.arange(DEC_SEQ)
    ang_row = (t // GRID_W).astype(F32)[:, None] * inv
    ang_col = (t % GRID_W).astype(F32)[:, None] * inv
    zeros = jnp.zeros_like(ang_row)
    pad = HEAD_DIM - rot_dim
    cos = jnp.concatenate([jnp.cos(ang_row)] * 2 + [jnp.cos(ang_col)] * 2
                          + [jnp.ones((DEC_SEQ, pad), F32)], axis=-1)
    sin_lo = jnp.concatenate([-jnp.sin(ang_row), zeros, -jnp.sin(ang_col), zeros,
                              jnp.zeros((DEC_SEQ, pad), F32)], axis=-1)
    sin_hi = jnp.concatenate([zeros, jnp.sin(ang_row), zeros, jnp.sin(ang_col),
                              jnp.zeros((DEC_SEQ, pad), F32)], axis=-1)
    return cos, sin_lo, sin_hi


def _nat_bias(rpb):
    n_rows = DEC_SEQ // GRID_W
    steps = len(NAT_PATTERN_STEPS)
    q_per = TQ_WIN // GRID_W
    rep = jnp.asarray(NAT_PATTERN_STEPS)
    qr = rep[:, None] * q_per + jnp.arange(q_per)[None, :]
    kr = jnp.asarray([_nat_win_start(n) for n in NAT_PATTERN_STEPS])[:, None] + jnp.arange(NAT_SPAN)[None, :]
    c = jnp.arange(GRID_W)
    row0 = jnp.clip(qr - NAT_KR // 2, 0, n_rows - NAT_KR)
    col0 = jnp.clip(c - NAT_KC // 2, 0, GRID_W - NAT_KC)
    ok_r = (kr[:, None, :] >= row0[:, :, None]) & (kr[:, None, :] < row0[:, :, None] + NAT_KR)
    ok_c = (c[None, :] >= col0[:, None]) & (c[None, :] < col0[:, None] + NAT_KC)
    dr = jnp.clip(kr[:, None, :] - qr[:, :, None], -(NAT_KR - 1), NAT_KR - 1) + (NAT_KR - 1)
    dc = jnp.clip(c[None, :] - c[:, None], -(NAT_KC - 1), NAT_KC - 1) + (NAT_KC - 1)
    dr, ok_r = jnp.repeat(dr, GRID_W, axis=-1), jnp.repeat(ok_r, GRID_W, axis=-1)
    dc, ok_c = jnp.tile(dc, (1, NAT_SPAN)), jnp.tile(ok_c, (1, NAT_SPAN))
    sel_r = jax.nn.one_hot(dr, 2 * NAT_KR - 1, dtype=F32)
    sel_c = jax.nn.one_hot(dc, 2 * NAT_KC - 1, dtype=F32)
    bias = jnp.einsum("nqKr,lhrc,xKc->lhnqxK", sel_r, rpb.astype(F32), sel_c,
                      precision=lax.Precision.HIGHEST)
    ok = ok_r[:, :, None, :] & ok_c[None, None, :, :]
    bias = jnp.where(ok[None, None], bias * LOG2E, NEG_INF)
    return bias.reshape(rpb.shape[0], rpb.shape[1], steps, TQ_WIN, NAT_SPAN * GRID_W)


def _swa_mask():
    qi = jnp.arange(TQ_WIN)[None, :, None] + TQ_WIN * jnp.arange(3)[:, None, None]
    kj = jnp.arange(3 * TQ_WIN)[None, None, :]
    return jnp.where(jnp.abs(qi - kj) <= SWA_WINDOW, 0.0, NEG_INF).astype(F32)


def _pack_w_uq(w_uq):
    depth = w_uq.shape[0]
    w = w_uq.reshape(depth, MLA_Q_LORA, 8, MLA_NOPE + MLA_ROPE)
    nope = w[..., :MLA_NOPE].reshape(depth, MLA_Q_LORA, 8 * MLA_NOPE)
    pe = jnp.pad(w[..., MLA_NOPE:], ((0, 0), (0, 0), (0, 0), (0, HEAD_DIM - MLA_ROPE)))
    return jnp.concatenate([nope, pe.reshape(depth, MLA_Q_LORA, 8 * HEAD_DIM)], axis=-1).astype(BF16)


def kernel(x_prompt, x_sample, cache_nat_k, cache_nat_v, cache_swa_k, cache_swa_v, cache_mla_ckv, cache_mla_kpe, cache_diff_k, cache_diff_v, c, c_ctx, w_mod, b_mod, g_pre, g_post, w_in, w_out, nat_rpb, swa_sink, mla_g_q, mla_g_kv, mla_w_uq, mla_w_ukv, diff_lambda, diff_g_subln):
    depth = w_mod.shape[0]
    bc, bl = x_prompt.shape[0], x_sample.shape[0]
    n_ctx, n_lat = bc * SEQ, bl * DEC_SEQ
    n_tok = n_ctx + n_lat
    assert n_ctx % DEC_SEQ == 0 and x_prompt.shape[1] == SEQ and x_sample.shape[1] == DEC_SEQ
    ctx = Group(0, n_ctx, lambda i: 0)
    lat = Group(n_ctx, n_lat, lambda i: 1 + i // (DEC_SEQ // ROW_BLOCK))

    n_cond = -(-(1 + bl) // 8) * 8
    cond = jnp.concatenate([c_ctx[None], c, jnp.zeros((n_cond - 1 - bl, D_MODEL), F32)], axis=0)
    mod = _modulation(cond, w_mod, b_mod).reshape(depth, n_cond, 3, D_MODEL)

    w_t = jnp.swapaxes(w_in, 1, 2)
    w_p = _cast_rows(w_t, B_START, N_B, 448, 0, N_PROJ, None, "pack_w_b")
    w_p = _cast_rows(w_t, 0, N_A, 256, N_B, N_PROJ, w_p, "pack_w_a")
    w_out_b = w_out.astype(BF16)
    w_uq_p = _pack_w_uq(mla_w_uq)
    w_ukv_b = mla_w_ukv.astype(BF16)
    g_pre3 = g_pre.reshape(depth, 1, D_MODEL)
    g_post3 = g_post.reshape(depth, 1, D_MODEL)
    g_q3 = mla_g_q.reshape(depth, 1, MLA_Q_LORA)
    g_kv3 = mla_g_kv.reshape(depth, 1, MLA_KV_LORA)
    gsub3 = diff_g_subln.reshape(depth, 1, 2 * HEAD_DIM)
    rope128 = _rope_tables(HEAD_DIM)
    rope64 = _rope_tables(MLA_ROPE)
    nat_bias = _nat_bias(nat_rpb)
    swa_mask = _swa_mask()

    caches = (cache_nat_k.reshape(bl, depth, PAST_LEN * NAT_HEADS, HEAD_DIM),
              cache_nat_v.reshape(bl, depth, PAST_LEN * NAT_HEADS, HEAD_DIM),
              cache_swa_k.reshape(bl, depth, PAST_LEN, -1), cache_swa_v.reshape(bl, depth, PAST_LEN, -1),
              jnp.pad(cache_mla_kpe, ((0, 0), (0, 0), (0, 0), (0, HEAD_DIM - MLA_ROPE))),
              cache_diff_k.reshape(bl, depth, PAST_LEN, -1), cache_diff_v.reshape(bl, depth, PAST_LEN, -1))

    x_c = x_prompt.reshape(n_ctx, D_MODEL)
    x_l = x_sample.reshape(n_lat, D_MODEL)
    h = _norm_mod(x_c, g_pre3, mod, 0, ctx, n_tok, None)
    h = _norm_mod(x_l, g_pre3, mod, 0, lat, n_tok, h)
    states = ()
    for l in range(depth):
        lam_init = 0.8 - 0.6 * math.exp(-0.3 * l)
        p = _matmul_nt(h, w_p, l, 768, "in_proj")
        pa = pb = p
        ckvn = _rms_cols(p, A_CKV, 128, MLA_KV_LORA, g_kv3, l, F32)
        q_up = _mla_q_up(p, g_q3, w_uq_p, l)
        ckv_all = jnp.concatenate([ckvn, cache_mla_ckv[:, l].reshape(bl * PAST_LEN, MLA_KV_LORA)], axis=0)
        kv_up = _matmul(ckv_all.astype(BF16), w_ukv_b, l, 2048, "mla_kv_up", out_dtype=BF16)

        mixed, states = _ctx_attention(pa, pb, q_up, kv_up, ckvn, swa_sink, diff_lambda, gsub3, l, depth,
                                       lam_init, ctx, n_tok, states)
        mixed = _lat_attention(mixed, pa, pb, q_up, kv_up, caches, l, nat_bias, swa_mask, swa_sink,
                               diff_lambda, gsub3, lam_init, rope128, rope64, lat, n_tok)
        o = _matmul(mixed, w_out_b, l, 1024, "out_proj", out_dtype=BF16)
        if l + 1 < depth:
            x_c, h = _post_norm(x_c, o, g_post3, g_pre3, mod, l, ctx, n_tok, None)
            x_l, h = _post_norm(x_l, o, g_post3, g_pre3, mod, l, lat, n_tok, h)
        else:
            x_c = _post(x_c, o, g_post3, mod, l, ctx)
            x_l = _post(x_l, o, g_post3, mod, l, lat)

    return (x_c.reshape(bc, SEQ, D_MODEL), x_l.reshape(bl, DEC_SEQ, D_MODEL)) + tuple(states)
```

```python
import functools
import math
from typing import Callable, NamedTuple

import jax
import jax.numpy as jnp
from jax import lax
from jax.experimental import pallas as pl
from jax.experimental.pallas import tpu as pltpu

F32 = jnp.float32
BF16 = jnp.bfloat16

D_MODEL = 4096
HEAD_DIM = 128
GRID_W = 64
SEQ = 256
DEC_SEQ = 1024
PAST_LEN = 512
NAT_KR, NAT_KC = 8, 16
NAT_HEADS = 8
SWA_WINDOW = 128
MLA_NOPE, MLA_ROPE, MLA_V = 128, 64, 128
MLA_Q_LORA, MLA_KV_LORA = 896, 256
ROPE_BASE = 10000.0
EPS = 1e-6
NEG_INF = -1e30
LOG2E = 1.4426950408889634
QSCALE = HEAD_DIM ** -0.5 * LOG2E
QSCALE_MLA = (MLA_NOPE + MLA_ROPE) ** -0.5 * LOG2E

B_START = 5824
N_B = 7168
N_A = 5888
N_PROJ = N_B + N_A
B_DQ, B_DK, B_DV, B_GP = 0, 1024, 2048, 3072
A_NQ, A_NK, A_NV = N_B + 0, N_B + 1024, N_B + 2048
A_SWQ, A_SWK, A_SWV = N_B + 3072, N_B + 4096, N_B + 4352
A_CQ, A_CKV, A_KPE = N_B + 4608, N_B + 5504, N_B + 5760

ROW_BLOCK = 256
TQ_WIN = 128
TQ_DENSE = 256
NAT_SPAN = 10
VMEM_LIMIT = 56 * 1024 * 1024


class Group(NamedTuple):
    row0: int
    rows: int
    mod_row: Callable


def _cparams(n_axes):
    return pltpu.CompilerParams(dimension_semantics=("arbitrary",) * n_axes,
                                vmem_limit_bytes=VMEM_LIMIT)


def _silu(x):
    return x * (1.0 / (1.0 + jnp.exp(-x)))


def _nt(a, b):
    return lax.dot_general(a, b, (((1,), (1,)), ((), ())), preferred_element_type=F32)


def _softmax_pv(parts, extra=None):
    m = parts[0][0].max(axis=-1, keepdims=True)
    for s, _ in parts[1:]:
        m = jnp.maximum(m, s.max(axis=-1, keepdims=True))
    if extra is not None:
        m = jnp.maximum(m, extra)
    den = None
    acc = None
    for s, v in parts:
        p = jnp.exp2(s - m)
        ps = p.sum(axis=-1, keepdims=True)
        pv = jnp.dot(p.astype(BF16), v, preferred_element_type=F32)
        den = ps if den is None else den + ps
        acc = pv if acc is None else acc + pv
    if extra is not None:
        den = den + jnp.exp2(extra - m)
    return acc / den


def _rope(x, cos, sin_lo, sin_hi, half):
    lanes = x.shape[-1]
    return (x * cos + pltpu.roll(x, lanes - half, 1) * sin_lo + pltpu.roll(x, half, 1) * sin_hi)


def _rms(x, g):
    return x * lax.rsqrt(jnp.mean(x * x, axis=-1, keepdims=True) + EPS) * g


def _aliased_call(body, n_in, alias_args, **kw):
    n_alias = len(alias_args)
    n_out = len(kw["out_shape"]) if isinstance(kw["out_shape"], (tuple, list)) else 1
    if n_alias == 0:
        return pl.pallas_call(body, **kw)

    def with_aliases(*refs):
        body(*refs[:n_in], *refs[n_in + n_alias:])

    kw["in_specs"] = list(kw["in_specs"]) + [pl.BlockSpec(memory_space=pl.ANY)] * n_alias
    kw["input_output_aliases"] = {n_in + k: n_out - n_alias + k for k in range(n_alias)}
    return pl.pallas_call(with_aliases, **kw)


def _mod_kernel(c_ref, w_ref, b_ref, o_ref):
    s = _silu(c_ref[...]).astype(BF16)
    o_ref[...] = jnp.dot(s, w_ref[...].astype(BF16), preferred_element_type=F32) + b_ref[...]


def _modulation(cond, w_mod, b_mod):
    depth = w_mod.shape[0]
    rows = cond.shape[0]
    tn = 512
    n = 3 * D_MODEL
    return pl.pallas_call(
        _mod_kernel,
        grid=(depth, n // tn),
        in_specs=[
            pl.BlockSpec((rows, D_MODEL), lambda l, j: (0, 0)),
            pl.BlockSpec((None, D_MODEL, tn), lambda l, j: (l, 0, j)),
            pl.BlockSpec((None, 1, tn), lambda l, j: (l, 0, j)),
        ],
        out_specs=pl.BlockSpec((None, rows, tn), lambda l, j: (l, 0, j)),
        out_shape=jax.ShapeDtypeStruct((depth, rows, n), F32),
        compiler_params=_cparams(2),
        name="modulation",
    )(cond, w_mod, b_mod.reshape(depth, 1, n))


def _mod_spec(grp, layer):
    return pl.BlockSpec((None, None, 3, D_MODEL), lambda i: (layer, grp.mod_row(i), 0, 0))


def _vec_spec(layer):
    return pl.BlockSpec((None, 1, D_MODEL), lambda i: (layer, 0, 0))


def _tok_spec(grp):
    return pl.BlockSpec((ROW_BLOCK, D_MODEL), lambda i: (grp.row0 // ROW_BLOCK + i, 0))


def _own_spec():
    return pl.BlockSpec((ROW_BLOCK, D_MODEL), lambda i: (i, 0))


def _norm_mod_kernel(x_ref, g_ref, m_ref, h_ref):
    y = _rms(x_ref[...], g_ref[...])
    h_ref[...] = (y * (1.0 + m_ref[1:2, :]) + m_ref[0:1, :]).astype(BF16)


def _norm_mod(x, g_pre, mod, layer, grp, n_tok, h_prev):
    return _aliased_call(
        _norm_mod_kernel, 3, [] if h_prev is None else [h_prev],
        grid=(grp.rows // ROW_BLOCK,),
        in_specs=[_own_spec(), _vec_spec(layer), _mod_spec(grp, layer)],
        out_specs=_tok_spec(grp),
        out_shape=jax.ShapeDtypeStruct((n_tok, D_MODEL), BF16),
        compiler_params=_cparams(1),
        name="norm_mod",
    )(x, g_pre, mod, *([] if h_prev is None else [h_prev]))


def _post_kernel(x_ref, o_ref, g_ref, m_ref, y_ref):
    y_ref[...] = x_ref[...] + m_ref[2:3, :] * _rms(o_ref[...].astype(F32), g_ref[...])


def _post(x, o, g_post, mod, layer, grp):
    return pl.pallas_call(
        _post_kernel,
        grid=(grp.rows // ROW_BLOCK,),
        in_specs=[_own_spec(), _tok_spec(grp), _vec_spec(layer), _mod_spec(grp, layer)],
        out_specs=_own_spec(),
        out_shape=jax.ShapeDtypeStruct((grp.rows, D_MODEL), F32),
        compiler_params=_cparams(1),
        name="post_residual",
    )(x, o, g_post, mod)


def _post_norm_kernel(x_ref, o_ref, g_ref, m_ref, gn_ref, mn_ref, y_ref, h_ref):
    y = x_ref[...] + m_ref[2:3, :] * _rms(o_ref[...].astype(F32), g_ref[...])
    y_ref[...] = y
    h_ref[...] = (_rms(y, gn_ref[...]) * (1.0 + mn_ref[1:2, :]) + mn_ref[0:1, :]).astype(BF16)


def _post_norm(x, o, g_post, g_pre, mod, layer, grp, n_tok, h_prev):
    return _aliased_call(
        _post_norm_kernel, 6, [] if h_prev is None else [h_prev],
        grid=(grp.rows // ROW_BLOCK,),
        in_specs=[_own_spec(), _tok_spec(grp), _vec_spec(layer), _mod_spec(grp, layer),
                  _vec_spec(layer + 1), _mod_spec(grp, layer + 1)],
        out_specs=[_own_spec(), _tok_spec(grp)],
        out_shape=[jax.ShapeDtypeStruct((grp.rows, D_MODEL), F32),
                   jax.ShapeDtypeStruct((n_tok, D_MODEL), BF16)],
        compiler_params=_cparams(1),
        name="post_norm",
    )(x, o, g_post, mod, g_pre, mod, *([] if h_prev is None else [h_prev]))


def _q_up_kernel(a_ref, b_ref, g_ref, w_ref, o_ref):
    cq = jnp.concatenate([a_ref[...], b_ref[:, :MLA_Q_LORA - 512]], axis=-1)
    o_ref[...] = jnp.dot(_rms(cq, g_ref[...]).astype(BF16), w_ref[...], preferred_element_type=F32)


def _mla_q_up(p, g, w_uq, layer):
    m = p.shape[0]
    tm = 512
    n = w_uq.shape[-1]
    first = A_CQ // 512
    return pl.pallas_call(
        _q_up_kernel,
        grid=(m // tm,),
        in_specs=[
            pl.BlockSpec((tm, 512), lambda i: (i, first)),
            pl.BlockSpec((tm, 512), lambda i: (i, first + 1)),
            pl.BlockSpec((None, 1, MLA_Q_LORA), lambda i: (layer, 0, 0)),
            pl.BlockSpec((None, MLA_Q_LORA, n), lambda i: (layer, 0, 0)),
        ],
        out_specs=pl.BlockSpec((tm, n), lambda i: (i, 0)),
        out_shape=jax.ShapeDtypeStruct((m, n), F32),
        compiler_params=_cparams(1),
        name="mla_q_up",
    )(p, p, g, w_uq)


def _kv_up_tokens_kernel(a_ref, b_ref, g_ref, w_ref, ckvn_ref, kv_ref):
    ckvn = _rms(jnp.concatenate([a_ref[...], b_ref[...]], axis=-1), g_ref[...])
    ckvn_ref[...] = ckvn
    kv_ref[...] = jnp.dot(ckvn.astype(BF16), w_ref[...], preferred_element_type=F32).astype(BF16)


def _mla_kv_up_tokens(p, g, w_ukv, layer):
    m = p.shape[0]
    tm = 1024
    n = w_ukv.shape[-1]
    first = A_CKV // HEAD_DIM
    return pl.pallas_call(
        _kv_up_tokens_kernel,
        grid=(m // tm,),
        in_specs=[
            pl.BlockSpec((tm, HEAD_DIM), lambda i: (i, first)),
            pl.BlockSpec((tm, HEAD_DIM), lambda i: (i, first + 1)),
            pl.BlockSpec((None, 1, MLA_KV_LORA), lambda i: (layer, 0, 0)),
            pl.BlockSpec((None, MLA_KV_LORA, n), lambda i: (layer, 0, 0)),
        ],
        out_specs=[pl.BlockSpec((tm, MLA_KV_LORA), lambda i: (i, 0)), pl.BlockSpec((tm, n), lambda i: (i, 0))],
        out_shape=[jax.ShapeDtypeStruct((m, MLA_KV_LORA), F32), jax.ShapeDtypeStruct((m, n), BF16)],
        compiler_params=_cparams(1),
        name="mla_kv_up",
    )(p, p, g, w_ukv)


def _kv_up_cache_kernel(c_ref, w_ref, kv_ref):
    kv_ref[...] = jnp.dot(c_ref[...].astype(BF16), w_ref[...], preferred_element_type=F32).astype(BF16)


def _mla_kv_up_cache(cache_ckv, w_ukv, layer):
    bl = cache_ckv.shape[0]
    n = w_ukv.shape[-1]
    return pl.pallas_call(
        _kv_up_cache_kernel,
        grid=(bl,),
        in_specs=[
            pl.BlockSpec((None, None, PAST_LEN, MLA_KV_LORA), lambda b: (b, layer, 0, 0)),
            pl.BlockSpec((None, MLA_KV_LORA, n), lambda b: (layer, 0, 0)),
        ],
        out_specs=pl.BlockSpec((PAST_LEN, n), lambda b: (b, 0)),
        out_shape=jax.ShapeDtypeStruct((bl * PAST_LEN, n), BF16),
        compiler_params=_cparams(1),
        name="mla_kv_up_cache",
    )(cache_ckv, w_ukv)


def _mm_kernel(a_ref, b_ref, o_ref):
    o_ref[...] = jnp.dot(a_ref[...], b_ref[...], preferred_element_type=F32).astype(o_ref.dtype)


def _matmul(a, w, layer, tn, name, out_dtype=F32):
    m, k = a.shape
    n = w.shape[-1]
    tm = 1024 if m % 1024 == 0 else 512
    return pl.pallas_call(
        _mm_kernel,
        grid=(m // tm, n // tn),
        in_specs=[
            pl.BlockSpec((tm, k), lambda i, j: (i, 0)),
            pl.BlockSpec((None, k, tn), lambda i, j: (layer, 0, j)),
        ],
        out_specs=pl.BlockSpec((tm, tn), lambda i, j: (i, j)),
        out_shape=jax.ShapeDtypeStruct((m, n), out_dtype),
        compiler_params=_cparams(2),
        name=name,
    )(a, w)


def _cast_kernel(w_ref, o_ref):
    o_ref[...] = w_ref[...].astype(BF16)


def _cast_rows(w_t, row0, rows, block_rows, out_row0, out_rows, prev, name):
    depth, _, k = w_t.shape
    first, out_first = row0 // block_rows, out_row0 // block_rows
    assert row0 % block_rows == 0 and rows % block_rows == 0 and out_row0 % block_rows == 0
    return _aliased_call(
        _cast_kernel, 1, [] if prev is None else [prev],
        grid=(depth, rows // block_rows),
        in_specs=[pl.BlockSpec((None, block_rows, k), lambda l, j: (l, first + j, 0))],
        out_specs=pl.BlockSpec((None, block_rows, k), lambda l, j: (l, out_first + j, 0)),
        out_shape=jax.ShapeDtypeStruct((depth, out_rows, k), BF16),
        compiler_params=_cparams(2),
        name=name,
    )(w_t, *([] if prev is None else [prev]))


def _mm_nt_kernel(a_ref, b_ref, o_ref):
    o_ref[...] = _nt(a_ref[...], b_ref[...]).astype(o_ref.dtype)


def _matmul_nt(a, w_t, layer, tn, name):
    m, k = a.shape
    n = w_t.shape[1]
    tm = 1024 if m % 1024 == 0 else 512
    return pl.pallas_call(
        _mm_nt_kernel,
        grid=(m // tm, n // tn),
        in_specs=[
            pl.BlockSpec((tm, k), lambda i, j: (i, 0)),
            pl.BlockSpec((None, tn, k), lambda i, j: (layer, j, 0)),
        ],
        out_specs=pl.BlockSpec((tm, tn), lambda i, j: (i, j)),
        out_shape=jax.ShapeDtypeStruct((m, n), F32),
        compiler_params=_cparams(2),
        name=name,
    )(a, w_t)


def _gated(o, gp):
    return (o * _silu(gp)).astype(BF16)


def _pipelined(n_steps, scores, finish, ahead=1):
    queue = [scores(n) for n in range(min(ahead, n_steps))]
    for n in range(n_steps):
        cur = queue.pop(0)
        if n + ahead < n_steps:
            queue.append(scores(n + ahead))
        finish(n, cur)


def _head(h, w=HEAD_DIM):
    return slice(h * w, (h + 1) * w)


def _ctx_a_kernel(q_ref, k_ref, v_ref, gp_ref, o_ref):
    def scores(h):
        q = (q_ref[:, _head(h)] * QSCALE).astype(BF16)
        return [(_nt(q, k_ref[:, _head(h)].astype(BF16)), v_ref[:, _head(h)].astype(BF16))]

    def finish(h, parts):
        o_ref[:, _head(h)] = _gated(_softmax_pv(parts), gp_ref[:, _head(h)])

    _pipelined(8, scores, finish)


def _ctx_b_kernel(q_ref, k_ref, v_ref, sink_ref, gp_ref, o_ref, *, layer):
    def scores(h):
        q = (q_ref[:, _head(h)] * QSCALE).astype(BF16)
        return [(_nt(q, k_ref[:, _head(h // 4)].astype(BF16)), v_ref[:, _head(h // 4)].astype(BF16))]

    def finish(h, parts):
        o = _softmax_pv(parts, extra=sink_ref[layer, h] * LOG2E)
        o_ref[:, _head(h)] = _gated(o, gp_ref[:, _head(h)])

    _pipelined(8, scores, finish)


def _ctx_c_kernel(qn_ref, qp_ref, kv_ref, kpe_ref, gp_ref, o_ref):
    kpe = kpe_ref[...].astype(BF16)

    def scores(h):
        kn = kv_ref[:, _head(2 * h)].astype(BF16)
        q = (jnp.concatenate([qn_ref[:, _head(h)], qp_ref[:, _head(h)]], axis=-1) * QSCALE_MLA).astype(BF16)
        return [(_nt(q, jnp.concatenate([kn, kpe], axis=-1)), kv_ref[:, _head(2 * h + 1)].astype(BF16))]

    def finish(h, parts):
        o_ref[:, _head(h)] = _gated(_softmax_pv(parts), gp_ref[:, _head(h)])

    _pipelined(8, scores, finish)


def _diff_lambda(lam_ref, lam_init):
    lp = lam_ref[...]
    a = jnp.sum(lp[0:1, :] * lp[1:2, :], axis=-1, keepdims=True)
    b = jnp.sum(lp[2:3, :] * lp[3:4, :], axis=-1, keepdims=True)
    return jnp.exp(a) - jnp.exp(b) + lam_init


def _diff_finish(o1, o2, lam, gsub, lam_init):
    return _rms(o1 - lam * o2, gsub) * (1.0 - lam_init)


def _ctx_d_kernel(q_ref, k_ref, v_ref, lam_ref, gsub_ref, gp_ref, o_ref, *, lam_init):
    lam = _diff_lambda(lam_ref, lam_init)
    gsub = gsub_ref[...]

    first_map = {}

    def scores(t):
        q = (q_ref[:, _head(t)] * QSCALE).astype(BF16)
        return [(_nt(q, k_ref[:, _head(t)].astype(BF16)), v_ref[:, _head(t // 2, 2 * HEAD_DIM)].astype(BF16))]

    def finish(t, parts):
        o = _softmax_pv(parts)
        if t % 2 == 0:
            first_map[t // 2] = o
            return
        sl = _head(t // 2, 2 * HEAD_DIM)
        o = _diff_finish(first_map.pop(t // 2), o, lam, gsub, lam_init)
        o_ref[:, sl] = _gated(o, gp_ref[:, sl])

    _pipelined(8, scores, finish)


STATE_TAILS = ((8, HEAD_DIM), (8, HEAD_DIM), (2, HEAD_DIM), (2, HEAD_DIM), (MLA_KV_LORA,), (MLA_ROPE,),
               (4, 2 * HEAD_DIM), (4, 2 * HEAD_DIM))


def _ctx_attention(pa, pb, q_up, kv_up, ckvn, sink, lam, gsub, layer, depth, lam_init, ctx, n_tok, states):
    nb = ctx.rows // SEQ
    wide = 8 * HEAD_DIM

    def blk(col, w=wide):
        return pl.BlockSpec((SEQ, w), lambda b: (b, col // w))

    def small(shape):
        return pl.BlockSpec((None,) + shape, lambda b: (layer,) + (0,) * len(shape))

    def state(*tail):
        return pl.BlockSpec((None, None, SEQ) + tail, lambda b: (b, layer, 0) + (0,) * len(tail))

    def body(aq, ak, av, bq, bk, bv, sink_ref, cqn, cqp, ckv, ckpe, dq, dk, dv, lam_ref, gsub_ref,
             gp_a, gp_b, gp_c, gp_d, ckvn_ref, o_ref, *state_refs):
        def out(mixer):
            return o_ref.at[:, mixer * wide:(mixer + 1) * wide]

        _ctx_a_kernel(aq, ak, av, gp_a, out(0))
        _ctx_b_kernel(bq, bk, bv, sink_ref, gp_b, out(1), layer=layer)
        _ctx_c_kernel(cqn, cqp, ckv, ckpe, gp_c, out(2))
        _ctx_d_kernel(dq, dk, dv, lam_ref, gsub_ref, gp_d, out(3), lam_init=lam_init)
        _state_kernel(ak, av, bk, bv, ckvn_ref, ckpe, dk, dv, *state_refs)

    outs = _aliased_call(
        body, 21, list(states), grid=(nb,),
        in_specs=[blk(A_NQ), blk(A_NK), blk(A_NV),
                  blk(A_SWQ), blk(A_SWK, 256), blk(A_SWV, 256), pl.BlockSpec(memory_space=pltpu.SMEM),
                  blk(0), blk(wide), blk(0, 2 * wide), blk(A_KPE, HEAD_DIM),
                  blk(B_DQ), blk(B_DK), blk(B_DV), small((4, HEAD_DIM)), small((1, 2 * HEAD_DIM))]
                 + [blk(B_GP + mixer * wide) for mixer in range(4)]
                 + [pl.BlockSpec((SEQ, MLA_KV_LORA), lambda b: (b, 0))],
        out_specs=[pl.BlockSpec((SEQ, 4 * wide), lambda b: (b, 0))] + [state(*t) for t in STATE_TAILS],
        out_shape=[jax.ShapeDtypeStruct((n_tok, D_MODEL), BF16)]
                  + [jax.ShapeDtypeStruct((nb, depth, SEQ) + t, F32) for t in STATE_TAILS],
        compiler_params=_cparams(1), name="ctx_mixers",
    )(pa, pa, pa, pa, pa, pa, sink, q_up, q_up, kv_up, pa, pb, pb, pb, lam, gsub, pb, pb, pb, pb, ckvn,
      *states)
    return outs[0], tuple(outs[1:])


def _nat_win_start(n):
    n_rows = DEC_SEQ // GRID_W
    return min(max(2 * n - NAT_KR // 2, 0), n_rows - NAT_SPAN)


def _nat_pattern_key(n):
    n_rows = DEC_SEQ // GRID_W
    q_per = TQ_WIN // GRID_W
    s = _nat_win_start(n)
    return tuple((q_per * n + j - s, min(max(q_per * n + j - NAT_KR // 2, 0), n_rows - NAT_KR) - s)
                 for j in range(q_per))


NAT_PATTERN_STEPS = tuple(sorted({_nat_pattern_key(n): n for n in reversed(range(DEC_SEQ // TQ_WIN))}.values()))
NAT_PATTERN = tuple([_nat_pattern_key(m) for m in NAT_PATTERN_STEPS].index(_nat_pattern_key(n))
                    for n in range(DEC_SEQ // TQ_WIN))


def _lat_a_kernel(q_ref, k_ref, v_ref, ck_ref, cv_ref, bias_ref, gp_ref, o_ref):
    head_rows = pl.ds(pl.program_id(1), PAST_LEN, stride=NAT_HEADS)
    k = k_ref[...].astype(BF16)
    v = v_ref[...].astype(BF16)
    ck = ck_ref[head_rows, :].astype(BF16)
    cv = cv_ref[head_rows, :].astype(BF16)

    def scores(n):
        win = slice(_nat_win_start(n) * GRID_W, (_nat_win_start(n) + NAT_SPAN) * GRID_W)
        q = (q_ref[_head(n, TQ_WIN), :] * QSCALE).astype(BF16)
        return [(_nt(q, ck), cv), (_nt(q, k[win]) + bias_ref[NAT_PATTERN[n]], v[win])]

    def finish(n, parts):
        rows = _head(n, TQ_WIN)
        o_ref[rows, :] = _gated(_softmax_pv(parts), gp_ref[rows, :])

    _pipelined(DEC_SEQ // TQ_WIN, scores, finish, ahead=2)


def _swa_win_start(n):
    return min(max(n - 1, 0), DEC_SEQ // TQ_WIN - 3) * TQ_WIN


def _lat_b_kernel(q_ref, k_ref, v_ref, ck_ref, cv_ref, sink_ref, mask_ref, cos_ref, sl_ref, sh_ref,
                  gp_ref, o_ref, *, layer):
    k = _rope(k_ref[...], cos_ref[...], sl_ref[...], sh_ref[...], 32).astype(BF16)
    v = v_ref[...].astype(BF16)
    ck = ck_ref[...].astype(BF16)
    cv = cv_ref[...].astype(BF16)
    sink = sink_ref[layer, pl.program_id(1)] * LOG2E

    def scores(n):
        rows = _head(n, TQ_WIN)
        w0 = _swa_win_start(n)
        win = slice(w0, w0 + 3 * TQ_WIN)
        q = _rope(q_ref[rows, :], cos_ref[rows, :], sl_ref[rows, :], sh_ref[rows, :], 32)
        q = (q * QSCALE).astype(BF16)
        return [(_nt(q, ck), cv), (_nt(q, k[win]) + mask_ref[n - w0 // TQ_WIN], v[win])]

    def finish(n, parts):
        rows = _head(n, TQ_WIN)
        o_ref[rows, :] = _gated(_softmax_pv(parts, extra=sink), gp_ref[rows, :])

    _pipelined(DEC_SEQ // TQ_WIN, scores, finish, ahead=2)


def _lat_c_kernel(qn_ref, qp_ref, kn_ref, v_ref, kpe_ref, ckn_ref, cv_ref, ckpe_ref,
                  cos_ref, sl_ref, sh_ref, gp_ref, o_ref):
    kpe = _rope(kpe_ref[...], cos_ref[...], sl_ref[...], sh_ref[...], 16)
    keys = jnp.concatenate([
        jnp.concatenate([ckn_ref[...], ckpe_ref[...].astype(BF16)], axis=-1),
        jnp.concatenate([kn_ref[...], kpe.astype(BF16)], axis=-1)], axis=0)
    vals = jnp.concatenate([cv_ref[...], v_ref[...]], axis=0)

    def scores(n):
        rows = _head(n, TQ_DENSE)
        qp = _rope(qp_ref[rows, :], cos_ref[rows, :], sl_ref[rows, :], sh_ref[rows, :], 16)
        q = (jnp.concatenate([qn_ref[rows, :], qp], axis=-1) * QSCALE_MLA).astype(BF16)
        return [(_nt(q, keys), vals)]

    def finish(n, parts):
        rows = _head(n, TQ_DENSE)
        o_ref[rows, :] = _gated(_softmax_pv(parts), gp_ref[rows, :])

    _pipelined(DEC_SEQ // TQ_DENSE, scores, finish, ahead=2)


def _lat_d_kernel(q1_ref, q2_ref, k1_ref, k2_ref, v_ref, ck1_ref, ck2_ref, cv_ref,
                  lam_ref, gsub_ref, cos_ref, sl_ref, sh_ref, gp_ref, o_ref, *, lam_init):
    cos, sl, sh = cos_ref[...], sl_ref[...], sh_ref[...]
    keys1 = jnp.concatenate([ck1_ref[...], _rope(k1_ref[...], cos, sl, sh, 32)], axis=0).astype(BF16)
    keys2 = jnp.concatenate([ck2_ref[...], _rope(k2_ref[...], cos, sl, sh, 32)], axis=0).astype(BF16)
    vals = jnp.concatenate([cv_ref[...], v_ref[...]], axis=0).astype(BF16)
    lam = _diff_lambda(lam_ref, lam_init)
    gsub = gsub_ref[...]
    first_map = {}

    def scores(t):
        rows = _head(t // 2, TQ_DENSE)
        q_ref, keys = ((q1_ref, keys1), (q2_ref, keys2))[t % 2]
        q = _rope(q_ref[rows, :], cos_ref[rows, :], sl_ref[rows, :], sh_ref[rows, :], 32)
        return [(_nt((q * QSCALE).astype(BF16), keys), vals)]

    def finish(t, parts):
        o = _softmax_pv(parts)
        if t % 2 == 0:
            first_map[t // 2] = o
            return
        rows = _head(t // 2, TQ_DENSE)
        o = _diff_finish(first_map.pop(t // 2), o, lam, gsub, lam_init)
        o_ref[rows, :] = _gated(o, gp_ref[rows, :])

    _pipelined(2 * (DEC_SEQ // TQ_DENSE), scores, finish)


def _lat_attention(mixed, pa, pb, q_up, kv_up, kv_cache, caches, layer, nat_bias, swa_mask, sink, lam,
                   gsub, lam_init, rope128, rope64, lat, n_tok):
    nb = lat.rows // DEC_SEQ
    rb0 = lat.row0 // DEC_SEQ
    c_nat_k, c_nat_v, c_swa_k, c_swa_v, c_kpe, c_dk, c_dv = caches
    hd = HEAD_DIM

    def rows(col_of, w=hd):
        return pl.BlockSpec((DEC_SEQ, w), lambda b, h: (rb0 + b, col_of(h)))

    def cache(col_of, w=hd):
        return pl.BlockSpec((None, None, PAST_LEN, w), lambda b, h: (b, layer, 0, col_of(h)))

    def table():
        return pl.BlockSpec((DEC_SEQ, hd), lambda b, h: (0, 0))

    def small(shape):
        return pl.BlockSpec((None,) + shape, lambda b, h: (layer,) + (0,) * len(shape))

    all_heads = pl.BlockSpec((None, None, PAST_LEN * NAT_HEADS, hd), lambda b, h: (b, layer, 0, 0))

    def call(body, n_in, heads, in_specs, args, mixed, w, col0, name):
        return _aliased_call(
            body, n_in, [mixed], grid=(nb, heads), in_specs=in_specs,
            out_specs=rows(lambda h: col0 // w + h, w),
            out_shape=jax.ShapeDtypeStruct((n_tok, D_MODEL), BF16),
            compiler_params=_cparams(2), name=name,
        )(*args, mixed)

    mixed = call(
        _lat_a_kernel, 7, 8,
        [rows(lambda h: A_NQ // hd + h), rows(lambda h: A_NK // hd + h), rows(lambda h: A_NV // hd + h),
         all_heads, all_heads,
         pl.BlockSpec((None, None, len(NAT_PATTERN_STEPS), TQ_WIN, NAT_SPAN * GRID_W),
                      lambda b, h: (layer, h, 0, 0, 0)),
         rows(lambda h: B_GP // hd + h)],
        (pa, pa, pa, c_nat_k, c_nat_v, nat_bias, pb), mixed, hd, 0, "lat_nat")
    mixed = call(
        functools.partial(_lat_b_kernel, layer=layer), 11, 8,
        [rows(lambda h: A_SWQ // hd + h), rows(lambda h: A_SWK // hd + h // 4),
         rows(lambda h: A_SWV // hd + h // 4), cache(lambda h: h // 4), cache(lambda h: h // 4),
         pl.BlockSpec(memory_space=pltpu.SMEM),
         pl.BlockSpec((3, TQ_WIN, 3 * TQ_WIN), lambda b, h: (0, 0, 0)), table(), table(), table(),
         rows(lambda h: B_GP // hd + 8 + h)],
        (pa, pa, pa, c_swa_k, c_swa_v, sink, swa_mask, *rope128, pb), mixed, hd, 8 * hd, "lat_swa")
    mixed = call(
        _lat_c_kernel, 12, 8,
        [rows(lambda h: h), rows(lambda h: 8 + h),
         rows(lambda h: 2 * h), rows(lambda h: 2 * h + 1), rows(lambda h: A_KPE // hd),
         pl.BlockSpec((PAST_LEN, hd), lambda b, h: (b, 2 * h)),
         pl.BlockSpec((PAST_LEN, hd), lambda b, h: (b, 2 * h + 1)),
         cache(lambda h: 0), table(), table(), table(),
         rows(lambda h: B_GP // hd + 16 + h)],
        (q_up, q_up, kv_up, kv_up, pa, kv_cache, kv_cache, c_kpe, *rope64, pb), mixed, hd, 16 * hd,
        "lat_mla")
    mixed = call(
        functools.partial(_lat_d_kernel, lam_init=lam_init), 14, 4,
        [rows(lambda h: B_DQ // hd + 2 * h), rows(lambda h: B_DQ // hd + 2 * h + 1),
         rows(lambda h: B_DK // hd + 2 * h), rows(lambda h: B_DK // hd + 2 * h + 1),
         rows(lambda h: B_DV // (2 * hd) + h, 2 * hd),
         cache(lambda h: 2 * h), cache(lambda h: 2 * h + 1), cache(lambda h: h, 2 * hd),
         small((4, hd)), small((1, 2 * hd)), table(), table(), table(),
         rows(lambda h: B_GP // (2 * hd) + 12 + h, 2 * hd)],
        (pb, pb, pb, pb, pb, c_dk, c_dk, c_dv, lam, gsub, *rope128, pb), mixed, 2 * hd, 24 * hd,
        "lat_diff")
    return mixed


def _state_kernel(nk_ref, nv_ref, sk_ref, sv_ref, ckv_ref, kpe_ref, dk_ref, dv_ref,
                  o_nk, o_nv, o_sk, o_sv, o_ckv, o_kpe, o_dk, o_dv):
    for src, dst, heads, w in ((nk_ref, o_nk, 8, HEAD_DIM), (nv_ref, o_nv, 8, HEAD_DIM),
                               (sk_ref, o_sk, 2, HEAD_DIM), (sv_ref, o_sv, 2, HEAD_DIM),
                               (dk_ref, o_dk, 4, 2 * HEAD_DIM), (dv_ref, o_dv, 4, 2 * HEAD_DIM)):
        for h in range(heads):
            dst[:, h, :] = src[:, h * w:(h + 1) * w]
    o_ckv[...] = ckv_ref[...]
    o_kpe[...] = kpe_ref[:, :MLA_ROPE]


def _rope_tables(rot_dim):
    axis_dim = rot_dim // 2
    inv = 1.0 / (ROPE_BASE ** (jnp.arange(0, axis_dim, 2, dtype=F32) / axis_dim))
    t = jnp.arange(DEC_SEQ)
    ang_row = (t // GRID_W).astype(F32)[:, None] * inv
    ang_col = (t % GRID_W).astype(F32)[:, None] * inv
    zeros = jnp.zeros_like(ang_row)
    pad = HEAD_DIM - rot_dim
    cos = jnp.concatenate([jnp.cos(ang_row)] * 2 + [jnp.cos(ang_col)] * 2
                          + [jnp.ones((DEC_SEQ, pad), F32)], axis=-1)
    sin_lo = jnp.concatenate([-jnp.sin(ang_row), zeros, -jnp.sin(ang_col), zeros,
                              jnp.zeros((DEC_SEQ, pad), F32)], axis=-1)
    sin_hi = jnp.concatenate([zeros, jnp.sin(ang_row), zeros, jnp.sin(ang_col),
                              jnp.zeros((DEC_SEQ, pad), F32)], axis=-1)
    return cos, sin_lo, sin_hi


def _nat_bias(rpb):
    n_rows = DEC_SEQ // GRID_W
    steps = len(NAT_PATTERN_STEPS)
    q_per = TQ_WIN // GRID_W
    rep = jnp.asarray(NAT_PATTERN_STEPS)
    qr = rep[:, None] * q_per + jnp.arange(q_per)[None, :]
    kr = jnp.asarray([_nat_win_start(n) for n in NAT_PATTERN_STEPS])[:, None] + jnp.arange(NAT_SPAN)[None, :]
    c = jnp.arange(GRID_W)
    row0 = jnp.clip(qr - NAT_KR // 2, 0, n_rows - NAT_KR)
    col0 = jnp.clip(c - NAT_KC // 2, 0, GRID_W - NAT_KC)
    ok_r = (kr[:, None, :] >= row0[:, :, None]) & (kr[:, None, :] < row0[:, :, None] + NAT_KR)
    ok_c = (c[None, :] >= col0[:, None]) & (c[None, :] < col0[:, None] + NAT_KC)
    dr = jnp.clip(kr[:, None, :] - qr[:, :, None], -(NAT_KR - 1), NAT_KR - 1) + (NAT_KR - 1)
    dc = jnp.clip(c[None, :] - c[:, None], -(NAT_KC - 1), NAT_KC - 1) + (NAT_KC - 1)
    dr, ok_r = jnp.repeat(dr, GRID_W, axis=-1), jnp.repeat(ok_r, GRID_W, axis=-1)
    dc, ok_c = jnp.tile(dc, (1, NAT_SPAN)), jnp.tile(ok_c, (1, NAT_SPAN))
    sel_r = jax.nn.one_hot(dr, 2 * NAT_KR - 1, dtype=F32)
    sel_c = jax.nn.one_hot(dc, 2 * NAT_KC - 1, dtype=F32)
    bias = jnp.einsum("nqKr,lhrc,xKc->lhnqxK", sel_r, rpb.astype(F32), sel_c,
                      precision=lax.Precision.HIGHEST)
    ok = ok_r[:, :, None, :] & ok_c[None, None, :, :]
    bias = jnp.where(ok[None, None], bias * LOG2E, NEG_INF)
    return bias.reshape(rpb.shape[0], rpb.shape[1], steps, TQ_WIN, NAT_SPAN * GRID_W)


def _swa_mask():
    qi = jnp.arange(TQ_WIN)[None, :, None] + TQ_WIN * jnp.arange(3)[:, None, None]
    kj = jnp.arange(3 * TQ_WIN)[None, None, :]
    return jnp.where(jnp.abs(qi - kj) <= SWA_WINDOW, 0.0, NEG_INF).astype(F32)


def _pack_w_uq(w_uq):
    depth = w_uq.shape[0]
    w = w_uq.reshape(depth, MLA_Q_LORA, 8, MLA_NOPE + MLA_ROPE)
    nope = w[..., :MLA_NOPE].reshape(depth, MLA_Q_LORA, 8 * MLA_NOPE)
    pe = jnp.pad(w[..., MLA_NOPE:], ((0, 0), (0, 0), (0, 0), (0, HEAD_DIM - MLA_ROPE)))
    return jnp.concatenate([nope, pe.reshape(depth, MLA_Q_LORA, 8 * HEAD_DIM)], axis=-1).astype(BF16)


def kernel(x_prompt, x_sample, cache_nat_k, cache_nat_v, cache_swa_k, cache_swa_v, cache_mla_ckv, cache_mla_kpe, cache_diff_k, cache_diff_v, c, c_ctx, w_mod, b_mod, g_pre, g_post, w_in, w_out, nat_rpb, swa_sink, mla_g_q, mla_g_kv, mla_w_uq, mla_w_ukv, diff_lambda, diff_g_subln):
    depth = w_mod.shape[0]
    bc, bl = x_prompt.shape[0], x_sample.shape[0]
    n_ctx, n_lat = bc * SEQ, bl * DEC_SEQ
    n_tok = n_ctx + n_lat
    assert n_ctx % DEC_SEQ == 0 and x_prompt.shape[1] == SEQ and x_sample.shape[1] == DEC_SEQ
    ctx = Group(0, n_ctx, lambda i: 0)
    lat = Group(n_ctx, n_lat, lambda i: 1 + i // (DEC_SEQ // ROW_BLOCK))

    n_cond = -(-(1 + bl) // 8) * 8
    cond = jnp.concatenate([c_ctx[None], c, jnp.zeros((n_cond - 1 - bl, D_MODEL), F32)], axis=0)
    mod = _modulation(cond, w_mod, b_mod).reshape(depth, n_cond, 3, D_MODEL)

    w_t = jnp.swapaxes(w_in, 1, 2)
    w_p = _cast_rows(w_t, B_START, N_B, 448, 0, N_PROJ, None, "pack_w_b")
    w_p = _cast_rows(w_t, 0, N_A, 256, N_B, N_PROJ, w_p, "pack_w_a")
    w_out_b = w_out.astype(BF16)
    w_uq_p = _pack_w_uq(mla_w_uq)
    w_ukv_b = mla_w_ukv.astype(BF16)
    g_pre3 = g_pre.reshape(depth, 1, D_MODEL)
    g_post3 = g_post.reshape(depth, 1, D_MODEL)
    g_q3 = mla_g_q.reshape(depth, 1, MLA_Q_LORA)
    g_kv3 = mla_g_kv.reshape(depth, 1, MLA_KV_LORA)
    gsub3 = diff_g_subln.reshape(depth, 1, 2 * HEAD_DIM)
    rope128 = _rope_tables(HEAD_DIM)
    rope64 = _rope_tables(MLA_ROPE)
    nat_bias = _nat_bias(nat_rpb)
    swa_mask = _swa_mask()

    caches = (cache_nat_k.reshape(bl, depth, PAST_LEN * NAT_HEADS, HEAD_DIM),
              cache_nat_v.reshape(bl, depth, PAST_LEN * NAT_HEADS, HEAD_DIM),
              cache_swa_k.reshape(bl, depth, PAST_LEN, -1), cache_swa_v.reshape(bl, depth, PAST_LEN, -1),
              jnp.pad(cache_mla_kpe, ((0, 0), (0, 0), (0, 0), (0, HEAD_DIM - MLA_ROPE))),
              cache_diff_k.reshape(bl, depth, PAST_LEN, -1), cache_diff_v.reshape(bl, depth, PAST_LEN, -1))

    x_c = x_prompt.reshape(n_ctx, D_MODEL)
    x_l = x_sample.reshape(n_lat, D_MODEL)
    h = _norm_mod(x_c, g_pre3, mod, 0, ctx, n_tok, None)
    h = _norm_mod(x_l, g_pre3, mod, 0, lat, n_tok, h)
    states = ()
    for l in range(depth):
        lam_init = 0.8 - 0.6 * math.exp(-0.3 * l)
        p = _matmul_nt(h, w_p, l, 768, "in_proj")
        pa = pb = p
        q_up = _mla_q_up(p, g_q3, w_uq_p, l)
        ckvn, kv_up = _mla_kv_up_tokens(p, g_kv3, w_ukv_b, l)
        kv_cache = _mla_kv_up_cache(cache_mla_ckv, w_ukv_b, l)

        mixed, states = _ctx_attention(pa, pb, q_up, kv_up, ckvn, swa_sink, diff_lambda, gsub3, l, depth,
                                       lam_init, ctx, n_tok, states)
        mixed = _lat_attention(mixed, pa, pb, q_up, kv_up, kv_cache, caches, l, nat_bias, swa_mask, swa_sink,
                               diff_lambda, gsub3, lam_init, rope128, rope64, lat, n_tok)
        o = _matmul(mixed, w_out_b, l, 1024, "out_proj", out_dtype=BF16)
        if l + 1 < depth:
            x_c, h = _post_norm(x_c, o, g_post3, g_pre3, mod, l, ctx, n_tok, None)
            x_l, h = _post_norm(x_l, o, g_post3, g_pre3, mod, l, lat, n_tok, h)
        else:
            x_c = _post(x_c, o, g_post3, mod, l, ctx)
            x_l = _post(x_l, o, g_post3, mod, l, lat)

    return (x_c.reshape(bc, SEQ, D_MODEL), x_l.reshape(bl, DEC_SEQ, D_MODEL)) + tuple(states)
```

```python
import functools
import math
from typing import Callable, NamedTuple

import jax
import jax.numpy as jnp
from jax import lax
from jax.experimental import pallas as pl
from jax.experimental.pallas import tpu as pltpu

F32 = jnp.float32
BF16 = jnp.bfloat16

D_MODEL = 4096
HEAD_DIM = 128
GRID_W = 64
SEQ = 256
DEC_SEQ = 1024
PAST_LEN = 512
NAT_KR, NAT_KC = 8, 16
NAT_HEADS = 8
SWA_WINDOW = 128
MLA_NOPE, MLA_ROPE = 128, 64
MLA_Q_LORA, MLA_KV_LORA = 896, 256
ROPE_BASE = 10000.0
EPS = 1e-6
NEG_INF = -1e30
LOG2E = 1.4426950408889634
QSCALE = HEAD_DIM ** -0.5 * LOG2E
QSCALE_MLA = (MLA_NOPE + MLA_ROPE) ** -0.5 * LOG2E

B_START = 5824
N_B = 7168
N_A = 5888
N_PROJ = N_B + N_A
B_DQ, B_DK, B_DV, B_GP = 0, 1024, 2048, 3072
A_NQ, A_NK, A_NV = N_B + 0, N_B + 1024, N_B + 2048
A_SWQ, A_SWK, A_SWV = N_B + 3072, N_B + 4096, N_B + 4352
A_CQ, A_CKV, A_KPE = N_B + 4608, N_B + 5504, N_B + 5760

V7X_VMEM_BYTES = 64 * 1024 * 1024
VMEM_LIMIT = V7X_VMEM_BYTES * 7 // 8
ROW_BLOCK = 256
PROJ_TM = 1024
IN_PROJ_TN = 768
OUT_PROJ_TN = 1024
Q_UP_TM = 1024
KV_UP_TM = 1024
PACK_ROWS_B = 448
PACK_ROWS_A = 256
TQ_WIN = 128
TQ_DENSE = 256
NAT_SPAN = 10


class Group(NamedTuple):
    row0: int
    rows: int
    mod_row: Callable


def _cparams(n_axes):
    return pltpu.CompilerParams(dimension_semantics=("arbitrary",) * n_axes,
                                vmem_limit_bytes=VMEM_LIMIT)


def _silu(x):
    return x * (1.0 / (1.0 + jnp.exp(-x)))


def _nt(a, b):
    return lax.dot_general(a, b, (((1,), (1,)), ((), ())), preferred_element_type=F32)


def _softmax_pv(parts, extra=None):
    m = parts[0][0].max(axis=-1, keepdims=True)
    for s, _ in parts[1:]:
        m = jnp.maximum(m, s.max(axis=-1, keepdims=True))
    if extra is not None:
        m = jnp.maximum(m, extra)
    den = None
    acc = None
    for s, v in parts:
        p = jnp.exp2(s - m)
        ps = p.sum(axis=-1, keepdims=True)
        pv = jnp.dot(p.astype(BF16), v, preferred_element_type=F32)
        den = ps if den is None else den + ps
        acc = pv if acc is None else acc + pv
    if extra is not None:
        den = den + jnp.exp2(extra - m)
    return acc / den


def _rope(x, cos, sin_lo, sin_hi, half):
    lanes = x.shape[-1]
    return (x * cos + pltpu.roll(x, lanes - half, 1) * sin_lo + pltpu.roll(x, half, 1) * sin_hi)


def _rms(x, g):
    return x * lax.rsqrt(jnp.mean(x * x, axis=-1, keepdims=True) + EPS) * g


def _aliased_call(body, n_in, alias_args, **kw):
    n_alias = len(alias_args)
    n_out = len(kw["out_shape"]) if isinstance(kw["out_shape"], (tuple, list)) else 1
    if n_alias == 0:
        return pl.pallas_call(body, **kw)

    def with_aliases(*refs):
        body(*refs[:n_in], *refs[n_in + n_alias:])

    kw["in_specs"] = list(kw["in_specs"]) + [pl.BlockSpec(memory_space=pl.ANY)] * n_alias
    kw["input_output_aliases"] = {n_in + k: n_out - n_alias + k for k in range(n_alias)}
    return pl.pallas_call(with_aliases, **kw)


def _mod_kernel(c_ref, w_ref, b_ref, o_ref):
    s = _silu(c_ref[...]).astype(BF16)
    o_ref[...] = jnp.dot(s, w_ref[...].astype(BF16), preferred_element_type=F32) + b_ref[...]


def _modulation(cond, w_mod, b_mod):
    depth = w_mod.shape[0]
    rows = cond.shape[0]
    tn = 512
    n = 3 * D_MODEL
    return pl.pallas_call(
        _mod_kernel,
        grid=(depth, n // tn),
        in_specs=[
            pl.BlockSpec((rows, D_MODEL), lambda l, j: (0, 0)),
            pl.BlockSpec((None, D_MODEL, tn), lambda l, j: (l, 0, j)),
            pl.BlockSpec((None, 1, tn), lambda l, j: (l, 0, j)),
        ],
        out_specs=pl.BlockSpec((None, rows, tn), lambda l, j: (l, 0, j)),
        out_shape=jax.ShapeDtypeStruct((depth, rows, n), F32),
        compiler_params=_cparams(2),
        name="modulation",
    )(cond, w_mod, b_mod.reshape(depth, 1, n))


def _mod_spec(grp, layer):
    return pl.BlockSpec((None, None, 3, D_MODEL), lambda i: (layer, grp.mod_row(i), 0, 0))


def _vec_spec(layer):
    return pl.BlockSpec((None, 1, D_MODEL), lambda i: (layer, 0, 0))


def _tok_spec(grp):
    return pl.BlockSpec((ROW_BLOCK, D_MODEL), lambda i: (grp.row0 // ROW_BLOCK + i, 0))


def _own_spec():
    return pl.BlockSpec((ROW_BLOCK, D_MODEL), lambda i: (i, 0))


def _norm_mod_kernel(x_ref, g_ref, m_ref, h_ref):
    y = _rms(x_ref[...], g_ref[...])
    h_ref[...] = (y * (1.0 + m_ref[1:2, :]) + m_ref[0:1, :]).astype(BF16)


def _norm_mod(x, g_pre, mod, layer, grp, n_tok, h_prev):
    return _aliased_call(
        _norm_mod_kernel, 3, [] if h_prev is None else [h_prev],
        grid=(grp.rows // ROW_BLOCK,),
        in_specs=[_own_spec(), _vec_spec(layer), _mod_spec(grp, layer)],
        out_specs=_tok_spec(grp),
        out_shape=jax.ShapeDtypeStruct((n_tok, D_MODEL), BF16),
        compiler_params=_cparams(1),
        name="norm_mod",
    )(x, g_pre, mod, *([] if h_prev is None else [h_prev]))


def _post_kernel(x_ref, o_ref, g_ref, m_ref, y_ref):
    y_ref[...] = x_ref[...] + m_ref[2:3, :] * _rms(o_ref[...].astype(F32), g_ref[...])


def _post(x, o, g_post, mod, layer, grp):
    return pl.pallas_call(
        _post_kernel,
        grid=(grp.rows // ROW_BLOCK,),
        in_specs=[_own_spec(), _tok_spec(grp), _vec_spec(layer), _mod_spec(grp, layer)],
        out_specs=_own_spec(),
        out_shape=jax.ShapeDtypeStruct((grp.rows, D_MODEL), F32),
        compiler_params=_cparams(1),
        name="post_residual",
    )(x, o, g_post, mod)


def _post_norm_kernel(x_ref, o_ref, g_ref, m_ref, gn_ref, mn_ref, y_ref, h_ref):
    y = x_ref[...] + m_ref[2:3, :] * _rms(o_ref[...].astype(F32), g_ref[...])
    y_ref[...] = y
    h_ref[...] = (_rms(y, gn_ref[...]) * (1.0 + mn_ref[1:2, :]) + mn_ref[0:1, :]).astype(BF16)


def _post_norm(x, o, g_post, g_pre, mod, layer, grp, n_tok, h_prev):
    return _aliased_call(
        _post_norm_kernel, 6, [] if h_prev is None else [h_prev],
        grid=(grp.rows // ROW_BLOCK,),
        in_specs=[_own_spec(), _tok_spec(grp), _vec_spec(layer), _mod_spec(grp, layer),
                  _vec_spec(layer + 1), _mod_spec(grp, layer + 1)],
        out_specs=[_own_spec(), _tok_spec(grp)],
        out_shape=[jax.ShapeDtypeStruct((grp.rows, D_MODEL), F32),
                   jax.ShapeDtypeStruct((n_tok, D_MODEL), BF16)],
        compiler_params=_cparams(1),
        name="post_norm",
    )(x, o, g_post, mod, g_pre, mod, *([] if h_prev is None else [h_prev]))


def _q_up_kernel(a_ref, b_ref, g_ref, w_ref, o_ref):
    cq = jnp.concatenate([a_ref[...], b_ref[:, :MLA_Q_LORA - 512]], axis=-1)
    o_ref[...] = jnp.dot(_rms(cq, g_ref[...]).astype(BF16), w_ref[...], preferred_element_type=F32)


def _mla_q_up(p, g, w_uq, layer):
    m = p.shape[0]
    tm = Q_UP_TM
    n = w_uq.shape[-1]
    first = A_CQ // 512
    return pl.pallas_call(
        _q_up_kernel,
        grid=(m // tm,),
        in_specs=[
            pl.BlockSpec((tm, 512), lambda i: (i, first)),
            pl.BlockSpec((tm, 512), lambda i: (i, first + 1)),
            pl.BlockSpec((None, 1, MLA_Q_LORA), lambda i: (layer, 0, 0)),
            pl.BlockSpec((None, MLA_Q_LORA, n), lambda i: (layer, 0, 0)),
        ],
        out_specs=pl.BlockSpec((tm, n), lambda i: (i, 0)),
        out_shape=jax.ShapeDtypeStruct((m, n), F32),
        compiler_params=_cparams(1),
        name="mla_q_up",
    )(p, p, g, w_uq)


def _kv_up_tokens_kernel(a_ref, b_ref, g_ref, w_ref, ckvn_ref, kv_ref):
    ckvn = _rms(jnp.concatenate([a_ref[...], b_ref[...]], axis=-1), g_ref[...])
    ckvn_ref[...] = ckvn
    kv_ref[...] = jnp.dot(ckvn.astype(BF16), w_ref[...], preferred_element_type=F32).astype(BF16)


def _mla_kv_up_tokens(p, g, w_ukv, layer):
    m = p.shape[0]
    tm = KV_UP_TM
    n = w_ukv.shape[-1]
    first = A_CKV // HEAD_DIM
    return pl.pallas_call(
        _kv_up_tokens_kernel,
        grid=(m // tm,),
        in_specs=[
            pl.BlockSpec((tm, HEAD_DIM), lambda i: (i, first)),
            pl.BlockSpec((tm, HEAD_DIM), lambda i: (i, first + 1)),
            pl.BlockSpec((None, 1, MLA_KV_LORA), lambda i: (layer, 0, 0)),
            pl.BlockSpec((None, MLA_KV_LORA, n), lambda i: (layer, 0, 0)),
        ],
        out_specs=[pl.BlockSpec((tm, MLA_KV_LORA), lambda i: (i, 0)), pl.BlockSpec((tm, n), lambda i: (i, 0))],
        out_shape=[jax.ShapeDtypeStruct((m, MLA_KV_LORA), F32), jax.ShapeDtypeStruct((m, n), BF16)],
        compiler_params=_cparams(1),
        name="mla_kv_up",
    )(p, p, g, w_ukv)


def _kv_up_cache_kernel(c_ref, w_ref, kv_ref):
    kv_ref[...] = jnp.dot(c_ref[...].astype(BF16), w_ref[...], preferred_element_type=F32).astype(BF16)


def _mla_kv_up_cache(cache_ckv, w_ukv, layer):
    bl = cache_ckv.shape[0]
    n = w_ukv.shape[-1]
    return pl.pallas_call(
        _kv_up_cache_kernel,
        grid=(bl,),
        in_specs=[
            pl.BlockSpec((None, None, PAST_LEN, MLA_KV_LORA), lambda b: (b, layer, 0, 0)),
            pl.BlockSpec((None, MLA_KV_LORA, n), lambda b: (layer, 0, 0)),
        ],
        out_specs=pl.BlockSpec((PAST_LEN, n), lambda b: (b, 0)),
        out_shape=jax.ShapeDtypeStruct((bl * PAST_LEN, n), BF16),
        compiler_params=_cparams(1),
        name="mla_kv_up_cache",
    )(cache_ckv, w_ukv)


def _mm_kernel(a_ref, b_ref, o_ref):
    o_ref[...] = jnp.dot(a_ref[...], b_ref[...], preferred_element_type=F32).astype(o_ref.dtype)


def _matmul(a, w, layer, tn, name, out_dtype=F32):
    m, k = a.shape
    n = w.shape[-1]
    tm = PROJ_TM
    assert m % tm == 0 and n % tn == 0
    return pl.pallas_call(
        _mm_kernel,
        grid=(m // tm, n // tn),
        in_specs=[
            pl.BlockSpec((tm, k), lambda i, j: (i, 0)),
            pl.BlockSpec((None, k, tn), lambda i, j: (layer, 0, j)),
        ],
        out_specs=pl.BlockSpec((tm, tn), lambda i, j: (i, j)),
        out_shape=jax.ShapeDtypeStruct((m, n), out_dtype),
        compiler_params=_cparams(2),
        name=name,
    )(a, w)


def _cast_kernel(w_ref, o_ref):
    o_ref[...] = w_ref[...].astype(BF16)


def _cast_rows(w_t, row0, rows, block_rows, out_row0, out_rows, prev, name):
    depth, _, k = w_t.shape
    first, out_first = row0 // block_rows, out_row0 // block_rows
    assert row0 % block_rows == 0 and rows % block_rows == 0 and out_row0 % block_rows == 0
    return _aliased_call(
        _cast_kernel, 1, [] if prev is None else [prev],
        grid=(depth, rows // block_rows),
        in_specs=[pl.BlockSpec((None, block_rows, k), lambda l, j: (l, first + j, 0))],
        out_specs=pl.BlockSpec((None, block_rows, k), lambda l, j: (l, out_first + j, 0)),
        out_shape=jax.ShapeDtypeStruct((depth, out_rows, k), BF16),
        compiler_params=_cparams(2),
        name=name,
    )(w_t, *([] if prev is None else [prev]))


def _mm_nt_kernel(a_ref, b_ref, o_ref):
    o_ref[...] = _nt(a_ref[...], b_ref[...]).astype(o_ref.dtype)


def _matmul_nt(a, w_t, layer, tn, name):
    m, k = a.shape
    n = w_t.shape[1]
    tm = PROJ_TM
    assert m % tm == 0 and n % tn == 0
    return pl.pallas_call(
        _mm_nt_kernel,
        grid=(m // tm, n // tn),
        in_specs=[
            pl.BlockSpec((tm, k), lambda i, j: (i, 0)),
            pl.BlockSpec((None, tn, k), lambda i, j: (layer, j, 0)),
        ],
        out_specs=pl.BlockSpec((tm, tn), lambda i, j: (i, j)),
        out_shape=jax.ShapeDtypeStruct((m, n), F32),
        compiler_params=_cparams(2),
        name=name,
    )(a, w_t)


def _gated(o, gp):
    return (o * _silu(gp)).astype(BF16)


def _pipelined(n_steps, scores, finish, ahead=1):
    queue = [scores(n) for n in range(min(ahead, n_steps))]
    for n in range(n_steps):
        cur = queue.pop(0)
        if n + ahead < n_steps:
            queue.append(scores(n + ahead))
        finish(n, cur)


def _head(h, w=HEAD_DIM):
    return slice(h * w, (h + 1) * w)


def _ctx_a_kernel(q_ref, k_ref, v_ref, gp_ref, o_ref):
    def scores(h):
        q = (q_ref[:, _head(h)] * QSCALE).astype(BF16)
        return [(_nt(q, k_ref[:, _head(h)].astype(BF16)), v_ref[:, _head(h)].astype(BF16))]

    def finish(h, parts):
        o_ref[:, _head(h)] = _gated(_softmax_pv(parts), gp_ref[:, _head(h)])

    _pipelined(8, scores, finish)


def _ctx_b_kernel(q_ref, k_ref, v_ref, sink_ref, gp_ref, o_ref, *, layer):
    def scores(h):
        q = (q_ref[:, _head(h)] * QSCALE).astype(BF16)
        return [(_nt(q, k_ref[:, _head(h // 4)].astype(BF16)), v_ref[:, _head(h // 4)].astype(BF16))]

    def finish(h, parts):
        o = _softmax_pv(parts, extra=sink_ref[layer, h] * LOG2E)
        o_ref[:, _head(h)] = _gated(o, gp_ref[:, _head(h)])

    _pipelined(8, scores, finish)


def _ctx_c_kernel(qn_ref, qp_ref, kv_ref, kpe_ref, gp_ref, o_ref):
    kpe = kpe_ref[...].astype(BF16)

    def scores(h):
        kn = kv_ref[:, _head(2 * h)].astype(BF16)
        q = (jnp.concatenate([qn_ref[:, _head(h)], qp_ref[:, _head(h)]], axis=-1) * QSCALE_MLA).astype(BF16)
        return [(_nt(q, jnp.concatenate([kn, kpe], axis=-1)), kv_ref[:, _head(2 * h + 1)].astype(BF16))]

    def finish(h, parts):
        o_ref[:, _head(h)] = _gated(_softmax_pv(parts), gp_ref[:, _head(h)])

    _pipelined(8, scores, finish)


def _diff_lambda(lam_ref, lam_init):
    lp = lam_ref[...]
    a = jnp.sum(lp[0:1, :] * lp[1:2, :], axis=-1, keepdims=True)
    b = jnp.sum(lp[2:3, :] * lp[3:4, :], axis=-1, keepdims=True)
    return jnp.exp(a) - jnp.exp(b) + lam_init


def _diff_finish(o1, o2, lam, gsub, lam_init):
    return _rms(o1 - lam * o2, gsub) * (1.0 - lam_init)


def _ctx_d_kernel(q_ref, k_ref, v_ref, lam_ref, gsub_ref, gp_ref, o_ref, *, lam_init):
    lam = _diff_lambda(lam_ref, lam_init)
    gsub = gsub_ref[...]

    first_map = {}

    def scores(t):
        q = (q_ref[:, _head(t)] * QSCALE).astype(BF16)
        return [(_nt(q, k_ref[:, _head(t)].astype(BF16)), v_ref[:, _head(t // 2, 2 * HEAD_DIM)].astype(BF16))]

    def finish(t, parts):
        o = _softmax_pv(parts)
        if t % 2 == 0:
            first_map[t // 2] = o
            return
        sl = _head(t // 2, 2 * HEAD_DIM)
        o = _diff_finish(first_map.pop(t // 2), o, lam, gsub, lam_init)
        o_ref[:, sl] = _gated(o, gp_ref[:, sl])

    _pipelined(8, scores, finish)


STATE_TAILS = ((8, HEAD_DIM), (8, HEAD_DIM), (2, HEAD_DIM), (2, HEAD_DIM), (MLA_KV_LORA,), (MLA_ROPE,),
               (4, 2 * HEAD_DIM), (4, 2 * HEAD_DIM))


def _ctx_attention(p, q_up, kv_up, ckvn, sink, lam, gsub, layer, depth, lam_init, ctx, n_tok, states):
    nb = ctx.rows // SEQ
    wide = 8 * HEAD_DIM

    def blk(col, w=wide):
        return pl.BlockSpec((SEQ, w), lambda b: (b, col // w))

    def small(shape):
        return pl.BlockSpec((None,) + shape, lambda b: (layer,) + (0,) * len(shape))

    def state(*tail):
        return pl.BlockSpec((None, None, SEQ) + tail, lambda b: (b, layer, 0) + (0,) * len(tail))

    def body(aq, ak, av, bq, bk, bv, sink_ref, cqn, cqp, ckv, ckpe, dq, dk, dv, lam_ref, gsub_ref,
             gp_a, gp_b, gp_c, gp_d, ckvn_ref, o_ref, *state_refs):
        def out(mixer):
            return o_ref.at[:, mixer * wide:(mixer + 1) * wide]

        _ctx_a_kernel(aq, ak, av, gp_a, out(0))
        _ctx_b_kernel(bq, bk, bv, sink_ref, gp_b, out(1), layer=layer)
        _ctx_c_kernel(cqn, cqp, ckv, ckpe, gp_c, out(2))
        _ctx_d_kernel(dq, dk, dv, lam_ref, gsub_ref, gp_d, out(3), lam_init=lam_init)
        _state_kernel(ak, av, bk, bv, ckvn_ref, ckpe, dk, dv, *state_refs)

    outs = _aliased_call(
        body, 21, list(states), grid=(nb,),
        in_specs=[blk(A_NQ), blk(A_NK), blk(A_NV),
                  blk(A_SWQ), blk(A_SWK, 256), blk(A_SWV, 256), pl.BlockSpec(memory_space=pltpu.SMEM),
                  blk(0), blk(wide), blk(0, 2 * wide), blk(A_KPE, HEAD_DIM),
                  blk(B_DQ), blk(B_DK), blk(B_DV), small((4, HEAD_DIM)), small((1, 2 * HEAD_DIM))]
                 + [blk(B_GP + mixer * wide) for mixer in range(4)]
                 + [pl.BlockSpec((SEQ, MLA_KV_LORA), lambda b: (b, 0))],
        out_specs=[pl.BlockSpec((SEQ, 4 * wide), lambda b: (b, 0))] + [state(*t) for t in STATE_TAILS],
        out_shape=[jax.ShapeDtypeStruct((n_tok, D_MODEL), BF16)]
                  + [jax.ShapeDtypeStruct((nb, depth, SEQ) + t, F32) for t in STATE_TAILS],
        compiler_params=_cparams(1), name="ctx_mixers",
    )(p, p, p, p, p, p, sink, q_up, q_up, kv_up, p, p, p, p, lam, gsub, p, p, p, p, ckvn,
      *states)
    return outs[0], tuple(outs[1:])


def _nat_win_start(n):
    n_rows = DEC_SEQ // GRID_W
    return min(max(2 * n - NAT_KR // 2, 0), n_rows - NAT_SPAN)


def _nat_pattern_key(n):
    n_rows = DEC_SEQ // GRID_W
    q_per = TQ_WIN // GRID_W
    s = _nat_win_start(n)
    return tuple((q_per * n + j - s, min(max(q_per * n + j - NAT_KR // 2, 0), n_rows - NAT_KR) - s)
                 for j in range(q_per))


NAT_PATTERN_STEPS = tuple(sorted({_nat_pattern_key(n): n for n in reversed(range(DEC_SEQ // TQ_WIN))}.values()))
NAT_PATTERN = tuple([_nat_pattern_key(m) for m in NAT_PATTERN_STEPS].index(_nat_pattern_key(n))
                    for n in range(DEC_SEQ // TQ_WIN))


def _lat_a_kernel(q_ref, k_ref, v_ref, ck_ref, cv_ref, bias_ref, gp_ref, o_ref):
    head_rows = pl.ds(pl.program_id(1), PAST_LEN, stride=NAT_HEADS)
    k = k_ref[...].astype(BF16)
    v = v_ref[...].astype(BF16)
    ck = ck_ref[head_rows, :].astype(BF16)
    cv = cv_ref[head_rows, :].astype(BF16)

    def scores(n):
        win = slice(_nat_win_start(n) * GRID_W, (_nat_win_start(n) + NAT_SPAN) * GRID_W)
        q = (q_ref[_head(n, TQ_WIN), :] * QSCALE).astype(BF16)
        return [(_nt(q, ck), cv), (_nt(q, k[win]) + bias_ref[NAT_PATTERN[n]], v[win])]

    def finish(n, parts):
        rows = _head(n, TQ_WIN)
        o_ref[rows, :] = _gated(_softmax_pv(parts), gp_ref[rows, :])

    _pipelined(DEC_SEQ // TQ_WIN, scores, finish, ahead=2)


def _swa_win_start(n):
    return min(max(n - 1, 0), DEC_SEQ // TQ_WIN - 3) * TQ_WIN


def _lat_b_kernel(q_ref, k_ref, v_ref, ck_ref, cv_ref, sink_ref, mask_ref, cos_ref, sl_ref, sh_ref,
                  gp_ref, o_ref, *, layer):
    k = _rope(k_ref[...], cos_ref[...], sl_ref[...], sh_ref[...], 32).astype(BF16)
    v = v_ref[...].astype(BF16)
    ck = ck_ref[...].astype(BF16)
    cv = cv_ref[...].astype(BF16)
    sink = sink_ref[layer, pl.program_id(1)] * LOG2E

    def scores(n):
        rows = _head(n, TQ_WIN)
        w0 = _swa_win_start(n)
        win = slice(w0, w0 + 3 * TQ_WIN)
        q = _rope(q_ref[rows, :], cos_ref[rows, :], sl_ref[rows, :], sh_ref[rows, :], 32)
        q = (q * QSCALE).astype(BF16)
        return [(_nt(q, ck), cv), (_nt(q, k[win]) + mask_ref[n - w0 // TQ_WIN], v[win])]

    def finish(n, parts):
        rows = _head(n, TQ_WIN)
        o_ref[rows, :] = _gated(_softmax_pv(parts, extra=sink), gp_ref[rows, :])

    _pipelined(DEC_SEQ // TQ_WIN, scores, finish, ahead=2)


def _lat_c_kernel(qn_ref, qp_ref, kn_ref, v_ref, kpe_ref, ckn_ref, cv_ref, ckpe_ref,
                  cos_ref, sl_ref, sh_ref, gp_ref, o_ref):
    kpe = _rope(kpe_ref[...], cos_ref[...], sl_ref[...], sh_ref[...], 16)
    keys = jnp.concatenate([
        jnp.concatenate([ckn_ref[...], ckpe_ref[...].astype(BF16)], axis=-1),
        jnp.concatenate([kn_ref[...], kpe.astype(BF16)], axis=-1)], axis=0)
    vals = jnp.concatenate([cv_ref[...], v_ref[...]], axis=0)

    def scores(n):
        rows = _head(n, TQ_DENSE)
        qp = _rope(qp_ref[rows, :], cos_ref[rows, :], sl_ref[rows, :], sh_ref[rows, :], 16)
        q = (jnp.concatenate([qn_ref[rows, :], qp], axis=-1) * QSCALE_MLA).astype(BF16)
        return [(_nt(q, keys), vals)]

    def finish(n, parts):
        rows = _head(n, TQ_DENSE)
        o_ref[rows, :] = _gated(_softmax_pv(parts), gp_ref[rows, :])

    _pipelined(DEC_SEQ // TQ_DENSE, scores, finish, ahead=2)


def _lat_d_kernel(q1_ref, q2_ref, k1_ref, k2_ref, v_ref, ck1_ref, ck2_ref, cv_ref,
                  lam_ref, gsub_ref, cos_ref, sl_ref, sh_ref, gp_ref, o_ref, *, lam_init):
    cos, sl, sh = cos_ref[...], sl_ref[...], sh_ref[...]
    keys1 = jnp.concatenate([ck1_ref[...], _rope(k1_ref[...], cos, sl, sh, 32)], axis=0).astype(BF16)
    keys2 = jnp.concatenate([ck2_ref[...], _rope(k2_ref[...], cos, sl, sh, 32)], axis=0).astype(BF16)
    vals = jnp.concatenate([cv_ref[...], v_ref[...]], axis=0).astype(BF16)
    lam = _diff_lambda(lam_ref, lam_init)
    gsub = gsub_ref[...]
    first_map = {}

    def scores(t):
        rows = _head(t // 2, TQ_DENSE)
        q_ref, keys = ((q1_ref, keys1), (q2_ref, keys2))[t % 2]
        q = _rope(q_ref[rows, :], cos_ref[rows, :], sl_ref[rows, :], sh_ref[rows, :], 32)
        return [(_nt((q * QSCALE).astype(BF16), keys), vals)]

    def finish(t, parts):
        o = _softmax_pv(parts)
        if t % 2 == 0:
            first_map[t // 2] = o
            return
        rows = _head(t // 2, TQ_DENSE)
        o = _diff_finish(first_map.pop(t // 2), o, lam, gsub, lam_init)
        o_ref[rows, :] = _gated(o, gp_ref[rows, :])

    _pipelined(2 * (DEC_SEQ // TQ_DENSE), scores, finish)


def _lat_attention(mixed, p, q_up, kv_up, kv_cache, caches, layer, nat_bias, swa_mask, sink, lam,
                   gsub, lam_init, rope128, rope64, lat, n_tok):
    nb = lat.rows // DEC_SEQ
    rb0 = lat.row0 // DEC_SEQ
    c_nat_k, c_nat_v, c_swa_k, c_swa_v, c_kpe, c_dk, c_dv = caches
    hd = HEAD_DIM

    def rows(col_of, w=hd):
        return pl.BlockSpec((DEC_SEQ, w), lambda b, h: (rb0 + b, col_of(h)))

    def cache(col_of, w=hd):
        return pl.BlockSpec((None, None, PAST_LEN, w), lambda b, h: (b, layer, 0, col_of(h)))

    def table():
        return pl.BlockSpec((DEC_SEQ, hd), lambda b, h: (0, 0))

    def small(shape):
        return pl.BlockSpec((None,) + shape, lambda b, h: (layer,) + (0,) * len(shape))

    all_heads = pl.BlockSpec((None, None, PAST_LEN * NAT_HEADS, hd), lambda b, h: (b, layer, 0, 0))

    def call(body, n_in, heads, in_specs, args, mixed, w, col0, name):
        return _aliased_call(
            body, n_in, [mixed], grid=(nb, heads), in_specs=in_specs,
            out_specs=rows(lambda h: col0 // w + h, w),
            out_shape=jax.ShapeDtypeStruct((n_tok, D_MODEL), BF16),
            compiler_params=_cparams(2), name=name,
        )(*args, mixed)

    mixed = call(
        _lat_a_kernel, 7, 8,
        [rows(lambda h: A_NQ // hd + h), rows(lambda h: A_NK // hd + h), rows(lambda h: A_NV // hd + h),
         all_heads, all_heads,
         pl.BlockSpec((None, None, len(NAT_PATTERN_STEPS), TQ_WIN, NAT_SPAN * GRID_W),
                      lambda b, h: (layer, h, 0, 0, 0)),
         rows(lambda h: B_GP // hd + h)],
        (p, p, p, c_nat_k, c_nat_v, nat_bias, p), mixed, hd, 0, "lat_nat")
    mixed = call(
        functools.partial(_lat_b_kernel, layer=layer), 11, 8,
        [rows(lambda h: A_SWQ // hd + h), rows(lambda h: A_SWK // hd + h // 4),
         rows(lambda h: A_SWV // hd + h // 4), cache(lambda h: h // 4), cache(lambda h: h // 4),
         pl.BlockSpec(memory_space=pltpu.SMEM),
         pl.BlockSpec((3, TQ_WIN, 3 * TQ_WIN), lambda b, h: (0, 0, 0)), table(), table(), table(),
         rows(lambda h: B_GP // hd + 8 + h)],
        (p, p, p, c_swa_k, c_swa_v, sink, swa_mask, *rope128, p), mixed, hd, 8 * hd, "lat_swa")
    mixed = call(
        _lat_c_kernel, 12, 8,
        [rows(lambda h: h), rows(lambda h: 8 + h),
         rows(lambda h: 2 * h), rows(lambda h: 2 * h + 1), rows(lambda h: A_KPE // hd),
         pl.BlockSpec((PAST_LEN, hd), lambda b, h: (b, 2 * h)),
         pl.BlockSpec((PAST_LEN, hd), lambda b, h: (b, 2 * h + 1)),
         cache(lambda h: 0), table(), table(), table(),
         rows(lambda h: B_GP // hd + 16 + h)],
        (q_up, q_up, kv_up, kv_up, p, kv_cache, kv_cache, c_kpe, *rope64, p), mixed, hd, 16 * hd,
        "lat_mla")
    mixed = call(
        functools.partial(_lat_d_kernel, lam_init=lam_init), 14, 4,
        [rows(lambda h: B_DQ // hd + 2 * h), rows(lambda h: B_DQ // hd + 2 * h + 1),
         rows(lambda h: B_DK // hd + 2 * h), rows(lambda h: B_DK // hd + 2 * h + 1),
         rows(lambda h: B_DV // (2 * hd) + h, 2 * hd),
         cache(lambda h: 2 * h), cache(lambda h: 2 * h + 1), cache(lambda h: h, 2 * hd),
         small((4, hd)), small((1, 2 * hd)), table(), table(), table(),
         rows(lambda h: B_GP // (2 * hd) + 12 + h, 2 * hd)],
        (p, p, p, p, p, c_dk, c_dk, c_dv, lam, gsub, *rope128, p), mixed, 2 * hd, 24 * hd,
        "lat_diff")
    return mixed


def _state_kernel(nk_ref, nv_ref, sk_ref, sv_ref, ckv_ref, kpe_ref, dk_ref, dv_ref,
                  o_nk, o_nv, o_sk, o_sv, o_ckv, o_kpe, o_dk, o_dv):
    for src, dst, heads, w in ((nk_ref, o_nk, 8, HEAD_DIM), (nv_ref, o_nv, 8, HEAD_DIM),
                               (sk_ref, o_sk, 2, HEAD_DIM), (sv_ref, o_sv, 2, HEAD_DIM),
                               (dk_ref, o_dk, 4, 2 * HEAD_DIM), (dv_ref, o_dv, 4, 2 * HEAD_DIM)):
        for h in range(heads):
            dst[:, h, :] = src[:, h * w:(h + 1) * w]
    o_ckv[...] = ckv_ref[...]
    o_kpe[...] = kpe_ref[:, :MLA_ROPE]


def _rope_tables(rot_dim):
    axis_dim = rot_dim // 2
    inv = 1.0 / (ROPE_BASE ** (jnp.arange(0, axis_dim, 2, dtype=F32) / axis_dim))
    t = jnp.arange(DEC_SEQ)
    ang_row = (t // GRID_W).astype(F32)[:, None] * inv
    ang_col = (t % GRID_W).astype(F32)[:, None] * inv
    zeros = jnp.zeros_like(ang_row)
    pad = HEAD_DIM - rot_dim
    cos = jnp.concatenate([jnp.cos(ang_row)] * 2 + [jnp.cos(ang_col)] * 2
                          + [jnp.ones((DEC_SEQ, pad), F32)], axis=-1)
    sin_lo = jnp.concatenate([-jnp.sin(ang_row), zeros, -jnp.sin(ang_col), zeros,
                              jnp.zeros((DEC_SEQ, pad), F32)], axis=-1)
    sin_hi = jnp.concatenate([zeros, jnp.sin(ang_row), zeros, jnp.sin(ang_col),
                              jnp.zeros((DEC_SEQ, pad), F32)], axis=-1)
    return cos, sin_lo, sin_hi


def _nat_bias(rpb):
    n_rows = DEC_SEQ // GRID_W
    steps = len(NAT_PATTERN_STEPS)
    q_per = TQ_WIN // GRID_W
    rep = jnp.asarray(NAT_PATTERN_STEPS)
    qr = rep[:, None] * q_per + jnp.arange(q_per)[None, :]
    kr = jnp.asarray([_nat_win_start(n) for n in NAT_PATTERN_STEPS])[:, None] + jnp.arange(NAT_SPAN)[None, :]
    c = jnp.arange(GRID_W)
    row0 = jnp.clip(qr - NAT_KR // 2, 0, n_rows - NAT_KR)
    col0 = jnp.clip(c - NAT_KC // 2, 0, GRID_W - NAT_KC)
    ok_r = (kr[:, None, :] >= row0[:, :, None]) & (kr[:, None, :] < row0[:, :, None] + NAT_KR)
    ok_c = (c[None, :] >= col0[:, None]) & (c[None, :] < col0[:, None] + NAT_KC)
    dr = jnp.clip(kr[:, None, :] - qr[:, :, None], -(NAT_KR - 1), NAT_KR - 1) + (NAT_KR - 1)
    dc = jnp.clip(c[None, :] - c[:, None], -(NAT_KC - 1), NAT_KC - 1) + (NAT_KC - 1)
    dr, ok_r = jnp.repeat(dr, GRID_W, axis=-1), jnp.repeat(ok_r, GRID_W, axis=-1)
    dc, ok_c = jnp.tile(dc, (1, NAT_SPAN)), jnp.tile(ok_c, (1, NAT_SPAN))
    sel_r = jax.nn.one_hot(dr, 2 * NAT_KR - 1, dtype=F32)
    sel_c = jax.nn.one_hot(dc, 2 * NAT_KC - 1, dtype=F32)
    bias = jnp.einsum("nqKr,lhrc,xKc->lhnqxK", sel_r, rpb.astype(F32), sel_c,
                      precision=lax.Precision.HIGHEST)
    ok = ok_r[:, :, None, :] & ok_c[None, None, :, :]
    bias = jnp.where(ok[None, None], bias * LOG2E, NEG_INF)
    return bias.reshape(rpb.shape[0], rpb.shape[1], steps, TQ_WIN, NAT_SPAN * GRID_W)


def _swa_mask():
    qi = jnp.arange(TQ_WIN)[None, :, None] + TQ_WIN * jnp.arange(3)[:, None, None]
    kj = jnp.arange(3 * TQ_WIN)[None, None, :]
    return jnp.where(jnp.abs(qi - kj) <= SWA_WINDOW, 0.0, NEG_INF).astype(F32)


def _pack_w_uq(w_uq):
    depth = w_uq.shape[0]
    w = w_uq.reshape(depth, MLA_Q_LORA, 8, MLA_NOPE + MLA_ROPE)
    nope = w[..., :MLA_NOPE].reshape(depth, MLA_Q_LORA, 8 * MLA_NOPE)
    pe = jnp.pad(w[..., MLA_NOPE:], ((0, 0), (0, 0), (0, 0), (0, HEAD_DIM - MLA_ROPE)))
    return jnp.concatenate([nope, pe.reshape(depth, MLA_Q_LORA, 8 * HEAD_DIM)], axis=-1).astype(BF16)


def kernel(x_prompt, x_sample, cache_nat_k, cache_nat_v, cache_swa_k, cache_swa_v, cache_mla_ckv, cache_mla_kpe, cache_diff_k, cache_diff_v, c, c_ctx, w_mod, b_mod, g_pre, g_post, w_in, w_out, nat_rpb, swa_sink, mla_g_q, mla_g_kv, mla_w_uq, mla_w_ukv, diff_lambda, diff_g_subln):
    depth = w_mod.shape[0]
    bc, bl = x_prompt.shape[0], x_sample.shape[0]
    n_ctx, n_lat = bc * SEQ, bl * DEC_SEQ
    n_tok = n_ctx + n_lat
    assert n_ctx % DEC_SEQ == 0 and x_prompt.shape[1] == SEQ and x_sample.shape[1] == DEC_SEQ
    ctx = Group(0, n_ctx, lambda i: 0)
    lat = Group(n_ctx, n_lat, lambda i: 1 + i // (DEC_SEQ // ROW_BLOCK))

    n_cond = -(-(1 + bl) // 8) * 8
    cond = jnp.concatenate([c_ctx[None], c, jnp.zeros((n_cond - 1 - bl, D_MODEL), F32)], axis=0)
    mod = _modulation(cond, w_mod, b_mod).reshape(depth, n_cond, 3, D_MODEL)

    w_t = jnp.swapaxes(w_in, 1, 2)
    w_p = _cast_rows(w_t, B_START, N_B, PACK_ROWS_B, 0, N_PROJ, None, "pack_w_b")
    w_p = _cast_rows(w_t, 0, N_A, PACK_ROWS_A, N_B, N_PROJ, w_p, "pack_w_a")
    w_out_b = w_out.astype(BF16)
    w_uq_p = _pack_w_uq(mla_w_uq)
    w_ukv_b = mla_w_ukv.astype(BF16)
    g_pre3 = g_pre.reshape(depth, 1, D_MODEL)
    g_post3 = g_post.reshape(depth, 1, D_MODEL)
    g_q3 = mla_g_q.reshape(depth, 1, MLA_Q_LORA)
    g_kv3 = mla_g_kv.reshape(depth, 1, MLA_KV_LORA)
    gsub3 = diff_g_subln.reshape(depth, 1, 2 * HEAD_DIM)
    rope128 = _rope_tables(HEAD_DIM)
    rope64 = _rope_tables(MLA_ROPE)
    nat_bias = _nat_bias(nat_rpb)
    swa_mask = _swa_mask()

    caches = (cache_nat_k.reshape(bl, depth, PAST_LEN * NAT_HEADS, HEAD_DIM),
              cache_nat_v.reshape(bl, depth, PAST_LEN * NAT_HEADS, HEAD_DIM),
              cache_swa_k.reshape(bl, depth, PAST_LEN, -1), cache_swa_v.reshape(bl, depth, PAST_LEN, -1),
              jnp.pad(cache_mla_kpe, ((0, 0), (0, 0), (0, 0), (0, HEAD_DIM - MLA_ROPE))),
              cache_diff_k.reshape(bl, depth, PAST_LEN, -1), cache_diff_v.reshape(bl, depth, PAST_LEN, -1))

    x_c = x_prompt.reshape(n_ctx, D_MODEL)
    x_l = x_sample.reshape(n_lat, D_MODEL)
    h = _norm_mod(x_c, g_pre3, mod, 0, ctx, n_tok, None)
    h = _norm_mod(x_l, g_pre3, mod, 0, lat, n_tok, h)
    states = ()
    for l in range(depth):
        lam_init = 0.8 - 0.6 * math.exp(-0.3 * l)
        p = _matmul_nt(h, w_p, l, IN_PROJ_TN, "in_proj")
        q_up = _mla_q_up(p, g_q3, w_uq_p, l)
        ckvn, kv_up = _mla_kv_up_tokens(p, g_kv3, w_ukv_b, l)
        kv_cache = _mla_kv_up_cache(cache_mla_ckv, w_ukv_b, l)

        mixed, states = _ctx_attention(p, q_up, kv_up, ckvn, swa_sink, diff_lambda, gsub3, l, depth,
                                       lam_init, ctx, n_tok, states)
        mixed = _lat_attention(mixed, p, q_up, kv_up, kv_cache, caches, l, nat_bias, swa_mask, swa_sink,
                               diff_lambda, gsub3, lam_init, rope128, rope64, lat, n_tok)
        o = _matmul(mixed, w_out_b, l, OUT_PROJ_TN, "out_proj", out_dtype=BF16)
        if l + 1 < depth:
            x_c, h = _post_norm(x_c, o, g_post3, g_pre3, mod, l, ctx, n_tok, None)
            x_l, h = _post_norm(x_l, o, g_post3, g_pre3, mod, l, lat, n_tok, h)
        else:
            x_c = _post(x_c, o, g_post3, mod, l, ctx)
            x_l = _post(x_l, o, g_post3, mod, l, lat)

    return (x_c.reshape(bc, SEQ, D_MODEL), x_l.reshape(bl, DEC_SEQ, D_MODEL)) + tuple(states)
```

```python
import functools
import math
from typing import Callable, NamedTuple

import jax
import jax.numpy as jnp
from jax import lax
from jax.experimental import pallas as pl
from jax.experimental.pallas import tpu as pltpu

F32 = jnp.float32
BF16 = jnp.bfloat16

D_MODEL = 4096
HEAD_DIM = 128
GRID_W = 64
SEQ = 256
DEC_SEQ = 1024
PAST_LEN = 512
NAT_KR, NAT_KC = 8, 16
NAT_HEADS = 8
SWA_WINDOW = 128
MLA_NOPE, MLA_ROPE = 128, 64
MLA_Q_LORA, MLA_KV_LORA = 896, 256
ROPE_BASE = 10000.0
EPS = 1e-6
NEG_INF = -1e30
LOG2E = 1.4426950408889634
QSCALE = HEAD_DIM ** -0.5 * LOG2E
QSCALE_MLA = (MLA_NOPE + MLA_ROPE) ** -0.5 * LOG2E

B_START = 5824
N_B = 7168
N_A = 5888
N_PROJ = N_B + N_A
B_DQ, B_DK, B_DV, B_GP = 0, 1024, 2048, 3072
A_NQ, A_NK, A_NV = N_B + 0, N_B + 1024, N_B + 2048
A_SWQ, A_SWK, A_SWV = N_B + 3072, N_B + 4096, N_B + 4352
A_CQ, A_CKV, A_KPE = N_B + 4608, N_B + 5504, N_B + 5760

V7X_VMEM_BYTES = 64 * 1024 * 1024
VMEM_LIMIT = V7X_VMEM_BYTES * 7 // 8
ROW_BLOCK = 256
PROJ_TM = 1024
IN_PROJ_TN = 768
OUT_PROJ_TN = 1024
Q_UP_TM = 1024
KV_UP_TM = 1024
PACK_ROWS_B = 448
PACK_ROWS_A = 256
TQ_WIN = 128
TQ_DENSE = 256
NAT_SPAN = 10


class Group(NamedTuple):
    row0: int
    rows: int
    mod_row: Callable


def _cparams(n_axes):
    return pltpu.CompilerParams(dimension_semantics=("arbitrary",) * n_axes,
                                vmem_limit_bytes=VMEM_LIMIT)


def _silu(x):
    return x * (1.0 / (1.0 + jnp.exp(-x)))


def _nt(a, b):
    return lax.dot_general(a, b, (((1,), (1,)), ((), ())), preferred_element_type=F32)


def _softmax_pv(parts, extra=None):
    m = parts[0][0].max(axis=-1, keepdims=True)
    for s, _ in parts[1:]:
        m = jnp.maximum(m, s.max(axis=-1, keepdims=True))
    if extra is not None:
        m = jnp.maximum(m, extra)
    den = None
    acc = None
    for s, v in parts:
        p = jnp.exp2(s - m)
        ps = p.sum(axis=-1, keepdims=True)
        pv = jnp.dot(p.astype(BF16), v, preferred_element_type=F32)
        den = ps if den is None else den + ps
        acc = pv if acc is None else acc + pv
    if extra is not None:
        den = den + jnp.exp2(extra - m)
    return acc / den


def _rope(x, cos, sin_lo, sin_hi, half):
    lanes = x.shape[-1]
    return (x * cos + pltpu.roll(x, lanes - half, 1) * sin_lo + pltpu.roll(x, half, 1) * sin_hi)


def _rms(x, g):
    return x * lax.rsqrt(jnp.mean(x * x, axis=-1, keepdims=True) + EPS) * g


def _aliased_call(body, n_in, alias_args, **kw):
    n_alias = len(alias_args)
    n_out = len(kw["out_shape"]) if isinstance(kw["out_shape"], (tuple, list)) else 1
    if n_alias == 0:
        return pl.pallas_call(body, **kw)

    def with_aliases(*refs):
        body(*refs[:n_in], *refs[n_in + n_alias:])

    kw["in_specs"] = list(kw["in_specs"]) + [pl.BlockSpec(memory_space=pl.ANY)] * n_alias
    kw["input_output_aliases"] = {n_in + k: n_out - n_alias + k for k in range(n_alias)}
    return pl.pallas_call(with_aliases, **kw)


def _mod_kernel(c_ref, w_ref, b_ref, o_ref):
    s = _silu(c_ref[...]).astype(BF16)
    o_ref[...] = jnp.dot(s, w_ref[...].astype(BF16), preferred_element_type=F32) + b_ref[...]


def _modulation(cond, w_mod, b_mod):
    depth = w_mod.shape[0]
    rows = cond.shape[0]
    tn = 512
    n = 3 * D_MODEL
    return pl.pallas_call(
        _mod_kernel,
        grid=(depth, n // tn),
        in_specs=[
            pl.BlockSpec((rows, D_MODEL), lambda l, j: (0, 0)),
            pl.BlockSpec((None, D_MODEL, tn), lambda l, j: (l, 0, j)),
            pl.BlockSpec((None, 1, tn), lambda l, j: (l, 0, j)),
        ],
        out_specs=pl.BlockSpec((None, rows, tn), lambda l, j: (l, 0, j)),
        out_shape=jax.ShapeDtypeStruct((depth, rows, n), F32),
        compiler_params=_cparams(2),
        name="modulation",
    )(cond, w_mod, b_mod.reshape(depth, 1, n))


def _mod_spec(grp, layer):
    return pl.BlockSpec((None, None, 3, D_MODEL), lambda i: (layer, grp.mod_row(i), 0, 0))


def _vec_spec(layer):
    return pl.BlockSpec((None, 1, D_MODEL), lambda i: (layer, 0, 0))


def _stream(deep):
    return pl.Buffered(3) if deep else None


def _streamed_call(body, n_in, alias_args, *, grid, in_specs, out_specs, out_shape, name):
    out_specs = list(out_specs) if isinstance(out_specs, (list, tuple)) else [out_specs]
    shapes = list(out_shape) if isinstance(out_shape, (list, tuple)) else [out_shape]
    n_alias, n_out = len(alias_args), len(shapes)

    def outer(*refs):
        ins, outs = refs[:n_in], refs[n_in + n_alias:]
        pltpu.emit_pipeline(body, grid=grid, in_specs=in_specs, out_specs=out_specs)(*ins, *outs)

    any_spec = pl.BlockSpec(memory_space=pl.ANY)
    res = pl.pallas_call(
        outer,
        in_specs=[any_spec] * (n_in + n_alias),
        out_specs=[any_spec] * n_out,
        out_shape=shapes,
        input_output_aliases={n_in + k: n_out - n_alias + k for k in range(n_alias)},
        compiler_params=pltpu.CompilerParams(vmem_limit_bytes=VMEM_LIMIT),
        name=name,
    )
    return res


def _tok_spec(grp, deep=False):
    return pl.BlockSpec((ROW_BLOCK, D_MODEL), lambda i: (grp.row0 // ROW_BLOCK + i, 0),
                        pipeline_mode=_stream(deep))


def _own_spec(deep=False):
    return pl.BlockSpec((ROW_BLOCK, D_MODEL), lambda i: (i, 0), pipeline_mode=_stream(deep))


def _norm_mod_kernel(x_ref, g_ref, m_ref, h_ref):
    y = _rms(x_ref[...], g_ref[...])
    h_ref[...] = (y * (1.0 + m_ref[1:2, :]) + m_ref[0:1, :]).astype(BF16)


def _norm_mod(x, g_pre, mod, layer, grp, n_tok, h_prev):
    return _streamed_call(
        _norm_mod_kernel, 3, [] if h_prev is None else [h_prev],
        grid=(grp.rows // ROW_BLOCK,),
        in_specs=[_own_spec(deep=True), _vec_spec(layer), _mod_spec(grp, layer)],
        out_specs=_tok_spec(grp),
        out_shape=jax.ShapeDtypeStruct((n_tok, D_MODEL), BF16),
        name="norm_mod",
    )(x, g_pre, mod, *([] if h_prev is None else [h_prev]))[0]


def _post_kernel(x_ref, o_ref, g_ref, m_ref, y_ref):
    y_ref[...] = x_ref[...] + m_ref[2:3, :] * _rms(o_ref[...].astype(F32), g_ref[...])


def _post(x, o, g_post, mod, layer, grp):
    return _streamed_call(
        _post_kernel, 4, [],
        grid=(grp.rows // ROW_BLOCK,),
        in_specs=[_own_spec(deep=True), _tok_spec(grp, deep=True), _vec_spec(layer), _mod_spec(grp, layer)],
        out_specs=_own_spec(),
        out_shape=jax.ShapeDtypeStruct((grp.rows, D_MODEL), F32),
        name="post_residual",
    )(x, o, g_post, mod)[0]


def _post_norm_kernel(x_ref, o_ref, g_ref, m_ref, gn_ref, mn_ref, y_ref, h_ref):
    y = x_ref[...] + m_ref[2:3, :] * _rms(o_ref[...].astype(F32), g_ref[...])
    y_ref[...] = y
    h_ref[...] = (_rms(y, gn_ref[...]) * (1.0 + mn_ref[1:2, :]) + mn_ref[0:1, :]).astype(BF16)


def _post_norm(x, o, g_post, g_pre, mod, layer, grp, n_tok, h_prev):
    return _streamed_call(
        _post_norm_kernel, 6, [] if h_prev is None else [h_prev],
        grid=(grp.rows // ROW_BLOCK,),
        in_specs=[_own_spec(deep=True), _tok_spec(grp, deep=True), _vec_spec(layer), _mod_spec(grp, layer),
                  _vec_spec(layer + 1), _mod_spec(grp, layer + 1)],
        out_specs=[_own_spec(), _tok_spec(grp)],
        out_shape=[jax.ShapeDtypeStruct((grp.rows, D_MODEL), F32),
                   jax.ShapeDtypeStruct((n_tok, D_MODEL), BF16)],
        name="post_norm",
    )(x, o, g_post, mod, g_pre, mod, *([] if h_prev is None else [h_prev]))


def _q_up_kernel(a_ref, b_ref, g_ref, w_ref, o_ref):
    cq = jnp.concatenate([a_ref[...], b_ref[:, :MLA_Q_LORA - 512]], axis=-1)
    o_ref[...] = jnp.dot(_rms(cq, g_ref[...]).astype(BF16), w_ref[...], preferred_element_type=F32)


def _mla_q_up(p, g, w_uq, layer):
    m = p.shape[0]
    tm = Q_UP_TM
    n = w_uq.shape[-1]
    first = A_CQ // 512
    return pl.pallas_call(
        _q_up_kernel,
        grid=(m // tm,),
        in_specs=[
            pl.BlockSpec((tm, 512), lambda i: (i, first)),
            pl.BlockSpec((tm, 512), lambda i: (i, first + 1)),
            pl.BlockSpec((None, 1, MLA_Q_LORA), lambda i: (layer, 0, 0)),
            pl.BlockSpec((None, MLA_Q_LORA, n), lambda i: (layer, 0, 0)),
        ],
        out_specs=pl.BlockSpec((tm, n), lambda i: (i, 0)),
        out_shape=jax.ShapeDtypeStruct((m, n), F32),
        compiler_params=_cparams(1),
        name="mla_q_up",
    )(p, p, g, w_uq)


def _kv_up_tokens_kernel(a_ref, b_ref, g_ref, w_ref, ckvn_ref, kv_ref):
    ckvn = _rms(jnp.concatenate([a_ref[...], b_ref[...]], axis=-1), g_ref[...])
    ckvn_ref[...] = ckvn
    kv_ref[...] = jnp.dot(ckvn.astype(BF16), w_ref[...], preferred_element_type=F32).astype(BF16)


def _mla_kv_up_tokens(p, g, w_ukv, layer):
    m = p.shape[0]
    tm = KV_UP_TM
    n = w_ukv.shape[-1]
    first = A_CKV // HEAD_DIM
    return pl.pallas_call(
        _kv_up_tokens_kernel,
        grid=(m // tm,),
        in_specs=[
            pl.BlockSpec((tm, HEAD_DIM), lambda i: (i, first)),
            pl.BlockSpec((tm, HEAD_DIM), lambda i: (i, first + 1)),
            pl.BlockSpec((None, 1, MLA_KV_LORA), lambda i: (layer, 0, 0)),
            pl.BlockSpec((None, MLA_KV_LORA, n), lambda i: (layer, 0, 0)),
        ],
        out_specs=[pl.BlockSpec((tm, MLA_KV_LORA), lambda i: (i, 0)), pl.BlockSpec((tm, n), lambda i: (i, 0))],
        out_shape=[jax.ShapeDtypeStruct((m, MLA_KV_LORA), F32), jax.ShapeDtypeStruct((m, n), BF16)],
        compiler_params=_cparams(1),
        name="mla_kv_up",
    )(p, p, g, w_ukv)


def _kv_up_cache_kernel(c_ref, w_ref, kv_ref):
    kv_ref[...] = jnp.dot(c_ref[...].astype(BF16), w_ref[...], preferred_element_type=F32).astype(BF16)


def _mla_kv_up_cache(cache_ckv, w_ukv, layer):
    bl = cache_ckv.shape[0]
    n = w_ukv.shape[-1]
    return pl.pallas_call(
        _kv_up_cache_kernel,
        grid=(bl,),
        in_specs=[
            pl.BlockSpec((None, None, PAST_LEN, MLA_KV_LORA), lambda b: (b, layer, 0, 0)),
            pl.BlockSpec((None, MLA_KV_LORA, n), lambda b: (layer, 0, 0)),
        ],
        out_specs=pl.BlockSpec((PAST_LEN, n), lambda b: (b, 0)),
        out_shape=jax.ShapeDtypeStruct((bl * PAST_LEN, n), BF16),
        compiler_params=_cparams(1),
        name="mla_kv_up_cache",
    )(cache_ckv, w_ukv)


def _mm_kernel(a_ref, b_ref, o_ref):
    o_ref[...] = jnp.dot(a_ref[...], b_ref[...], preferred_element_type=F32).astype(o_ref.dtype)


def _matmul(a, w, layer, tn, name, out_dtype=F32):
    m, k = a.shape
    n = w.shape[-1]
    tm = PROJ_TM
    assert m % tm == 0 and n % tn == 0
    return pl.pallas_call(
        _mm_kernel,
        grid=(m // tm, n // tn),
        in_specs=[
            pl.BlockSpec((tm, k), lambda i, j: (i, 0)),
            pl.BlockSpec((None, k, tn), lambda i, j: (layer, 0, j)),
        ],
        out_specs=pl.BlockSpec((tm, tn), lambda i, j: (i, j)),
        out_shape=jax.ShapeDtypeStruct((m, n), out_dtype),
        compiler_params=_cparams(2),
        name=name,
    )(a, w)


def _cast_kernel(w_ref, o_ref):
    o_ref[...] = w_ref[...].astype(BF16)


def _cast_rows(w_t, row0, rows, block_rows, out_row0, out_rows, prev, name):
    depth, _, k = w_t.shape
    first, out_first = row0 // block_rows, out_row0 // block_rows
    assert row0 % block_rows == 0 and rows % block_rows == 0 and out_row0 % block_rows == 0
    return _aliased_call(
        _cast_kernel, 1, [] if prev is None else [prev],
        grid=(depth, rows // block_rows),
        in_specs=[pl.BlockSpec((None, block_rows, k), lambda l, j: (l, first + j, 0))],
        out_specs=pl.BlockSpec((None, block_rows, k), lambda l, j: (l, out_first + j, 0)),
        out_shape=jax.ShapeDtypeStruct((depth, out_rows, k), BF16),
        compiler_params=_cparams(2),
        name=name,
    )(w_t, *([] if prev is None else [prev]))


def _mm_nt_kernel(a_ref, b_ref, o_ref):
    o_ref[...] = _nt(a_ref[...], b_ref[...]).astype(o_ref.dtype)


def _matmul_nt(a, w_t, layer, tn, name):
    m, k = a.shape
    n = w_t.shape[1]
    tm = PROJ_TM
    assert m % tm == 0 and n % tn == 0
    return pl.pallas_call(
        _mm_nt_kernel,
        grid=(m // tm, n // tn),
        in_specs=[
            pl.BlockSpec((tm, k), lambda i, j: (i, 0)),
            pl.BlockSpec((None, tn, k), lambda i, j: (layer, j, 0)),
        ],
        out_specs=pl.BlockSpec((tm, tn), lambda i, j: (i, j)),
        out_shape=jax.ShapeDtypeStruct((m, n), F32),
        compiler_params=_cparams(2),
        name=name,
    )(a, w_t)


def _gated(o, gp):
    return (o * _silu(gp)).astype(BF16)


def _pipelined(n_steps, scores, finish, ahead=1):
    queue = [scores(n) for n in range(min(ahead, n_steps))]
    for n in range(n_steps):
        cur = queue.pop(0)
        if n + ahead < n_steps:
            queue.append(scores(n + ahead))
        finish(n, cur)


def _head(h, w=HEAD_DIM):
    return slice(h * w, (h + 1) * w)


def _ctx_a_kernel(q_ref, k_ref, v_ref, gp_ref, o_ref):
    def scores(h):
        q = (q_ref[:, _head(h)] * QSCALE).astype(BF16)
        return [(_nt(q, k_ref[:, _head(h)].astype(BF16)), v_ref[:, _head(h)].astype(BF16))]

    def finish(h, parts):
        o_ref[:, _head(h)] = _gated(_softmax_pv(parts), gp_ref[:, _head(h)])

    _pipelined(8, scores, finish)


def _ctx_b_kernel(q_ref, k_ref, v_ref, sink_ref, gp_ref, o_ref, *, layer):
    def scores(h):
        q = (q_ref[:, _head(h)] * QSCALE).astype(BF16)
        return [(_nt(q, k_ref[:, _head(h // 4)].astype(BF16)), v_ref[:, _head(h // 4)].astype(BF16))]

    def finish(h, parts):
        o = _softmax_pv(parts, extra=sink_ref[layer, h] * LOG2E)
        o_ref[:, _head(h)] = _gated(o, gp_ref[:, _head(h)])

    _pipelined(8, scores, finish)


def _ctx_c_kernel(qn_ref, qp_ref, kv_ref, kpe_ref, gp_ref, o_ref):
    kpe = kpe_ref[...].astype(BF16)

    def scores(h):
        kn = kv_ref[:, _head(2 * h)].astype(BF16)
        q = (jnp.concatenate([qn_ref[:, _head(h)], qp_ref[:, _head(h)]], axis=-1) * QSCALE_MLA).astype(BF16)
        return [(_nt(q, jnp.concatenate([kn, kpe], axis=-1)), kv_ref[:, _head(2 * h + 1)].astype(BF16))]

    def finish(h, parts):
        o_ref[:, _head(h)] = _gated(_softmax_pv(parts), gp_ref[:, _head(h)])

    _pipelined(8, scores, finish)


def _diff_lambda(lam_ref, lam_init):
    lp = lam_ref[...]
    a = jnp.sum(lp[0:1, :] * lp[1:2, :], axis=-1, keepdims=True)
    b = jnp.sum(lp[2:3, :] * lp[3:4, :], axis=-1, keepdims=True)
    return jnp.exp(a) - jnp.exp(b) + lam_init


def _diff_finish(o1, o2, lam, gsub, lam_init):
    return _rms(o1 - lam * o2, gsub) * (1.0 - lam_init)


def _ctx_d_kernel(q_ref, k_ref, v_ref, lam_ref, gsub_ref, gp_ref, o_ref, *, lam_init):
    lam = _diff_lambda(lam_ref, lam_init)
    gsub = gsub_ref[...]

    first_map = {}

    def scores(t):
        q = (q_ref[:, _head(t)] * QSCALE).astype(BF16)
        return [(_nt(q, k_ref[:, _head(t)].astype(BF16)), v_ref[:, _head(t // 2, 2 * HEAD_DIM)].astype(BF16))]

    def finish(t, parts):
        o = _softmax_pv(parts)
        if t % 2 == 0:
            first_map[t // 2] = o
            return
        sl = _head(t // 2, 2 * HEAD_DIM)
        o = _diff_finish(first_map.pop(t // 2), o, lam, gsub, lam_init)
        o_ref[:, sl] = _gated(o, gp_ref[:, sl])

    _pipelined(8, scores, finish)


STATE_TAILS = ((8, HEAD_DIM), (8, HEAD_DIM), (2, HEAD_DIM), (2, HEAD_DIM), (MLA_KV_LORA,), (MLA_ROPE,),
               (4, 2 * HEAD_DIM), (4, 2 * HEAD_DIM))


def _ctx_attention(p, q_up, kv_up, ckvn, sink, lam, gsub, layer, depth, lam_init, ctx, n_tok, states):
    nb = ctx.rows // SEQ
    wide = 8 * HEAD_DIM

    def blk(col, w=wide):
        return pl.BlockSpec((SEQ, w), lambda b: (b, col // w))

    def small(shape):
        return pl.BlockSpec((None,) + shape, lambda b: (layer,) + (0,) * len(shape))

    def state(*tail):
        return pl.BlockSpec((None, None, SEQ) + tail, lambda b: (b, layer, 0) + (0,) * len(tail))

    def body(aq, ak, av, bq, bk, bv, sink_ref, cqn, cqp, ckv, ckpe, dq, dk, dv, lam_ref, gsub_ref,
             gp_a, gp_b, gp_c, gp_d, ckvn_ref, o_ref, *state_refs):
        def out(mixer):
            return o_ref.at[:, mixer * wide:(mixer + 1) * wide]

        _ctx_a_kernel(aq, ak, av, gp_a, out(0))
        _ctx_b_kernel(bq, bk, bv, sink_ref, gp_b, out(1), layer=layer)
        _ctx_c_kernel(cqn, cqp, ckv, ckpe, gp_c, out(2))
        _ctx_d_kernel(dq, dk, dv, lam_ref, gsub_ref, gp_d, out(3), lam_init=lam_init)
        _state_kernel(ak, av, bk, bv, ckvn_ref, ckpe, dk, dv, *state_refs)

    outs = _aliased_call(
        body, 21, list(states), grid=(nb,),
        in_specs=[blk(A_NQ), blk(A_NK), blk(A_NV),
                  blk(A_SWQ), blk(A_SWK, 256), blk(A_SWV, 256), pl.BlockSpec(memory_space=pltpu.SMEM),
                  blk(0), blk(wide), blk(0, 2 * wide), blk(A_KPE, HEAD_DIM),
                  blk(B_DQ), blk(B_DK), blk(B_DV), small((4, HEAD_DIM)), small((1, 2 * HEAD_DIM))]
                 + [blk(B_GP + mixer * wide) for mixer in range(4)]
                 + [pl.BlockSpec((SEQ, MLA_KV_LORA), lambda b: (b, 0))],
        out_specs=[pl.BlockSpec((SEQ, 4 * wide), lambda b: (b, 0))] + [state(*t) for t in STATE_TAILS],
        out_shape=[jax.ShapeDtypeStruct((n_tok, D_MODEL), BF16)]
                  + [jax.ShapeDtypeStruct((nb, depth, SEQ) + t, F32) for t in STATE_TAILS],
        compiler_params=_cparams(1), name="ctx_mixers",
    )(p, p, p, p, p, p, sink, q_up, q_up, kv_up, p, p, p, p, lam, gsub, p, p, p, p, ckvn,
      *states)
    return outs[0], tuple(outs[1:])


def _nat_win_start(n):
    n_rows = DEC_SEQ // GRID_W
    return min(max(2 * n - NAT_KR // 2, 0), n_rows - NAT_SPAN)


def _nat_pattern_key(n):
    n_rows = DEC_SEQ // GRID_W
    q_per = TQ_WIN // GRID_W
    s = _nat_win_start(n)
    return tuple((q_per * n + j - s, min(max(q_per * n + j - NAT_KR // 2, 0), n_rows - NAT_KR) - s)
                 for j in range(q_per))


NAT_PATTERN_STEPS = tuple(sorted({_nat_pattern_key(n): n for n in reversed(range(DEC_SEQ // TQ_WIN))}.values()))
NAT_PATTERN = tuple([_nat_pattern_key(m) for m in NAT_PATTERN_STEPS].index(_nat_pattern_key(n))
                    for n in range(DEC_SEQ // TQ_WIN))


def _lat_a_kernel(q_ref, k_ref, v_ref, ck_ref, cv_ref, bias_ref, gp_ref, o_ref):
    head_rows = pl.ds(pl.program_id(1), PAST_LEN, stride=NAT_HEADS)
    k = k_ref[...].astype(BF16)
    v = v_ref[...].astype(BF16)
    ck = ck_ref[head_rows, :].astype(BF16)
    cv = cv_ref[head_rows, :].astype(BF16)

    def scores(n):
        win = slice(_nat_win_start(n) * GRID_W, (_nat_win_start(n) + NAT_SPAN) * GRID_W)
        q = (q_ref[_head(n, TQ_WIN), :] * QSCALE).astype(BF16)
        return [(_nt(q, ck), cv), (_nt(q, k[win]) + bias_ref[NAT_PATTERN[n]], v[win])]

    def finish(n, parts):
        rows = _head(n, TQ_WIN)
        o_ref[rows, :] = _gated(_softmax_pv(parts), gp_ref[rows, :])

    _pipelined(DEC_SEQ // TQ_WIN, scores, finish, ahead=2)


def _swa_win_start(n):
    return min(max(n - 1, 0), DEC_SEQ // TQ_WIN - 3) * TQ_WIN


def _lat_b_kernel(q_ref, k_ref, v_ref, ck_ref, cv_ref, sink_ref, mask_ref, cos_ref, sl_ref, sh_ref,
                  gp_ref, o_ref, *, layer):
    k = _rope(k_ref[...], cos_ref[...], sl_ref[...], sh_ref[...], 32).astype(BF16)
    v = v_ref[...].astype(BF16)
    ck = ck_ref[...].astype(BF16)
    cv = cv_ref[...].astype(BF16)
    sink = sink_ref[layer, pl.program_id(1)] * LOG2E

    def scores(n):
        rows = _head(n, TQ_WIN)
        w0 = _swa_win_start(n)
        win = slice(w0, w0 + 3 * TQ_WIN)
        q = _rope(q_ref[rows, :], cos_ref[rows, :], sl_ref[rows, :], sh_ref[rows, :], 32)
        q = (q * QSCALE).astype(BF16)
        return [(_nt(q, ck), cv), (_nt(q, k[win]) + mask_ref[n - w0 // TQ_WIN], v[win])]

    def finish(n, parts):
        rows = _head(n, TQ_WIN)
        o_ref[rows, :] = _gated(_softmax_pv(parts, extra=sink), gp_ref[rows, :])

    _pipelined(DEC_SEQ // TQ_WIN, scores, finish, ahead=2)


def _lat_c_kernel(qn_ref, qp_ref, kn_ref, v_ref, kpe_ref, ckn_ref, cv_ref, ckpe_ref,
                  cos_ref, sl_ref, sh_ref, gp_ref, o_ref):
    kpe = _rope(kpe_ref[...], cos_ref[...], sl_ref[...], sh_ref[...], 16)
    keys = jnp.concatenate([
        jnp.concatenate([ckn_ref[...], ckpe_ref[...].astype(BF16)], axis=-1),
        jnp.concatenate([kn_ref[...], kpe.astype(BF16)], axis=-1)], axis=0)
    vals = jnp.concatenate([cv_ref[...], v_ref[...]], axis=0)

    def scores(n):
        rows = _head(n, TQ_DENSE)
        qp = _rope(qp_ref[rows, :], cos_ref[rows, :], sl_ref[rows, :], sh_ref[rows, :], 16)
        q = (jnp.concatenate([qn_ref[rows, :], qp], axis=-1) * QSCALE_MLA).astype(BF16)
        return [(_nt(q, keys), vals)]

    def finish(n, parts):
        rows = _head(n, TQ_DENSE)
        o_ref[rows, :] = _gated(_softmax_pv(parts), gp_ref[rows, :])

    _pipelined(DEC_SEQ // TQ_DENSE, scores, finish, ahead=2)


def _lat_d_kernel(q1_ref, q2_ref, k1_ref, k2_ref, v_ref, ck1_ref, ck2_ref, cv_ref,
                  lam_ref, gsub_ref, cos_ref, sl_ref, sh_ref, gp_ref, o_ref, *, lam_init):
    cos, sl, sh = cos_ref[...], sl_ref[...], sh_ref[...]
    keys1 = jnp.concatenate([ck1_ref[...], _rope(k1_ref[...], cos, sl, sh, 32)], axis=0).astype(BF16)
    keys2 = jnp.concatenate([ck2_ref[...], _rope(k2_ref[...], cos, sl, sh, 32)], axis=0).astype(BF16)
    vals = jnp.concatenate([cv_ref[...], v_ref[...]], axis=0).astype(BF16)
    lam = _diff_lambda(lam_ref, lam_init)
    gsub = gsub_ref[...]
    first_map = {}

    def scores(t):
        rows = _head(t // 2, TQ_DENSE)
        q_ref, keys = ((q1_ref, keys1), (q2_ref, keys2))[t % 2]
        q = _rope(q_ref[rows, :], cos_ref[rows, :], sl_ref[rows, :], sh_ref[rows, :], 32)
        return [(_nt((q * QSCALE).astype(BF16), keys), vals)]

    def finish(t, parts):
        o = _softmax_pv(parts)
        if t % 2 == 0:
            first_map[t // 2] = o
            return
        rows = _head(t // 2, TQ_DENSE)
        o = _diff_finish(first_map.pop(t // 2), o, lam, gsub, lam_init)
        o_ref[rows, :] = _gated(o, gp_ref[rows, :])

    _pipelined(2 * (DEC_SEQ // TQ_DENSE), scores, finish)


def _lat_attention(mixed, p, q_up, kv_up, kv_cache, caches, layer, nat_bias, swa_mask, sink, lam,
                   gsub, lam_init, rope128, rope64, lat, n_tok):
    nb = lat.rows // DEC_SEQ
    rb0 = lat.row0 // DEC_SEQ
    c_nat_k, c_nat_v, c_swa_k, c_swa_v, c_kpe, c_dk, c_dv = caches
    hd = HEAD_DIM

    def rows(col_of, w=hd):
        return pl.BlockSpec((DEC_SEQ, w), lambda b, h: (rb0 + b, col_of(h)))

    def cache(col_of, w=hd):
        return pl.BlockSpec((None, None, PAST_LEN, w), lambda b, h: (b, layer, 0, col_of(h)))

    def table():
        return pl.BlockSpec((DEC_SEQ, hd), lambda b, h: (0, 0))

    def small(shape):
        return pl.BlockSpec((None,) + shape, lambda b, h: (layer,) + (0,) * len(shape))

    all_heads = pl.BlockSpec((None, None, PAST_LEN * NAT_HEADS, hd), lambda b, h: (b, layer, 0, 0))

    def call(body, n_in, heads, in_specs, args, mixed, w, col0, name):
        return _aliased_call(
            body, n_in, [mixed], grid=(nb, heads), in_specs=in_specs,
            out_specs=rows(lambda h: col0 // w + h, w),
            out_shape=jax.ShapeDtypeStruct((n_tok, D_MODEL), BF16),
            compiler_params=_cparams(2), name=name,
        )(*args, mixed)

    mixed = call(
        _lat_a_kernel, 7, 8,
        [rows(lambda h: A_NQ // hd + h), rows(lambda h: A_NK // hd + h), rows(lambda h: A_NV // hd + h),
         all_heads, all_heads,
         pl.BlockSpec((None, None, len(NAT_PATTERN_STEPS), TQ_WIN, NAT_SPAN * GRID_W),
                      lambda b, h: (layer, h, 0, 0, 0)),
         rows(lambda h: B_GP // hd + h)],
        (p, p, p, c_nat_k, c_nat_v, nat_bias, p), mixed, hd, 0, "lat_nat")
    mixed = call(
        functools.partial(_lat_b_kernel, layer=layer), 11, 8,
        [rows(lambda h: A_SWQ // hd + h), rows(lambda h: A_SWK // hd + h // 4),
         rows(lambda h: A_SWV // hd + h // 4), cache(lambda h: h // 4), cache(lambda h: h // 4),
         pl.BlockSpec(memory_space=pltpu.SMEM),
         pl.BlockSpec((3, TQ_WIN, 3 * TQ_WIN), lambda b, h: (0, 0, 0)), table(), table(), table(),
         rows(lambda h: B_GP // hd + 8 + h)],
        (p, p, p, c_swa_k, c_swa_v, sink, swa_mask, *rope128, p), mixed, hd, 8 * hd, "lat_swa")
    mixed = call(
        _lat_c_kernel, 12, 8,
        [rows(lambda h: h), rows(lambda h: 8 + h),
         rows(lambda h: 2 * h), rows(lambda h: 2 * h + 1), rows(lambda h: A_KPE // hd),
         pl.BlockSpec((PAST_LEN, hd), lambda b, h: (b, 2 * h)),
         pl.BlockSpec((PAST_LEN, hd), lambda b, h: (b, 2 * h + 1)),
         cache(lambda h: 0), table(), table(), table(),
         rows(lambda h: B_GP // hd + 16 + h)],
        (q_up, q_up, kv_up, kv_up, p, kv_cache, kv_cache, c_kpe, *rope64, p), mixed, hd, 16 * hd,
        "lat_mla")
    mixed = call(
        functools.partial(_lat_d_kernel, lam_init=lam_init), 14, 4,
        [rows(lambda h: B_DQ // hd + 2 * h), rows(lambda h: B_DQ // hd + 2 * h + 1),
         rows(lambda h: B_DK // hd + 2 * h), rows(lambda h: B_DK // hd + 2 * h + 1),
         rows(lambda h: B_DV // (2 * hd) + h, 2 * hd),
         cache(lambda h: 2 * h), cache(lambda h: 2 * h + 1), cache(lambda h: h, 2 * hd),
         small((4, hd)), small((1, 2 * hd)), table(), table(), table(),
         rows(lambda h: B_GP // (2 * hd) + 12 + h, 2 * hd)],
        (p, p, p, p, p, c_dk, c_dk, c_dv, lam, gsub, *rope128, p), mixed, 2 * hd, 24 * hd,
        "lat_diff")
    return mixed


def _state_kernel(nk_ref, nv_ref, sk_ref, sv_ref, ckv_ref, kpe_ref, dk_ref, dv_ref,
                  o_nk, o_nv, o_sk, o_sv, o_ckv, o_kpe, o_dk, o_dv):
    for src, dst, heads, w in ((nk_ref, o_nk, 8, HEAD_DIM), (nv_ref, o_nv, 8, HEAD_DIM),
                               (sk_ref, o_sk, 2, HEAD_DIM), (sv_ref, o_sv, 2, HEAD_DIM),
                               (dk_ref, o_dk, 4, 2 * HEAD_DIM), (dv_ref, o_dv, 4, 2 * HEAD_DIM)):
        for h in range(heads):
            dst[:, h, :] = src[:, h * w:(h + 1) * w]
    o_ckv[...] = ckv_ref[...]
    o_kpe[...] = kpe_ref[:, :MLA_ROPE]


def _rope_tables(rot_dim):
    axis_dim = rot_dim // 2
    inv = 1.0 / (ROPE_BASE ** (jnp.arange(0, axis_dim, 2, dtype=F32) / axis_dim))
    t = jnp.arange(DEC_SEQ)
    ang_row = (t // GRID_W).astype(F32)[:, None] * inv
    ang_col = (t % GRID_W).astype(F32)[:, None] * inv
    zeros = jnp.zeros_like(ang_row)
    pad = HEAD_DIM - rot_dim
    cos = jnp.concatenate([jnp.cos(ang_row)] * 2 + [jnp.cos(ang_col)] * 2
                          + [jnp.ones((DEC_SEQ, pad), F32)], axis=-1)
    sin_lo = jnp.concatenate([-jnp.sin(ang_row), zeros, -jnp.sin(ang_col), zeros,
                              jnp.zeros((DEC_SEQ, pad), F32)], axis=-1)
    sin_hi = jnp.concatenate([zeros, jnp.sin(ang_row), zeros, jnp.sin(ang_col),
                              jnp.zeros((DEC_SEQ, pad), F32)], axis=-1)
    return cos, sin_lo, sin_hi


def _nat_bias(rpb):
    n_rows = DEC_SEQ // GRID_W
    steps = len(NAT_PATTERN_STEPS)
    q_per = TQ_WIN // GRID_W
    rep = jnp.asarray(NAT_PATTERN_STEPS)
    qr = rep[:, None] * q_per + jnp.arange(q_per)[None, :]
    kr = jnp.asarray([_nat_win_start(n) for n in NAT_PATTERN_STEPS])[:, None] + jnp.arange(NAT_SPAN)[None, :]
    c = jnp.arange(GRID_W)
    row0 = jnp.clip(qr - NAT_KR // 2, 0, n_rows - NAT_KR)
    col0 = jnp.clip(c - NAT_KC // 2, 0, GRID_W - NAT_KC)
    ok_r = (kr[:, None, :] >= row0[:, :, None]) & (kr[:, None, :] < row0[:, :, None] + NAT_KR)
    ok_c = (c[None, :] >= col0[:, None]) & (c[None, :] < col0[:, None] + NAT_KC)
    dr = jnp.clip(kr[:, None, :] - qr[:, :, None], -(NAT_KR - 1), NAT_KR - 1) + (NAT_KR - 1)
    dc = jnp.clip(c[None, :] - c[:, None], -(NAT_KC - 1), NAT_KC - 1) + (NAT_KC - 1)
    dr, ok_r = jnp.repeat(dr, GRID_W, axis=-1), jnp.repeat(ok_r, GRID_W, axis=-1)
    dc, ok_c = jnp.tile(dc, (1, NAT_SPAN)), jnp.tile(ok_c, (1, NAT_SPAN))
    sel_r = jax.nn.one_hot(dr, 2 * NAT_KR - 1, dtype=F32)
    sel_c = jax.nn.one_hot(dc, 2 * NAT_KC - 1, dtype=F32)
    bias = jnp.einsum("nqKr,lhrc,xKc->lhnqxK", sel_r, rpb.astype(F32), sel_c,
                      precision=lax.Precision.HIGHEST)
    ok = ok_r[:, :, None, :] & ok_c[None, None, :, :]
    bias = jnp.where(ok[None, None], bias * LOG2E, NEG_INF)
    return bias.reshape(rpb.shape[0], rpb.shape[1], steps, TQ_WIN, NAT_SPAN * GRID_W)


def _swa_mask():
    qi = jnp.arange(TQ_WIN)[None, :, None] + TQ_WIN * jnp.arange(3)[:, None, None]
    kj = jnp.arange(3 * TQ_WIN)[None, None, :]
    return jnp.where(jnp.abs(qi - kj) <= SWA_WINDOW, 0.0, NEG_INF).astype(F32)


def _pack_w_uq(w_uq):
    depth = w_uq.shape[0]
    w = w_uq.reshape(depth, MLA_Q_LORA, 8, MLA_NOPE + MLA_ROPE)
    nope = w[..., :MLA_NOPE].reshape(depth, MLA_Q_LORA, 8 * MLA_NOPE)
    pe = jnp.pad(w[..., MLA_NOPE:], ((0, 0), (0, 0), (0, 0), (0, HEAD_DIM - MLA_ROPE)))
    return jnp.concatenate([nope, pe.reshape(depth, MLA_Q_LORA, 8 * HEAD_DIM)], axis=-1).astype(BF16)


def kernel(x_prompt, x_sample, cache_nat_k, cache_nat_v, cache_swa_k, cache_swa_v, cache_mla_ckv, cache_mla_kpe, cache_diff_k, cache_diff_v, c, c_ctx, w_mod, b_mod, g_pre, g_post, w_in, w_out, nat_rpb, swa_sink, mla_g_q, mla_g_kv, mla_w_uq, mla_w_ukv, diff_lambda, diff_g_subln):
    depth = w_mod.shape[0]
    bc, bl = x_prompt.shape[0], x_sample.shape[0]
    n_ctx, n_lat = bc * SEQ, bl * DEC_SEQ
    n_tok = n_ctx + n_lat
    assert n_ctx % DEC_SEQ == 0 and x_prompt.shape[1] == SEQ and x_sample.shape[1] == DEC_SEQ
    ctx = Group(0, n_ctx, lambda i: 0)
    lat = Group(n_ctx, n_lat, lambda i: 1 + i // (DEC_SEQ // ROW_BLOCK))

    n_cond = -(-(1 + bl) // 8) * 8
    cond = jnp.concatenate([c_ctx[None], c, jnp.zeros((n_cond - 1 - bl, D_MODEL), F32)], axis=0)
    mod = _modulation(cond, w_mod, b_mod).reshape(depth, n_cond, 3, D_MODEL)

    w_t = jnp.swapaxes(w_in, 1, 2)
    w_p = _cast_rows(w_t, B_START, N_B, PACK_ROWS_B, 0, N_PROJ, None, "pack_w_b")
    w_p = _cast_rows(w_t, 0, N_A, PACK_ROWS_A, N_B, N_PROJ, w_p, "pack_w_a")
    w_out_b = w_out.astype(BF16)
    w_uq_p = _pack_w_uq(mla_w_uq)
    w_ukv_b = mla_w_ukv.astype(BF16)
    g_pre3 = g_pre.reshape(depth, 1, D_MODEL)
    g_post3 = g_post.reshape(depth, 1, D_MODEL)
    g_q3 = mla_g_q.reshape(depth, 1, MLA_Q_LORA)
    g_kv3 = mla_g_kv.reshape(depth, 1, MLA_KV_LORA)
    gsub3 = diff_g_subln.reshape(depth, 1, 2 * HEAD_DIM)
    rope128 = _rope_tables(HEAD_DIM)
    rope64 = _rope_tables(MLA_ROPE)
    nat_bias = _nat_bias(nat_rpb)
    swa_mask = _swa_mask()

    caches = (cache_nat_k.reshape(bl, depth, PAST_LEN * NAT_HEADS, HEAD_DIM),
              cache_nat_v.reshape(bl, depth, PAST_LEN * NAT_HEADS, HEAD_DIM),
              cache_swa_k.reshape(bl, depth, PAST_LEN, -1), cache_swa_v.reshape(bl, depth, PAST_LEN, -1),
              jnp.pad(cache_mla_kpe, ((0, 0), (0, 0), (0, 0), (0, HEAD_DIM - MLA_ROPE))),
              cache_diff_k.reshape(bl, depth, PAST_LEN, -1), cache_diff_v.reshape(bl, depth, PAST_LEN, -1))

    x_c = x_prompt.reshape(n_ctx, D_MODEL)
    x_l = x_sample.reshape(n_lat, D_MODEL)
    h = _norm_mod(x_c, g_pre3, mod, 0, ctx, n_tok, None)
    h = _norm_mod(x_l, g_pre3, mod, 0, lat, n_tok, h)
    states = ()
    for l in range(depth):
        lam_init = 0.8 - 0.6 * math.exp(-0.3 * l)
        p = _matmul_nt(h, w_p, l, IN_PROJ_TN, "in_proj")
        q_up = _mla_q_up(p, g_q3, w_uq_p, l)
        ckvn, kv_up = _mla_kv_up_tokens(p, g_kv3, w_ukv_b, l)
        kv_cache = _mla_kv_up_cache(cache_mla_ckv, w_ukv_b, l)

        mixed, states = _ctx_attention(p, q_up, kv_up, ckvn, swa_sink, diff_lambda, gsub3, l, depth,
                                       lam_init, ctx, n_tok, states)
        mixed = _lat_attention(mixed, p, q_up, kv_up, kv_cache, caches, l, nat_bias, swa_mask, swa_sink,
                               diff_lambda, gsub3, lam_init, rope128, rope64, lat, n_tok)
        o = _matmul(mixed, w_out_b, l, OUT_PROJ_TN, "out_proj", out_dtype=BF16)
        if l + 1 < depth:
            x_c, h = _post_norm(x_c, o, g_post3, g_pre3, mod, l, ctx, n_tok, None)
            x_l, h = _post_norm(x_l, o, g_post3, g_pre3, mod, l, lat, n_tok, h)
        else:
            x_c = _post(x_c, o, g_post3, mod, l, ctx)
            x_l = _post(x_l, o, g_post3, mod, l, lat)

    return (x_c.reshape(bc, SEQ, D_MODEL), x_l.reshape(bl, DEC_SEQ, D_MODEL)) + tuple(states)
```
